```python
import jax, jax.numpy as jnp
from jax import lax
import numpy as np

D_MODEL = 1024
BATCH = 8
SEQ = 8192
DEPTH = 2

GRID_W = 64
CTX_LEN = 256
N_MOD = 9
EPS = 1e-6
ROPE_BASE = 10000.0
BLOCK = 128
NEG_INF = -1e30
D_FF = 2816

MLA_HEADS = 8
MLA_Q_RANK = 256
MLA_KV_RANK = 128
MLA_NOPE_DIM = 64
MLA_ROPE_DIM = 32
MLA_V_DIM = 64
MLA_QK_DIM = MLA_NOPE_DIM + MLA_ROPE_DIM
MLA_SCALE = MLA_QK_DIM ** -0.5

WIN_HEADS = 8
WIN_KV_HEADS = 2
WIN_GROUP = WIN_HEADS // WIN_KV_HEADS
WIN_HEAD_DIM = 64
WINDOW = 128
WIN_SCALE = WIN_HEAD_DIM ** -0.5

PAIR_KV_SIZES = (MLA_KV_RANK, MLA_ROPE_DIM, WIN_KV_HEADS * WIN_HEAD_DIM, WIN_KV_HEADS * WIN_HEAD_DIM)
PAIR_Q_SIZES = (MLA_Q_RANK, WIN_HEADS * WIN_HEAD_DIM)
PAIR_KV_COLS = sum(PAIR_KV_SIZES)
PAIR_IN = PAIR_KV_COLS + sum(PAIR_Q_SIZES)
PAIR_OUT = MLA_HEADS * MLA_V_DIM + WIN_HEADS * WIN_HEAD_DIM

GLA_HEADS = 4
GLA_DK = D_MODEL // 2
GLA_DV = D_MODEL
GLA_DK_HEAD = GLA_DK // GLA_HEADS
GLA_DV_HEAD = GLA_DV // GLA_HEADS
GLA_LOWRANK = 16
GLA_GATE_NORM = 16.0
GLA_CHUNK = 64
GLA_KV_SIZES = (GLA_DK, GLA_DV, GLA_LOWRANK, GLA_LOWRANK)
GLA_Q_SIZES = (GLA_DK, GLA_DV)
GLA_KV_COLS = sum(GLA_KV_SIZES)
GLA_IN = GLA_KV_COLS + sum(GLA_Q_SIZES)

kernel_name = "hybrid_mla_swa_gla_prefix_dit"


def rms_norm(x, g):
    xf = x.astype(jnp.float32)
    return xf * lax.rsqrt(jnp.mean(xf * xf, axis=-1, keepdims=True) + EPS) * g


def split_cols(z, sizes):
    out, start = [], 0
    for s in sizes:
        out.append(z[..., start:start + s])
        start += s
    return out


def swiglu(h, w_gu, w_down):
    gu = h @ w_gu
    return (jax.nn.silu(gu[..., :D_FF]) * gu[..., D_FF:]) @ w_down


def axial_rope_tables(rows, rot_dim):
    n_freq = rot_dim // 4
    inv = ROPE_BASE ** (-jnp.arange(n_freq, dtype=jnp.float32) / n_freq)
    t = jnp.arange(rows * GRID_W)
    row = (t // GRID_W).astype(jnp.float32)
    col = (t % GRID_W).astype(jnp.float32)
    ang = jnp.concatenate([row[:, None] * inv, col[:, None] * inv], axis=-1)
    return jnp.cos(ang), jnp.sin(ang)


def apply_rope(x, rope):
    cos, sin = rope
    cos = cos[None, :, None, :]
    sin = sin[None, :, None, :]
    x1 = x[..., 0::2]
    x2 = x[..., 1::2]
    return jnp.stack([x1 * cos - x2 * sin, x1 * sin + x2 * cos], axis=-1).reshape(x.shape)


def mla_kv(z_ckv, z_kr, p, rope):
    Bsz, T, _ = z_ckv.shape
    kv = (rms_norm(z_ckv, p["mla_g_kva"]) @ p["mla_w_ukv"]).reshape(Bsz, T, MLA_HEADS, MLA_NOPE_DIM + MLA_V_DIM)
    k_nope = rms_norm(kv[..., :MLA_NOPE_DIM], p["mla_g_kn"])
    v = kv[..., MLA_NOPE_DIM:]
    k_rope = rms_norm(z_kr, p["mla_g_kr"])[:, :, None, :]
    if rope is not None:
        k_rope = apply_rope(k_rope, rope)
    k = jnp.concatenate([k_nope, jnp.broadcast_to(k_rope, (Bsz, T, MLA_HEADS, MLA_ROPE_DIM))], axis=-1)
    return k, v


def mla_q(z_cq, p, rope):
    Bsz, T, _ = z_cq.shape
    q = (rms_norm(z_cq, p["mla_g_qa"]) @ p["mla_w_uq"]).reshape(Bsz, T, MLA_HEADS, MLA_QK_DIM)
    q_nope = rms_norm(q[..., :MLA_NOPE_DIM], p["mla_g_qn"])
    q_rope = rms_norm(q[..., MLA_NOPE_DIM:], p["mla_g_qr"])
    if rope is not None:
        q_rope = apply_rope(q_rope, rope)
    return jnp.concatenate([q_nope, q_rope], axis=-1)


def dense_block_attention(q, k, v):
    Bsz, S, H, dq = q.shape
    nb = S // BLOCK
    qb = jnp.swapaxes(q.reshape(Bsz, nb, BLOCK, H, dq), 0, 1)

    def one_block(qi):
        s = jnp.einsum("bqhd,bkhd->bhqk", qi, k).astype(jnp.float32) * MLA_SCALE
        pr = jax.nn.softmax(s, axis=-1)
        return jnp.einsum("bhqk,bkhd->bqhd", pr, v)

    o = lax.map(one_block, qb)
    return jnp.swapaxes(o, 0, 1).reshape(Bsz, S, H * v.shape[-1])


def context_attention(q, k, v):
    Bsz, T, H, _ = q.shape
    s = jnp.einsum("bqhd,bkhd->bhqk", q, k).astype(jnp.float32) * MLA_SCALE
    pr = jax.nn.softmax(s, axis=-1)
    return jnp.einsum("bhqk,bkhd->bqhd", pr, v).reshape(Bsz, T, H * v.shape[-1])


def win_kv(z_k, z_v, p, rope):
    Bsz, T, _ = z_k.shape
    k = rms_norm(z_k.reshape(Bsz, T, WIN_KV_HEADS, WIN_HEAD_DIM), p["win_g_k"])
    if rope is not None:
        k = apply_rope(k, rope)
    v = z_v.reshape(Bsz, T, WIN_KV_HEADS, WIN_HEAD_DIM)
    return k, v


def win_q(z_q, p, rope):
    Bsz, T, _ = z_q.shape
    q = rms_norm(z_q.reshape(Bsz, T, WIN_HEADS, WIN_HEAD_DIM), p["win_g_q"])
    if rope is not None:
        q = apply_rope(q, rope)
    return q.reshape(Bsz, T, WIN_KV_HEADS, WIN_GROUP, WIN_HEAD_DIM)


def window_block_attention(q, k, v, kc, vc, sink):
    Bsz, S, Hkv, G, d = q.shape
    nb = S // BLOCK
    n_ctx = kc.shape[1]
    pad = ((0, 0), (BLOCK, BLOCK), (0, 0), (0, 0))
    kp = jnp.pad(k, pad)
    vp = jnp.pad(v, pad)
    qb = jnp.swapaxes(q.reshape(Bsz, nb, BLOCK, Hkv, G, d), 0, 1)
    q_off = jnp.arange(BLOCK)
    k_off = jnp.arange(3 * BLOCK) - BLOCK
    band = jnp.abs(q_off[:, None] - k_off[None, :]) <= WINDOW
    sink_col = jnp.broadcast_to(sink.astype(jnp.float32)[None, :, :, None, None], (Bsz, Hkv, G, BLOCK, 1))

    def one_block(args):
        i, qi = args
        start = i * BLOCK
        kw = lax.dynamic_slice_in_dim(kp, start, 3 * BLOCK, axis=1)
        vw = lax.dynamic_slice_in_dim(vp, start, 3 * BLOCK, axis=1)
        k_abs = start + k_off
        valid = band & ((k_abs >= 0) & (k_abs < S))[None, :]
        s_win = jnp.einsum("bqngd,bknd->bngqk", qi, kw).astype(jnp.float32) * WIN_SCALE
        s_win = jnp.where(valid, s_win, NEG_INF)
        s_ctx = jnp.einsum("bqngd,bknd->bngqk", qi, kc).astype(jnp.float32) * WIN_SCALE
        pr = jax.nn.softmax(jnp.concatenate([s_ctx, s_win, sink_col], axis=-1), axis=-1)
        return (jnp.einsum("bngqk,bknd->bqngd", pr[..., :n_ctx], vc)
                + jnp.einsum("bngqk,bknd->bqngd", pr[..., n_ctx:n_ctx + 3 * BLOCK], vw))

    o = lax.map(one_block, (jnp.arange(nb), qb))
    return jnp.swapaxes(o, 0, 1).reshape(Bsz, S, Hkv * G * d)


def sink_context_attention(q, k, v, sink):
    Bsz, T, Hkv, G, d = q.shape
    s = jnp.einsum("bqngd,bknd->bngqk", q, k).astype(jnp.float32) * WIN_SCALE
    sink_col = jnp.broadcast_to(sink.astype(jnp.float32)[None, :, :, None, None], (Bsz, Hkv, G, T, 1))
    pr = jax.nn.softmax(jnp.concatenate([s, sink_col], axis=-1), axis=-1)
    return jnp.einsum("bngqk,bknd->bqngd", pr[..., :-1], v).reshape(Bsz, T, Hkv * G * d)


def attention_pair_mixer(h, hc, p, rope_mla, rope_win, ctx_out):
    Bsz, S, _ = h.shape
    z = h @ p["w_in"]
    ckv, kr, wk, wv, cq, wq = split_cols(z, PAIR_KV_SIZES + PAIR_Q_SIZES)
    zc = hc @ (p["w_in"] if ctx_out else p["w_in"][:, :PAIR_KV_COLS])
    ckv_c, kr_c, wk_c, wv_c = split_cols(zc, PAIR_KV_SIZES)
    k_a, v_a = mla_kv(ckv, kr, p, rope_mla)
    kc_a, vc_a = mla_kv(ckv_c, kr_c, p, None)
    o_a = dense_block_attention(mla_q(cq, p, rope_mla),
                                jnp.concatenate([kc_a, k_a], axis=1),
                                jnp.concatenate([vc_a, v_a], axis=1))
    k_b, v_b = win_kv(wk, wv, p, rope_win)
    kc_b, vc_b = win_kv(wk_c, wv_c, p, None)
    o_b = window_block_attention(win_q(wq, p, rope_win), k_b, v_b, kc_b, vc_b, p["win_sink"])
    y = jnp.concatenate([o_a, o_b], axis=-1) @ p["w_out"]
    if not ctx_out:
        return y, None
    cq_c, wq_c = split_cols(zc[..., PAIR_KV_COLS:], PAIR_Q_SIZES)
    oc_a = context_attention(mla_q(cq_c, p, None), kc_a, vc_a)
    oc_b = sink_context_attention(win_q(wq_c, p, None), kc_b, vc_b, p["win_sink"])
    yc = jnp.concatenate([oc_a, oc_b], axis=-1) @ p["w_out"]
    return y, yc


def gla_chunked(q, k, v, g, s0):
    Bsz, T, H, _ = q.shape
    dv = v.shape[-1]
    n = T // GLA_CHUNK

    def to_chunks(a):
        return a.astype(jnp.float32).reshape(Bsz, n, GLA_CHUNK, H, a.shape[-1]).transpose(1, 0, 3, 2, 4)

    qc, kc, vc = to_chunks(q), to_chunks(k), to_chunks(v)
    bc = jnp.cumsum(to_chunks(g), axis=3)
    lower = jnp.tril(jnp.ones((GLA_CHUNK, GLA_CHUNK), dtype=bool))

    def step(s, inp):
        qq, kk, vv, bb = inp
        b_end = bb[:, :, -1:, :]
        q_dec = qq * jnp.exp(bb)
        k_inv = kk * jnp.exp(-bb)
        k_end = kk * jnp.exp(b_end - bb)
        a = jnp.where(lower, jnp.einsum("bhtk,bhsk->bhts", q_dec, k_inv), 0.0)
        o = jnp.einsum("bhts,bhsv->bhtv", a, vv) + jnp.einsum("bhtk,bhkv->bhtv", q_dec, s)
        s = jnp.exp(b_end[:, :, 0, :])[..., None] * s + jnp.einsum("bhsk,bhsv->bhkv", k_end, vv)
        return s, o

    s_final, o = lax.scan(step, s0.astype(jnp.float32), (qc, kc, vc, bc))
    return o.transpose(1, 0, 3, 2, 4).reshape(Bsz, T, H, dv), s_final


def gla_final_state(k, v, g):
    G = jnp.cumsum(g.astype(jnp.float32), axis=1)
    w = jnp.exp(G[:, -1:] - G)
    return jnp.einsum("bthk,bthv->bhkv", k.astype(jnp.float32) * w, v.astype(jnp.float32))


def gla_gate(z_low, w_up, b):
    return jax.nn.log_sigmoid((z_low @ w_up + b).astype(jnp.float32)) / GLA_GATE_NORM


def gla_mixer(h, hc, p, ctx_out):
    Bsz, S, _ = h.shape

    def heads(t):
        return t.reshape(t.shape[0], t.shape[1], GLA_HEADS, -1)

    def flip(t):
        return jnp.flip(t, axis=1)

    q_scale = GLA_DK_HEAD ** -0.5
    z = h @ p["w_in"]
    k, v, lf, lb, q, g = split_cols(z, GLA_KV_SIZES + GLA_Q_SIZES)
    k, v, q = heads(k), heads(v), heads(q) * q_scale
    gf = heads(gla_gate(lf, p["w_gk_f"], p["b_gk_f"]))
    gb = heads(gla_gate(lb, p["w_gk_b"], p["b_gk_b"]))
    zc = hc @ (p["w_in"] if ctx_out else p["w_in"][:, :GLA_KV_COLS])
    kc, vc, lfc, lbc = split_cols(zc, GLA_KV_SIZES)
    kc, vc = heads(kc), heads(vc)
    gfc = heads(gla_gate(lfc, p["w_gk_f"], p["b_gk_f"]))
    gbc = heads(gla_gate(lbc, p["w_gk_b"], p["b_gk_b"]))
    yc = None
    if ctx_out:
        qc_raw, g_c = split_cols(zc[..., GLA_KV_COLS:], GLA_Q_SIZES)
        qc = heads(qc_raw) * q_scale
        zero = jnp.zeros((Bsz, GLA_HEADS, GLA_DK_HEAD, GLA_DV_HEAD), jnp.float32)
        oc_f, s_f = gla_chunked(qc, kc, vc, gfc, zero)
        oc_b, s_b = gla_chunked(flip(qc), flip(kc), flip(vc), flip(gbc), zero)
        oc = rms_norm(oc_f + flip(oc_b), p["g_norm"]) * jax.nn.silu(heads(g_c))
        yc = oc.reshape(Bsz, hc.shape[1], GLA_DV) @ p["w_out"]
    else:
        s_f = gla_final_state(kc, vc, gfc)
        s_b = gla_final_state(flip(kc), flip(vc), flip(gbc))
    o_f, _ = gla_chunked(q, k, v, gf, s_f)
    o_b, _ = gla_chunked(flip(q), flip(k), flip(v), flip(gb), s_b)
    o = rms_norm(o_f + flip(o_b), p["g_norm"]) * jax.nn.silu(heads(g))
    y = o.reshape(Bsz, S, GLA_DV) @ p["w_out"]
    return y, yc


def pre_mod(t, m, norm_g, i):
    h = rms_norm(t, norm_g[i]) * (1.0 + m[..., 3 * i + 1, :]) + m[..., 3 * i, :]
    return h, m[..., 3 * i + 2, :]


def trunk_layer(x, xc, c, c_ctx, p, layer, ropes, ctx_out):
    Bsz = x.shape[0]
    mod = (jax.nn.silu(c) @ p["w_mod"] + p["b_mod"]).reshape(Bsz, 1, N_MOD, D_MODEL)
    mod_c = (jax.nn.silu(c_ctx) @ p["w_mod"] + p["b_mod"]).reshape(1, 1, N_MOD, D_MODEL)
    ng = p["norm_g"]
    h, gt = pre_mod(x, mod, ng, 0)
    hc, gtc = pre_mod(xc, mod_c, ng, 0)
    x = x + 0.5 * gt * swiglu(h, p["ffn1_w_gu"], p["ffn1_w_down"])
    xc = xc + 0.5 * gtc * swiglu(hc, p["ffn1_w_gu"], p["ffn1_w_down"])
    h, gt = pre_mod(x, mod, ng, 1)
    hc, gtc = pre_mod(xc, mod_c, ng, 1)
    if layer % 2 == 0:
        y, yc = attention_pair_mixer(h, hc, p, ropes[0], ropes[1], ctx_out)
    else:
        y, yc = gla_mixer(h, hc, p, ctx_out)
    x = x + gt * y
    h, gt = pre_mod(x, mod, ng, 2)
    x = x + 0.5 * gt * swiglu(h, p["ffn2_w_gu"], p["ffn2_w_down"])
    if ctx_out:
        xc = xc + gtc * yc
        hc, gtc = pre_mod(xc, mod_c, ng, 2)
        xc = xc + 0.5 * gtc * swiglu(hc, p["ffn2_w_gu"], p["ffn2_w_down"])
    return x, xc


def _dense(key, fan_in, shape, gain=1.0):
    return jax.random.normal(key, shape, jnp.float32) * (gain * fan_in ** -0.5)


def _gain(key, shape):
    return 1.0 + 0.05 * jax.random.normal(key, shape, jnp.float32)


def _common_params(key, pre):
    ks = jax.random.split(key, 7)
    return {
        pre + "norm_g": _gain(ks[0], (3, D_MODEL)),
        pre + "w_mod": _dense(ks[1], D_MODEL, (D_MODEL, N_MOD * D_MODEL), 0.5),
        pre + "b_mod": 0.02 * jax.random.normal(ks[2], (N_MOD * D_MODEL,), jnp.float32),
        pre + "ffn1_w_gu": _dense(ks[3], D_MODEL, (D_MODEL, 2 * D_FF)),
        pre + "ffn1_w_down": _dense(ks[4], D_FF, (D_FF, D_MODEL)),
        pre + "ffn2_w_gu": _dense(ks[5], D_MODEL, (D_MODEL, 2 * D_FF)),
        pre + "ffn2_w_down": _dense(ks[6], D_FF, (D_FF, D_MODEL)),
    }


def _pair_params(key, pre):
    ks = jax.random.split(key, 13)
    return {
        pre + "w_in": _dense(ks[0], D_MODEL, (D_MODEL, PAIR_IN)),
        pre + "mla_g_qa": _gain(ks[1], (MLA_Q_RANK,)),
        pre + "mla_g_kva": _gain(ks[2], (MLA_KV_RANK,)),
        pre + "mla_w_uq": _dense(ks[3], MLA_Q_RANK, (MLA_Q_RANK, MLA_HEADS * MLA_QK_DIM)),
        pre + "mla_w_ukv": _dense(ks[4], MLA_KV_RANK, (MLA_KV_RANK, MLA_HEADS * (MLA_NOPE_DIM + MLA_V_DIM))),
        pre + "mla_g_qn": _gain(ks[5], (MLA_NOPE_DIM,)),
        pre + "mla_g_qr": _gain(ks[6], (MLA_ROPE_DIM,)),
        pre + "mla_g_kn": _gain(ks[7], (MLA_NOPE_DIM,)),
        pre + "mla_g_kr": _gain(ks[8], (MLA_ROPE_DIM,)),
        pre + "win_g_q": _gain(ks[9], (WIN_HEAD_DIM,)),
        pre + "win_g_k": _gain(ks[10], (WIN_HEAD_DIM,)),
        pre + "win_sink": 0.5 * jax.random.normal(ks[11], (WIN_KV_HEADS, WIN_GROUP), jnp.float32),
        pre + "w_out": _dense(ks[12], PAIR_OUT, (PAIR_OUT, D_MODEL)),
    }


def _gla_params(key, pre):
    ks = jax.random.split(key, 7)
    return {
        pre + "w_in": _dense(ks[0], D_MODEL, (D_MODEL, GLA_IN)),
        pre + "w_gk_f": _dense(ks[1], GLA_LOWRANK, (GLA_LOWRANK, GLA_DK)),
        pre + "b_gk_f": 0.1 * jax.random.normal(ks[2], (GLA_DK,), jnp.float32),
        pre + "w_gk_b": _dense(ks[3], GLA_LOWRANK, (GLA_LOWRANK, GLA_DK)),
        pre + "b_gk_b": 0.1 * jax.random.normal(ks[4], (GLA_DK,), jnp.float32),
        pre + "g_norm": _gain(ks[5], (GLA_DV_HEAD,)),
        pre + "w_out": _dense(ks[6], GLA_DV, (GLA_DV, D_MODEL)),
    }


def setup_inputs(seed: int = 0) -> dict:
    key = jax.random.key(seed)
    k_x, k_c, k_ctx, k_cc, k_0a, k_0b, k_1a, k_1b = jax.random.split(key, 8)
    out = {
        "x": jax.random.normal(k_x, (BATCH, SEQ, D_MODEL), jnp.float32),
        "c": jax.random.normal(k_c, (BATCH, D_MODEL), jnp.float32),
        "ctx": jax.random.normal(k_ctx, (BATCH, CTX_LEN, D_MODEL), jnp.float32),
        "c_ctx": jax.random.normal(k_cc, (D_MODEL,), jnp.float32),
    }
    out.update(_common_params(k_0a, "l0_"))
    out.update(_pair_params(k_0b, "l0_"))
    out.update(_common_params(k_1a, "l1_"))
    out.update(_gla_params(k_1b, "l1_"))
    return out


def reference(x, c, ctx, c_ctx,
              l0_norm_g, l0_w_mod, l0_b_mod, l0_ffn1_w_gu, l0_ffn1_w_down, l0_ffn2_w_gu, l0_ffn2_w_down,
              l0_w_in, l0_mla_g_qa, l0_mla_g_kva, l0_mla_w_uq, l0_mla_w_ukv, l0_mla_g_qn, l0_mla_g_qr,
              l0_mla_g_kn, l0_mla_g_kr, l0_win_g_q, l0_win_g_k, l0_win_sink, l0_w_out,
              l1_norm_g, l1_w_mod, l1_b_mod, l1_ffn1_w_gu, l1_ffn1_w_down, l1_ffn2_w_gu, l1_ffn2_w_down,
              l1_w_in, l1_w_gk_f, l1_b_gk_f, l1_w_gk_b, l1_b_gk_b, l1_g_norm, l1_w_out):
    out_dtype = x.dtype
    rows = x.shape[1] // GRID_W
    ropes = (axial_rope_tables(rows, MLA_ROPE_DIM), axial_rope_tables(rows, WIN_HEAD_DIM))
    layers = [
        dict(norm_g=l0_norm_g, w_mod=l0_w_mod, b_mod=l0_b_mod,
             ffn1_w_gu=l0_ffn1_w_gu, ffn1_w_down=l0_ffn1_w_down,
             ffn2_w_gu=l0_ffn2_w_gu, ffn2_w_down=l0_ffn2_w_down,
             w_in=l0_w_in, mla_g_qa=l0_mla_g_qa, mla_g_kva=l0_mla_g_kva,
             mla_w_uq=l0_mla_w_uq, mla_w_ukv=l0_mla_w_ukv,
             mla_g_qn=l0_mla_g_qn, mla_g_qr=l0_mla_g_qr, mla_g_kn=l0_mla_g_kn, mla_g_kr=l0_mla_g_kr,
             win_g_q=l0_win_g_q, win_g_k=l0_win_g_k, win_sink=l0_win_sink, w_out=l0_w_out),
        dict(norm_g=l1_norm_g, w_mod=l1_w_mod, b_mod=l1_b_mod,
             ffn1_w_gu=l1_ffn1_w_gu, ffn1_w_down=l1_ffn1_w_down,
             ffn2_w_gu=l1_ffn2_w_gu, ffn2_w_down=l1_ffn2_w_down,
             w_in=l1_w_in, w_gk_f=l1_w_gk_f, b_gk_f=l1_b_gk_f, w_gk_b=l1_w_gk_b, b_gk_b=l1_b_gk_b,
             g_norm=l1_g_norm, w_out=l1_w_out),
    ]
    xc = ctx
    for layer in range(DEPTH):
        x, xc = trunk_layer(x, xc, c, c_ctx, layers[layer], layer, ropes, layer < DEPTH - 1)
    return x.astype(out_dtype)
```

```python
import functools

import numpy as np
import jax
import jax.numpy as jnp
from jax import lax
from jax.experimental import pallas as pl
from jax.experimental.pallas import tpu as pltpu

F32 = jnp.float32
BF16 = jnp.bfloat16

D_MODEL = 1024
GRID_W = 64
N_MOD = 9
EPS = 1e-6
ROPE_BASE = 10000.0
NEG_INF = -1e30
D_FF = 2816

MLA_HEADS = 8
MLA_Q_RANK = 256
MLA_KV_RANK = 128
MLA_NOPE_DIM = 64
MLA_ROPE_DIM = 32
MLA_V_DIM = 64
MLA_QK_DIM = MLA_NOPE_DIM + MLA_ROPE_DIM
MLA_SCALE = MLA_QK_DIM ** -0.5

WIN_HEADS = 8
WIN_KV_HEADS = 2
WIN_GROUP = WIN_HEADS // WIN_KV_HEADS
WIN_HEAD_DIM = 64
WINDOW = 128
WIN_BLOCK = 128
WIN_SCALE = WIN_HEAD_DIM ** -0.5

GLA_HEADS = 4
GLA_DK = D_MODEL // 2
GLA_DV = D_MODEL
GLA_DK_HEAD = GLA_DK // GLA_HEADS
GLA_DV_HEAD = GLA_DV // GLA_HEADS
GLA_LOWRANK = 16
GLA_GATE_NORM = 16.0
GLA_CHUNK = 64

LANES = 128
FF_CHUNK = 256
VMEM_LIMIT = 58 * 1024 * 1024

P0_CKV, P0_WK, P0_WV, P0_CQ, P0_WQ, P0_KR, P0_COLS = 0, 128, 256, 384, 640, 1152, 1280
P1_K, P1_V, P1_Q, P1_G, P1_LOW, P1_COLS = 0, 512, 1536, 2048, 3072, 3200


def _dot(a, b):
    return jnp.dot(a, b, preferred_element_type=F32)


def _dot_nt(a, b):
    return lax.dot_general(a, b, (((1,), (1,)), ((), ())), preferred_element_type=F32)


def _dot_tn(a, b):
    return lax.dot_general(a, b, (((0,), (0,)), ((), ())), preferred_element_type=F32)


def _params(*sem):
    return pltpu.CompilerParams(dimension_semantics=sem, vmem_limit_bytes=VMEM_LIMIT)


def _resident(shape):
    nd = len(shape)
    return pl.BlockSpec(shape, lambda *_: (0,) * nd, pipeline_mode=pl.Buffered(1))


def _silu(x):
    return x / (1.0 + jnp.exp(-x))


def _pre_mod(x, mod_ref, ng_ref, sub):
    shift = mod_ref[0, 3 * sub:3 * sub + 1, :]
    scale = mod_ref[0, 3 * sub + 1:3 * sub + 2, :]
    gate = mod_ref[0, 3 * sub + 2:3 * sub + 3, :]
    rinv = lax.rsqrt(jnp.mean(x * x, axis=-1, keepdims=True) + EPS)
    h = x * rinv * ng_ref[sub:sub + 1, :] * (1.0 + scale) + shift
    return h.astype(BF16), gate


def _swiglu(hb, wgu_ref, wd_ref, act_ref):
    for c in range(D_FF // FF_CHUNK):
        lo = c * FF_CHUNK
        g = _dot(hb, wgu_ref[:, lo:lo + FF_CHUNK])
        u = _dot(hb, wgu_ref[:, D_FF + lo:D_FF + lo + FF_CHUNK])
        act_ref[:, lo:lo + FF_CHUNK] = (_silu(g) * u).astype(BF16)
    return _dot(act_ref[...], wd_ref[...])


def _mod_kernel(c_ref, w_ref, b_ref, o_ref):
    a = _silu(c_ref[...]).astype(BF16)
    o_ref[...] = _dot(a, w_ref[...].astype(BF16)) + b_ref[...]


def _mod_table(cc, w_mod, b_mod):
    rows = cc.shape[0]
    n = w_mod.shape[1]
    bn = 1152
    return pl.pallas_call(
        _mod_kernel,
        grid=(n // bn,),
        in_specs=[pl.BlockSpec((rows, D_MODEL), lambda j: (0, 0)),
                  pl.BlockSpec((D_MODEL, bn), lambda j: (0, j)),
                  pl.BlockSpec((1, bn), lambda j: (0, j))],
        out_specs=pl.BlockSpec((rows, bn), lambda j: (0, j)),
        out_shape=jax.ShapeDtypeStruct((rows, n), F32),
        compiler_params=_params("arbitrary"),
        name="mod_table",
    )(cc, w_mod, b_mod.reshape(1, n))


def _ffn_kernel(x_ref, mod_ref, ng_ref, wgu_ref, wd_ref, o_ref, act_ref, *, sub):
    x = x_ref[0]
    hb, gate = _pre_mod(x, mod_ref, ng_ref, sub)
    o_ref[0] = x + 0.5 * gate * _swiglu(hb, wgu_ref, wd_ref, act_ref)


def _attn_out_ffn_kernel(x_ref, oa_ref, ob_ref, mod_ref, ng_ref, wo_ref, wgu_ref, wd_ref, o_ref, act_ref):
    half = oa_ref.shape[-1]
    y = _dot(oa_ref[0], wo_ref[:half, :]) + _dot(ob_ref[0], wo_ref[half:, :])
    x = x_ref[0] + mod_ref[0, 5:6, :] * y
    hb, gate = _pre_mod(x, mod_ref, ng_ref, 2)
    o_ref[0] = x + 0.5 * gate * _swiglu(hb, wgu_ref, wd_ref, act_ref)


def _gla_out_ffn_kernel(x_ref, of_ref, ob_ref, gt_ref, gn_ref, mod_ref, ng_ref, wo_ref, wgu_ref, wd_ref,
                        o_ref, act_ref, on_ref):
    for h in range(GLA_HEADS):
        hs = slice(h * GLA_DV_HEAD, (h + 1) * GLA_DV_HEAD)
        o = of_ref[0, :, hs] + ob_ref[0, :, hs]
        rinv = lax.rsqrt(jnp.mean(o * o, axis=-1, keepdims=True) + EPS)
        on_ref[:, hs] = (o * rinv * gn_ref[...] * gt_ref[0, :, hs]).astype(BF16)
    x = x_ref[0] + mod_ref[0, 5:6, :] * _dot(on_ref[...], wo_ref[...])
    hb, gate = _pre_mod(x, mod_ref, ng_ref, 2)
    o_ref[0] = x + 0.5 * gate * _swiglu(hb, wgu_ref, wd_ref, act_ref)


def _tok_spec(tm, width):
    return pl.BlockSpec((1, tm, width), lambda b, i: (b, i, 0))


def _mod_spec():
    return pl.BlockSpec((1, N_MOD, D_MODEL), lambda b, i: (b, 0, 0))


def _row_tile(t, pref=512):
    return pref if t % pref == 0 else t


def _ffn(x, mod, ng, wgu, wd, sub):
    bx, t, _ = x.shape
    tm = _row_tile(t)
    return pl.pallas_call(
        functools.partial(_ffn_kernel, sub=sub),
        grid=(bx, t // tm),
        in_specs=[_tok_spec(tm, D_MODEL), _mod_spec(), _resident((3, D_MODEL)),
                  _resident(wgu.shape), _resident(wd.shape)],
        out_specs=_tok_spec(tm, D_MODEL),
        out_shape=jax.ShapeDtypeStruct(x.shape, F32),
        scratch_shapes=[pltpu.VMEM((tm, D_FF), BF16)],
        compiler_params=_params("parallel", "parallel"),
        name="ffn",
    )(x, mod, ng, wgu, wd)


def _attn_out_ffn(x, oa, ob, mod, ng, wo, wgu, wd):
    bx, t, _ = x.shape
    tm = _row_tile(t)
    return pl.pallas_call(
        _attn_out_ffn_kernel,
        grid=(bx, t // tm),
        in_specs=[_tok_spec(tm, D_MODEL), _tok_spec(tm, oa.shape[-1]), _tok_spec(tm, ob.shape[-1]),
                  _mod_spec(), _resident((3, D_MODEL)), _resident(wo.shape),
                  _resident(wgu.shape), _resident(wd.shape)],
        out_specs=_tok_spec(tm, D_MODEL),
        out_shape=jax.ShapeDtypeStruct(x.shape, F32),
        scratch_shapes=[pltpu.VMEM((tm, D_FF), BF16)],
        compiler_params=_params("parallel", "parallel"),
        name="attn_out_ffn",
    )(x, oa, ob, mod, ng, wo, wgu, wd)


def _gla_out_ffn(x, of, ob, gt, gn, mod, ng, wo, wgu, wd):
    bx, t, _ = x.shape
    tm = _row_tile(t)
    return pl.pallas_call(
        _gla_out_ffn_kernel,
        grid=(bx, t // tm),
        in_specs=[_tok_spec(tm, D_MODEL), _tok_spec(tm, GLA_DV), _tok_spec(tm, GLA_DV), _tok_spec(tm, GLA_DV),
                  _resident(gn.shape), _mod_spec(), _resident((3, D_MODEL)), _resident(wo.shape),
                  _resident(wgu.shape), _resident(wd.shape)],
        out_specs=_tok_spec(tm, D_MODEL),
        out_shape=jax.ShapeDtypeStruct(x.shape, F32),
        scratch_shapes=[pltpu.VMEM((tm, D_FF), BF16), pltpu.VMEM((tm, GLA_DV), BF16)],
        compiler_params=_params("parallel", "parallel"),
        name="gla_out_ffn",
    )(x, of, ob, gt, gn, mod, ng, wo, wgu, wd)


def _lane(shape):
    return lax.broadcasted_iota(jnp.int32, shape, 1)


def _rope(x, cos, sin, lane, half):
    partner = jnp.where((lane & half) == 0, pltpu.roll(x, LANES - half, 1), pltpu.roll(x, half, 1))
    return x * cos + partner * sin


def _proj0_kernel(x_ref, mod_ref, ng_ref, win_ref, wukv_ref, wuq_ref, gv_ref, cm_ref, sm_ref, cw_ref, sw_ref,
                  qa_ref, ka_ref, va_ref, qw_ref, kw_ref, vw_ref):
    hb, _ = _pre_mod(x_ref[0], mod_ref, ng_ref, 1)
    z = _dot(hb, win_ref[...])
    tm = z.shape[0]
    lane = _lane((tm, LANES))
    lo_half = lane < 64
    cm, sm, cw, sw = cm_ref[...], sm_ref[...], cw_ref[...], sw_ref[...]
    g_kva, g_qa0, g_qa1 = gv_ref[0:1, :], gv_ref[1:2, :], gv_ref[2:3, :]
    g_kn, g_kr, g_q, g_wq, g_wk = gv_ref[3:4, :], gv_ref[4:5, :], gv_ref[5:6, :], gv_ref[6:7, :], gv_ref[7:8, :]

    def seg_rinv(v, n0, n1):
        sq = v * v
        s0 = jnp.sum(jnp.where(lo_half, sq, 0.0), axis=-1, keepdims=True)
        s1 = jnp.sum(jnp.where(lo_half, 0.0, sq), axis=-1, keepdims=True)
        return jnp.where(lo_half, lax.rsqrt(s0 * (1.0 / n0) + EPS), lax.rsqrt(s1 * (1.0 / n1) + EPS))

    ckv = z[:, P0_CKV:P0_CKV + MLA_KV_RANK]
    ckv = ckv * lax.rsqrt(jnp.mean(ckv * ckv, axis=-1, keepdims=True) + EPS) * g_kva
    kv = _dot(ckv.astype(BF16), wukv_ref[...])
    va_ref[0] = kv[:, MLA_HEADS * LANES:].astype(BF16)
    kr = z[:, P0_KR:P0_KR + LANES]
    kr = kr * lax.rsqrt(jnp.sum(kr * kr, axis=-1, keepdims=True) * (1.0 / MLA_ROPE_DIM) + EPS) * g_kr
    kr = _rope(kr, cm, sm, lane, 16)
    for h in range(MLA_HEADS):
        kn = kv[:, h * LANES:(h + 1) * LANES]
        rinv = lax.rsqrt(jnp.sum(kn * kn, axis=-1, keepdims=True) * (1.0 / MLA_NOPE_DIM) + EPS)
        ka_ref[0, :, h * LANES:(h + 1) * LANES] = (kn * rinv * g_kn + kr).astype(BF16)

    cq = z[:, P0_CQ:P0_CQ + MLA_Q_RANK]
    rq = lax.rsqrt(jnp.mean(cq * cq, axis=-1, keepdims=True) + EPS)
    cqn = jnp.concatenate([cq[:, :LANES] * rq * g_qa0, cq[:, LANES:] * rq * g_qa1], axis=-1).astype(BF16)
    q = _dot(cqn, wuq_ref[...])
    for h in range(MLA_HEADS):
        qh = q[:, h * LANES:(h + 1) * LANES]
        qh = qh * seg_rinv(qh, MLA_NOPE_DIM, MLA_ROPE_DIM) * g_q
        qa_ref[0, :, h * LANES:(h + 1) * LANES] = _rope(qh, cm, sm, lane, 16).astype(BF16)

    for j in range(WIN_HEADS // 2):
        qj = z[:, P0_WQ + j * LANES:P0_WQ + (j + 1) * LANES]
        qj = qj * seg_rinv(qj, WIN_HEAD_DIM, WIN_HEAD_DIM) * g_wq
        qw_ref[0, :, j * LANES:(j + 1) * LANES] = _rope(qj, cw, sw, lane, 32).astype(BF16)
    wk = z[:, P0_WK:P0_WK + LANES]
    wk = wk * seg_rinv(wk, WIN_HEAD_DIM, WIN_HEAD_DIM) * g_wk
    kw_ref[0] = _rope(wk, cw, sw, lane, 32).astype(BF16)
    vw_ref[0] = z[:, P0_WV:P0_WV + LANES].astype(BF16)


def _proj0(x, mod, ng, win, wukv, wuq, gv, tabs):
    bx, t, _ = x.shape
    tm = _row_tile(t)
    ntab = tabs[0].shape[0] // tm
    tab_spec = pl.BlockSpec((tm, LANES), lambda b, i: (i % ntab, 0))
    widths = (MLA_HEADS * LANES, MLA_HEADS * LANES, MLA_HEADS * MLA_V_DIM,
              WIN_HEADS * WIN_HEAD_DIM, WIN_KV_HEADS * WIN_HEAD_DIM, WIN_KV_HEADS * WIN_HEAD_DIM)
    return pl.pallas_call(
        _proj0_kernel,
        grid=(bx, t // tm),
        in_specs=[_tok_spec(tm, D_MODEL), _mod_spec(), _resident((3, D_MODEL)), _resident(win.shape),
                  _resident(wukv.shape), _resident(wuq.shape), _resident(gv.shape)] + [tab_spec] * 4,
        out_specs=[_tok_spec(tm, w) for w in widths],
        out_shape=[jax.ShapeDtypeStruct((bx, t, w), BF16) for w in widths],
        compiler_params=_params("parallel", "parallel"),
        name="proj0",
    )(x, mod, ng, win, wukv, wuq, gv, *tabs)


def _mla_attn_kernel(*refs, tk, has_latent):
    if has_latent:
        q_ref, kc_ref, vc_ref, k_ref, v_ref, o_ref, m_ref, l_ref, acc_ref = refs
    else:
        q_ref, kc_ref, vc_ref, o_ref, m_ref, l_ref, acc_ref = refs
    tq = q_ref.shape[1]
    lane = _lane((tq, LANES))
    lo_half = lane < 64

    def step(kblk, vblk, first):
        vlane = _lane(vblk.shape) < 64
        zero = jnp.zeros_like(vblk)
        vsel = (jnp.where(vlane, vblk, zero), jnp.where(vlane, zero, vblk))
        alphas, pvs = [], []
        for hh in range(2):
            s = _dot_nt(q_ref[0, :, hh * LANES:(hh + 1) * LANES], kblk[:, hh * LANES:(hh + 1) * LANES])
            m_cur = jnp.max(s, axis=-1, keepdims=True)
            if first:
                m_new = m_cur
            else:
                m_prev = m_ref[hh]
                m_new = jnp.maximum(m_prev, m_cur)
                alpha = jnp.exp(m_prev - m_new)
            p = jnp.exp(s - m_new)
            l_cur = jnp.sum(p, axis=-1, keepdims=True)
            if first:
                l_ref[hh] = l_cur
            else:
                l_ref[hh] = alpha * l_ref[hh] + l_cur
                alphas.append(alpha)
            m_ref[hh] = m_new
            pvs.append(_dot(p.astype(BF16), vsel[hh]))
        if first:
            acc_ref[...] = pvs[0] + pvs[1]
        else:
            acc_ref[...] = acc_ref[...] * jnp.where(lo_half, alphas[0], alphas[1]) + pvs[0] + pvs[1]

    step(kc_ref[0], vc_ref[0], True)
    if has_latent:
        def body(j, carry):
            r0 = pl.multiple_of(j * tk, tk)
            step(k_ref[0, pl.ds(r0, tk), :], v_ref[0, pl.ds(r0, tk), :], False)
            return carry
        lax.fori_loop(0, k_ref.shape[1] // tk, body, 0)
    o_ref[0] = (acc_ref[...] / jnp.where(lo_half, l_ref[0], l_ref[1])).astype(BF16)


def _mla_attn(q, kc, vc, k=None, v=None):
    bsz, t, _ = q.shape
    nc = kc.shape[1]
    has_latent = k is not None
    tq = _row_tile(t)
    tk = 512
    grid = (bsz, MLA_HEADS // 2, t // tq)
    in_specs = [pl.BlockSpec((1, tq, 2 * LANES), lambda b, h, i: (b, i, h)),
                pl.BlockSpec((1, nc, 2 * LANES), lambda b, h, i: (b, 0, h)),
                pl.BlockSpec((1, nc, LANES), lambda b, h, i: (b, 0, h))]
    args = [q, kc, vc]
    if has_latent:
        n = k.shape[1]
        tk = tk if n % tk == 0 else n
        in_specs += [pl.BlockSpec((1, n, 2 * LANES), lambda b, h, i: (b, 0, h)),
                     pl.BlockSpec((1, n, LANES), lambda b, h, i: (b, 0, h))]
        args += [k, v]
    return pl.pallas_call(
        functools.partial(_mla_attn_kernel, tk=tk, has_latent=has_latent),
        grid=grid,
        in_specs=in_specs,
        out_specs=pl.BlockSpec((1, tq, LANES), lambda b, h, i: (b, i, h)),
        out_shape=jax.ShapeDtypeStruct((bsz, t, MLA_HEADS * MLA_V_DIM), BF16),
        scratch_shapes=[pltpu.VMEM((2, tq, 1), F32), pltpu.VMEM((2, tq, 1), F32), pltpu.VMEM((tq, LANES), F32)],
        compiler_params=_params("parallel", "parallel", "arbitrary"),
        name="mla_attn" if has_latent else "mla_attn_ctx",
    )(*args)


def _win_attn_kernel(*refs, has_window, seq):
    if has_window:
        sink_ref, q_ref, kc_ref, vc_ref, kp_ref, kq_ref, kn_ref, vp_ref, vq_ref, vn_ref, o_ref = refs
        kall = jnp.concatenate([kc_ref[0], kp_ref[0], kq_ref[0], kn_ref[0]], axis=0)
        vall = jnp.concatenate([vc_ref[0], vp_ref[0], vq_ref[0], vn_ref[0]], axis=0)
    else:
        sink_ref, q_ref, kc_ref, vc_ref, o_ref = refs
        kall, vall = kc_ref[0], vc_ref[0]
    tq = q_ref.shape[1]
    nk = kall.shape[0]
    nc = kc_ref.shape[1]
    klane = _lane(kall.shape) < 64
    kzero = jnp.zeros_like(kall)
    ksel = (jnp.where(klane, kall, kzero), jnp.where(klane, kzero, kall))
    vsel = (jnp.where(klane, vall, kzero), jnp.where(klane, kzero, vall))
    if has_window:
        i = pl.program_id(1)
        col = lax.broadcasted_iota(jnp.int32, (tq, nk), 1)
        row = lax.broadcasted_iota(jnp.int32, (tq, nk), 0)
        rel = col - (nc + WIN_BLOCK) - row
        kpos = i * WIN_BLOCK + col - (nc + WIN_BLOCK)
        valid = (col < nc) | ((jnp.abs(rel) <= WINDOW) & (kpos >= 0) & (kpos < seq))
    for j in range(WIN_GROUP):
        qj = q_ref[0, :, j * LANES:(j + 1) * LANES]
        out = None
        for n in range(WIN_KV_HEADS):
            sink = sink_ref[n * WIN_GROUP + j]
            s = _dot_nt(qj, ksel[n])
            if has_window:
                s = jnp.where(valid, s, NEG_INF)
            m = jnp.maximum(jnp.max(s, axis=-1, keepdims=True), sink)
            p = jnp.exp(s - m)
            denom = jnp.sum(p, axis=-1, keepdims=True) + jnp.exp(sink - m)
            o = _dot(p.astype(BF16), vsel[n]) / denom
            out = o if out is None else out + o
        o_ref[0, :, j * LANES:(j + 1) * LANES] = out.astype(BF16)


def _win_attn(sink, q, kc, vc, k=None, v=None):
    bsz, t, _ = q.shape
    nc = kc.shape[1]
    has_window = k is not None
    nb = t // WIN_BLOCK
    kvw = WIN_KV_HEADS * WIN_HEAD_DIM
    in_specs = [pl.BlockSpec(memory_space=pltpu.SMEM),
                pl.BlockSpec((1, WIN_BLOCK, WIN_HEADS * WIN_HEAD_DIM), lambda b, i: (b, i, 0)),
                pl.BlockSpec((1, nc, kvw), lambda b, i: (b, 0, 0)),
                pl.BlockSpec((1, nc, kvw), lambda b, i: (b, 0, 0))]
    args = [sink, q, kc, vc]
    if has_window:
        prev_spec = pl.BlockSpec((1, WIN_BLOCK, kvw), lambda b, i: (b, jnp.maximum(i - 1, 0), 0))
        cur_spec = pl.BlockSpec((1, WIN_BLOCK, kvw), lambda b, i: (b, i, 0))
        next_spec = pl.BlockSpec((1, WIN_BLOCK, kvw), lambda b, i: (b, jnp.minimum(i + 1, nb - 1), 0))
        in_specs += [prev_spec, cur_spec, next_spec] * 2
        args += [k, k, k, v, v, v]
    return pl.pallas_call(
        functools.partial(_win_attn_kernel, has_window=has_window, seq=t),
        grid=(bsz, nb),
        in_specs=in_specs,
        out_specs=pl.BlockSpec((1, WIN_BLOCK, WIN_HEADS * WIN_HEAD_DIM), lambda b, i: (b, i, 0)),
        out_shape=jax.ShapeDtypeStruct((bsz, t, WIN_HEADS * WIN_HEAD_DIM), BF16),
        compiler_params=_params("parallel", "parallel"),
        name="win_attn" if has_window else "win_attn_ctx",
    )(*args)


def _log_sigmoid(x):
    return jnp.minimum(x, 0.0) - jnp.log1p(jnp.exp(-jnp.abs(x)))


def _proj1_kernel(x_ref, mod_ref, ng_ref, win_ref, wgk_ref, bgk_ref, k_ref, v_ref, q_ref, gt_ref, gf_ref, gb_ref):
    hb, _ = _pre_mod(x_ref[0], mod_ref, ng_ref, 1)
    z = _dot(hb, win_ref[...])
    k_ref[0] = z[:, P1_K:P1_K + GLA_DK]
    v_ref[0] = z[:, P1_V:P1_V + GLA_DV].astype(BF16)
    q_ref[0] = z[:, P1_Q:P1_Q + GLA_DK] * (GLA_DK_HEAD ** -0.5)
    gt_ref[0] = _silu(z[:, P1_G:P1_G + GLA_DV])
    pre = _dot(z[:, P1_LOW:P1_LOW + LANES].astype(BF16), wgk_ref[...]) + bgk_ref[...]
    gate = _log_sigmoid(pre) * (1.0 / GLA_GATE_NORM)
    gf_ref[0] = gate[:, :GLA_DK]
    gb_ref[0] = gate[:, GLA_DK:]


def _proj1(x, mod, ng, win, wgk, bgk):
    bx, t, _ = x.shape
    tm = _row_tile(t)
    outs = ((GLA_DK, F32), (GLA_DV, BF16), (GLA_DK, F32), (GLA_DV, F32), (GLA_DK, F32), (GLA_DK, F32))
    return pl.pallas_call(
        _proj1_kernel,
        grid=(bx, t // tm),
        in_specs=[_tok_spec(tm, D_MODEL), _mod_spec(), _resident((3, D_MODEL)), _resident(win.shape),
                  _resident(wgk.shape), _resident(bgk.shape)],
        out_specs=[_tok_spec(tm, w) for w, _ in outs],
        out_shape=[jax.ShapeDtypeStruct((bx, t, w), d) for w, d in outs],
        compiler_params=_params("parallel", "parallel"),
        name="proj1",
    )(x, mod, ng, win, wgk, bgk)


def _gla_chunk(q_ref, k_ref, v_ref, g_ref, o_ref, st_ref, r0, keep, ones, backward):
    rows = slice(r0, r0 + GLA_CHUNK)
    g = g_ref[0, rows, :]
    g_hi = g.astype(BF16)
    g_lo = (g - g_hi.astype(F32)).astype(BF16)
    bb = _dot(ones, g_hi) + _dot(ones, g_lo)
    b_end = bb[0:1, :] if backward else bb[GLA_CHUNK - 1:GLA_CHUNK, :]
    q = q_ref[0, rows, :]
    k = k_ref[0, rows, :]
    q_dec = (q * jnp.exp(bb)).astype(BF16)
    k_inv = (k * jnp.exp(-bb)).astype(BF16)
    k_end = (k * jnp.exp(b_end - bb)).astype(BF16)
    decay = jnp.exp(b_end)
    for h in range(GLA_HEADS):
        ks = slice(h * GLA_DK_HEAD, (h + 1) * GLA_DK_HEAD)
        vs = slice(h * GLA_DV_HEAD, (h + 1) * GLA_DV_HEAD)
        vh = v_ref[0, rows, vs]
        a = jnp.where(keep, _dot_nt(q_dec[:, ks], k_inv[:, ks]), 0.0).astype(BF16)
        st = st_ref[h]
        o_ref[0, rows, vs] = _dot(a, vh) + _dot_nt(q_dec[:, ks], st.astype(BF16))
        st_ref[h] = st * decay[:, ks] + _dot_tn(vh, k_end[:, ks])


def _gla_scan_kernel(qf_ref, kf_ref, vf_ref, gf_ref, qb_ref, kb_ref, vb_ref, gb_ref, s0f_ref, s0b_ref,
                     of_ref, ob_ref, sf_ref, sb_ref, stf_ref, stb_ref, *, n_sub):
    i = pl.program_id(1)

    @pl.when(i == 0)
    def _():
        stf_ref[...] = s0f_ref[0]
        stb_ref[...] = s0b_ref[0]

    row = lax.broadcasted_iota(jnp.int32, (GLA_CHUNK, GLA_CHUNK), 0)
    col = lax.broadcasted_iota(jnp.int32, (GLA_CHUNK, GLA_CHUNK), 1)
    lower = col <= row
    upper = col >= row
    ones_lo = jnp.where(lower, 1.0, 0.0).astype(BF16)
    ones_up = jnp.where(upper, 1.0, 0.0).astype(BF16)
    for c in range(n_sub):
        _gla_chunk(qf_ref, kf_ref, vf_ref, gf_ref, of_ref, stf_ref, c * GLA_CHUNK, lower, ones_lo, False)
    for c in reversed(range(n_sub)):
        _gla_chunk(qb_ref, kb_ref, vb_ref, gb_ref, ob_ref, stb_ref, c * GLA_CHUNK, upper, ones_up, True)

    @pl.when(i == pl.num_programs(1) - 1)
    def _():
        sf_ref[0] = stf_ref[...]
        sb_ref[0] = stb_ref[...]


def _gla_scan(q, k, v, gf, gb, s0f, s0b):
    bsz, t, _ = q.shape
    tb = 256 if t % 256 == 0 else GLA_CHUNK
    nblk = t // tb
    fwd = lambda w: pl.BlockSpec((1, tb, w), lambda b, i: (b, i, 0))
    bwd = lambda w: pl.BlockSpec((1, tb, w), lambda b, i: (b, nblk - 1 - i, 0))
    st_spec = pl.BlockSpec((1, GLA_HEADS, GLA_DV_HEAD, GLA_DK_HEAD), lambda b, i: (b, 0, 0, 0))
    st_shape = jax.ShapeDtypeStruct((bsz, GLA_HEADS, GLA_DV_HEAD, GLA_DK_HEAD), F32)
    return pl.pallas_call(
        functools.partial(_gla_scan_kernel, n_sub=tb // GLA_CHUNK),
        grid=(bsz, nblk),
        in_specs=[fwd(GLA_DK), fwd(GLA_DK), fwd(GLA_DV), fwd(GLA_DK),
                  bwd(GLA_DK), bwd(GLA_DK), bwd(GLA_DV), bwd(GLA_DK), st_spec, st_spec],
        out_specs=[fwd(GLA_DV), bwd(GLA_DV), st_spec, st_spec],
        out_shape=[jax.ShapeDtypeStruct((bsz, t, GLA_DV), F32)] * 2 + [st_shape] * 2,
        scratch_shapes=[pltpu.VMEM((GLA_HEADS, GLA_DV_HEAD, GLA_DK_HEAD), F32)] * 2,
        compiler_params=_params("parallel", "arbitrary"),
        name="gla_scan",
    )(q, k, v, gf, q, k, v, gb, s0f, s0b)


def _deinterleave(n):
    return np.concatenate([np.arange(0, n, 2), np.arange(1, n, 2)])


def _gather_cols(w, idx):
    idx = np.asarray(idx)
    cols = jnp.take(w, jnp.asarray(np.maximum(idx, 0)), axis=-1)
    return jnp.where(jnp.asarray(idx >= 0), cols, 0.0)


def _layer0_layout(p):
    ckv0, kr0 = 0, MLA_KV_RANK
    wk0 = kr0 + MLA_ROPE_DIM
    wv0 = wk0 + WIN_KV_HEADS * WIN_HEAD_DIM
    cq0 = wv0 + WIN_KV_HEADS * WIN_HEAD_DIM
    wq0 = cq0 + MLA_Q_RANK
    de64, de32 = _deinterleave(WIN_HEAD_DIM), _deinterleave(MLA_ROPE_DIM)
    idx = np.full((P0_COLS,), -1, np.int64)
    idx[P0_CKV:P0_CKV + MLA_KV_RANK] = ckv0 + np.arange(MLA_KV_RANK)
    for n in range(WIN_KV_HEADS):
        idx[P0_WK + n * 64:P0_WK + (n + 1) * 64] = wk0 + n * 64 + de64
    idx[P0_WV:P0_WV + 128] = wv0 + np.arange(128)
    idx[P0_CQ:P0_CQ + MLA_Q_RANK] = cq0 + np.arange(MLA_Q_RANK)
    for j in range(WIN_GROUP):
        for n in range(WIN_KV_HEADS):
            lo = P0_WQ + j * LANES + n * 64
            idx[lo:lo + 64] = wq0 + (n * WIN_GROUP + j) * 64 + de64
    idx[P0_KR + 64:P0_KR + 96] = kr0 + de32
    w_in = _gather_cols(p["w_in"], idx).astype(BF16)

    per = MLA_NOPE_DIM + MLA_V_DIM
    kidx = np.full((MLA_HEADS * LANES,), -1, np.int64)
    qidx = np.full((MLA_HEADS * LANES,), -1, np.int64)
    vidx = np.zeros((MLA_HEADS * MLA_V_DIM,), np.int64)
    for h in range(MLA_HEADS):
        kidx[h * LANES:h * LANES + 64] = h * per + np.arange(64)
        vidx[h * 64:(h + 1) * 64] = h * per + MLA_NOPE_DIM + np.arange(64)
        qidx[h * LANES:h * LANES + 64] = h * MLA_QK_DIM + np.arange(64)
        qidx[h * LANES + 64:h * LANES + 96] = h * MLA_QK_DIM + MLA_NOPE_DIM + de32
    w_ukv = _gather_cols(p["mla_w_ukv"], np.concatenate([kidx, vidx])).astype(BF16)
    w_uq = _gather_cols(p["mla_w_uq"], qidx).astype(BF16)

    z32, z64 = jnp.zeros((32,), F32), jnp.zeros((64,), F32)
    g_qr, g_kr = p["mla_g_qr"][de32], p["mla_g_kr"][de32]
    g_wq, g_wk = p["win_g_q"][de64], p["win_g_k"][de64]
    gv = jnp.stack([
        p["mla_g_kva"],
        p["mla_g_qa"][:LANES], p["mla_g_qa"][LANES:],
        jnp.concatenate([p["mla_g_kn"], z64]),
        jnp.concatenate([z64, g_kr, z32]),
        jnp.concatenate([p["mla_g_qn"], g_qr, z32]) * MLA_SCALE,
        jnp.concatenate([g_wq, g_wq]) * WIN_SCALE,
        jnp.concatenate([g_wk, g_wk]),
    ])

    out_rows = np.arange(D_MODEL)
    base = MLA_HEADS * MLA_V_DIM
    for j in range(WIN_GROUP):
        for n in range(WIN_KV_HEADS):
            lo = base + j * LANES + n * 64
            out_rows[lo:lo + 64] = base + (n * WIN_GROUP + j) * 64 + np.arange(64)
    w_out = p["w_out"][jnp.asarray(out_rows)].astype(BF16)
    return w_in, w_ukv, w_uq, gv, w_out


def _layer1_layout(p):
    k0, v0 = 0, GLA_DK
    lf0 = v0 + GLA_DV
    lb0 = lf0 + GLA_LOWRANK
    q0 = lb0 + GLA_LOWRANK
    g0 = q0 + GLA_DK
    idx = np.full((P1_COLS,), -1, np.int64)
    idx[P1_K:P1_K + GLA_DK] = k0 + np.arange(GLA_DK)
    idx[P1_V:P1_V + GLA_DV] = v0 + np.arange(GLA_DV)
    idx[P1_Q:P1_Q + GLA_DK] = q0 + np.arange(GLA_DK)
    idx[P1_G:P1_G + GLA_DV] = g0 + np.arange(GLA_DV)
    idx[P1_LOW:P1_LOW + GLA_LOWRANK] = lf0 + np.arange(GLA_LOWRANK)
    idx[P1_LOW + GLA_LOWRANK:P1_LOW + 2 * GLA_LOWRANK] = lb0 + np.arange(GLA_LOWRANK)
    w_in = _gather_cols(p["w_in"], idx).astype(BF16)
    wgk = jnp.zeros((LANES, 2 * GLA_DK), F32)
    wgk = wgk.at[:GLA_LOWRANK, :GLA_DK].set(p["w_gk_f"])
    wgk = wgk.at[GLA_LOWRANK:2 * GLA_LOWRANK, GLA_DK:].set(p["w_gk_b"])
    bgk = jnp.concatenate([p["b_gk_f"], p["b_gk_b"]]).reshape(1, 2 * GLA_DK)
    return w_in, wgk.astype(BF16), bgk


def _rope_tables(seq):
    t = jnp.arange(seq)
    row = (t // GRID_W).astype(F32)
    col = (t % GRID_W).astype(F32)

    def angles(rot_dim):
        n_freq = rot_dim // 4
        inv = ROPE_BASE ** (-jnp.arange(n_freq, dtype=F32) / n_freq)
        return jnp.concatenate([row[:, None] * inv, col[:, None] * inv], axis=-1)

    am, aw = angles(MLA_ROPE_DIM), angles(WIN_HEAD_DIM)
    one64, one32 = jnp.ones((seq, 64), F32), jnp.ones((seq, 32), F32)
    cm = jnp.concatenate([one64, jnp.cos(am), jnp.cos(am), one32], axis=-1)
    sm = jnp.concatenate([0 * one64, -jnp.sin(am), jnp.sin(am), 0 * one32], axis=-1)
    cw = jnp.concatenate([jnp.cos(aw)] * 4, axis=-1)
    sw = jnp.concatenate([-jnp.sin(aw), jnp.sin(aw)] * 2, axis=-1)
    return cm, sm, cw, sw


def _identity_tables(rows):
    one, zero = jnp.ones((rows, LANES), F32), jnp.zeros((rows, LANES), F32)
    return one, zero, one, zero


def kernel(x, c, ctx, c_ctx, l0_norm_g, l0_w_mod, l0_b_mod, l0_ffn1_w_gu, l0_ffn1_w_down, l0_ffn2_w_gu, l0_ffn2_w_down, l0_w_in, l0_mla_g_qa, l0_mla_g_kva, l0_mla_w_uq, l0_mla_w_ukv, l0_mla_g_qn, l0_mla_g_qr, l0_mla_g_kn, l0_mla_g_kr, l0_win_g_q, l0_win_g_k, l0_win_sink, l0_w_out, l1_norm_g, l1_w_mod, l1_b_mod, l1_ffn1_w_gu, l1_ffn1_w_down, l1_ffn2_w_gu, l1_ffn2_w_down, l1_w_in, l1_w_gk_f, l1_b_gk_f, l1_w_gk_b, l1_b_gk_b, l1_g_norm, l1_w_out):
    bsz, seq, _ = x.shape
    n_ctx = ctx.shape[1]
    out_dtype = x.dtype
    x = x.astype(F32)
    xc = ctx.astype(F32).reshape(1, bsz * n_ctx, D_MODEL)

    cc = jnp.zeros((16, D_MODEL), F32).at[:bsz].set(c).at[bsz].set(c_ctx)

    def mods(w_mod, b_mod):
        tab = _mod_table(cc, w_mod, b_mod).reshape(16, N_MOD, D_MODEL)
        return tab[:bsz], tab[bsz:bsz + 1]

    p0 = dict(w_in=l0_w_in, mla_g_qa=l0_mla_g_qa, mla_g_kva=l0_mla_g_kva, mla_w_uq=l0_mla_w_uq,
              mla_w_ukv=l0_mla_w_ukv, mla_g_qn=l0_mla_g_qn, mla_g_qr=l0_mla_g_qr, mla_g_kn=l0_mla_g_kn,
              mla_g_kr=l0_mla_g_kr, win_g_q=l0_win_g_q, win_g_k=l0_win_g_k, w_out=l0_w_out)
    w_in0, w_ukv, w_uq, gv, w_out0 = _layer0_layout(p0)
    mod, mod_c = mods(l0_w_mod, l0_b_mod)
    wgu1, wd1 = l0_ffn1_w_gu.astype(BF16), l0_ffn1_w_down.astype(BF16)
    wgu2, wd2 = l0_ffn2_w_gu.astype(BF16), l0_ffn2_w_down.astype(BF16)
    sink = l0_win_sink.astype(F32).reshape(WIN_HEADS)

    x = _ffn(x, mod, l0_norm_g, wgu1, wd1, 0)
    xc = _ffn(xc, mod_c, l0_norm_g, wgu1, wd1, 0)
    qa, ka, va, qw, kw, vw = _proj0(x, mod, l0_norm_g, w_in0, w_ukv, w_uq, gv, _rope_tables(seq))
    ctx_rows = _row_tile(bsz * n_ctx)
    proj_c = _proj0(xc, mod_c, l0_norm_g, w_in0, w_ukv, w_uq, gv, _identity_tables(ctx_rows))
    qa_c, ka_c, va_c, qw_c, kw_c, vw_c = [a.reshape(bsz, n_ctx, a.shape[-1]) for a in proj_c]
    oa = _mla_attn(qa, ka_c, va_c, ka, va)
    ow = _win_attn(sink, qw, kw_c, vw_c, kw, vw)
    oa_c = _mla_attn(qa_c, ka_c, va_c)
    ow_c = _win_attn(sink, qw_c, kw_c, vw_c)
    x = _attn_out_ffn(x, oa, ow, mod, l0_norm_g, w_out0, wgu2, wd2)
    flat = lambda a: a.reshape(1, bsz * n_ctx, a.shape[-1])
    xc = _attn_out_ffn(xc, flat(oa_c), flat(ow_c), mod_c, l0_norm_g, w_out0, wgu2, wd2)

    w_in1, wgk, bgk = _layer1_layout(dict(w_in=l1_w_in, w_gk_f=l1_w_gk_f, b_gk_f=l1_b_gk_f,
                                          w_gk_b=l1_w_gk_b, b_gk_b=l1_b_gk_b))
    mod, mod_c = mods(l1_w_mod, l1_b_mod)
    wgu1, wd1 = l1_ffn1_w_gu.astype(BF16), l1_ffn1_w_down.astype(BF16)
    wgu2, wd2 = l1_ffn2_w_gu.astype(BF16), l1_ffn2_w_down.astype(BF16)

    x = _ffn(x, mod, l1_norm_g, wgu1, wd1, 0)
    xc = _ffn(xc, mod_c, l1_norm_g, wgu1, wd1, 0)
    k1, v1, q1, gt1, gf1, gb1 = _proj1(x, mod, l1_norm_g, w_in1, wgk, bgk)
    k1c, v1c, q1c, _, gf1c, gb1c = [a.reshape(bsz, n_ctx, a.shape[-1])
                                    for a in _proj1(xc, mod_c, l1_norm_g, w_in1, wgk, bgk)]
    zero_state = jnp.zeros((bsz, GLA_HEADS, GLA_DV_HEAD, GLA_DK_HEAD), F32)
    _, _, s_f, s_b = _gla_scan(q1c, k1c, v1c, gf1c, gb1c, zero_state, zero_state)
    o_f, o_b, _, _ = _gla_scan(q1, k1, v1, gf1, gb1, s_f, s_b)
    x = _gla_out_ffn(x, o_f, o_b, gt1, l1_g_norm.reshape(1, GLA_DV_HEAD), mod, l1_norm_g,
                     l1_w_out.astype(BF16), wgu2, wd2)
    return x.astype(out_dtype)
```

```python
import functools

import numpy as np
import jax
import jax.numpy as jnp
from jax import lax
from jax.experimental import pallas as pl
from jax.experimental.pallas import tpu as pltpu

F32 = jnp.float32
BF16 = jnp.bfloat16

D_MODEL = 1024
GRID_W = 64
N_MOD = 9
EPS = 1e-6
ROPE_BASE = 10000.0
NEG_INF = -1e30
D_FF = 2816

MLA_HEADS = 8
MLA_Q_RANK = 256
MLA_KV_RANK = 128
MLA_NOPE_DIM = 64
MLA_ROPE_DIM = 32
MLA_V_DIM = 64
MLA_QK_DIM = MLA_NOPE_DIM + MLA_ROPE_DIM
MLA_SCALE = MLA_QK_DIM ** -0.5

WIN_HEADS = 8
WIN_KV_HEADS = 2
WIN_GROUP = WIN_HEADS // WIN_KV_HEADS
WIN_HEAD_DIM = 64
WINDOW = 128
WIN_BLOCK = 128
WIN_SCALE = WIN_HEAD_DIM ** -0.5

GLA_HEADS = 4
GLA_DK = D_MODEL // 2
GLA_DV = D_MODEL
GLA_DK_HEAD = GLA_DK // GLA_HEADS
GLA_DV_HEAD = GLA_DV // GLA_HEADS
GLA_LOWRANK = 16
GLA_GATE_NORM = 16.0
GLA_CHUNK = 64

LANES = 128
FF_CHUNK = 256
MLA_TQ = 1024
MLA_TK = 512
LOG2E = 1.4426950408889634
VMEM_LIMIT = 58 * 1024 * 1024

P0_CKV, P0_WK, P0_WV, P0_CQ, P0_WQ, P0_KR, P0_COLS = 0, 128, 256, 384, 640, 1152, 1280
P1_K, P1_V, P1_Q, P1_G, P1_LOW, P1_COLS = 0, 512, 1536, 2048, 3072, 3200


def _dot(a, b):
    return jnp.dot(a, b, preferred_element_type=F32)


def _dot_nt(a, b):
    return lax.dot_general(a, b, (((1,), (1,)), ((), ())), preferred_element_type=F32)


def _dot_tn(a, b):
    return lax.dot_general(a, b, (((0,), (0,)), ((), ())), preferred_element_type=F32)


def _params(*sem):
    return pltpu.CompilerParams(dimension_semantics=sem, vmem_limit_bytes=VMEM_LIMIT)


def _resident(shape):
    nd = len(shape)
    return pl.BlockSpec(shape, lambda *_: (0,) * nd, pipeline_mode=pl.Buffered(1))


def _silu(x):
    return x / (1.0 + jnp.exp(-x))


def _pre_mod(x, mod_ref, ng_ref, sub):
    shift = mod_ref[0, 3 * sub:3 * sub + 1, :]
    scale = mod_ref[0, 3 * sub + 1:3 * sub + 2, :]
    gate = mod_ref[0, 3 * sub + 2:3 * sub + 3, :]
    rinv = lax.rsqrt(jnp.mean(x * x, axis=-1, keepdims=True) + EPS)
    h = x * rinv * ng_ref[sub:sub + 1, :] * (1.0 + scale) + shift
    return h.astype(BF16), gate


def _swiglu(hb, wgu_ref, wd_ref, act_ref):
    for c in range(D_FF // FF_CHUNK):
        lo = c * FF_CHUNK
        g = _dot(hb, wgu_ref[:, lo:lo + FF_CHUNK])
        u = _dot(hb, wgu_ref[:, D_FF + lo:D_FF + lo + FF_CHUNK])
        act_ref[:, lo:lo + FF_CHUNK] = (_silu(g) * u).astype(BF16)
    return _dot(act_ref[...], wd_ref[...])


def _mod_kernel(c_ref, w_ref, b_ref, o_ref):
    a = _silu(c_ref[...]).astype(BF16)
    o_ref[...] = _dot(a, w_ref[...].astype(BF16)) + b_ref[...]


def _mod_table(cc, w_mod, b_mod):
    rows = cc.shape[0]
    n = w_mod.shape[1]
    bn = 1152
    return pl.pallas_call(
        _mod_kernel,
        grid=(n // bn,),
        in_specs=[pl.BlockSpec((rows, D_MODEL), lambda j: (0, 0)),
                  pl.BlockSpec((D_MODEL, bn), lambda j: (0, j)),
                  pl.BlockSpec((1, bn), lambda j: (0, j))],
        out_specs=pl.BlockSpec((rows, bn), lambda j: (0, j)),
        out_shape=jax.ShapeDtypeStruct((rows, n), F32),
        compiler_params=_params("arbitrary"),
        name="mod_table",
    )(cc, w_mod, b_mod.reshape(1, n))


def _ffn_kernel(x_ref, mod_ref, ng_ref, wgu_ref, wd_ref, o_ref, act_ref, *, sub):
    x = x_ref[0]
    hb, gate = _pre_mod(x, mod_ref, ng_ref, sub)
    o_ref[0] = x + 0.5 * gate * _swiglu(hb, wgu_ref, wd_ref, act_ref)


def _attn_out_ffn_kernel(x_ref, oa_ref, ob_ref, mod_ref, ng_ref, wo_ref, wgu_ref, wd_ref, o_ref, act_ref):
    half = oa_ref.shape[-1]
    y = _dot(oa_ref[0], wo_ref[:half, :]) + _dot(ob_ref[0], wo_ref[half:, :])
    x = x_ref[0] + mod_ref[0, 5:6, :] * y
    hb, gate = _pre_mod(x, mod_ref, ng_ref, 2)
    o_ref[0] = x + 0.5 * gate * _swiglu(hb, wgu_ref, wd_ref, act_ref)


def _gla_out_ffn_kernel(x_ref, of_ref, ob_ref, gt_ref, gn_ref, mod_ref, ng_ref, wo_ref, wgu_ref, wd_ref,
                        o_ref, act_ref, on_ref):
    for h in range(GLA_HEADS):
        hs = slice(h * GLA_DV_HEAD, (h + 1) * GLA_DV_HEAD)
        o = of_ref[0, :, hs] + ob_ref[0, :, hs]
        rinv = lax.rsqrt(jnp.mean(o * o, axis=-1, keepdims=True) + EPS)
        on_ref[:, hs] = (o * rinv * gn_ref[...] * gt_ref[0, :, hs]).astype(BF16)
    x = x_ref[0] + mod_ref[0, 5:6, :] * _dot(on_ref[...], wo_ref[...])
    hb, gate = _pre_mod(x, mod_ref, ng_ref, 2)
    o_ref[0] = x + 0.5 * gate * _swiglu(hb, wgu_ref, wd_ref, act_ref)


def _tok_spec(tm, width):
    return pl.BlockSpec((1, tm, width), lambda b, i: (b, i, 0))


def _mod_spec(mod):
    if mod.shape[0] == 1:
        return pl.BlockSpec((1, N_MOD, D_MODEL), lambda b, i: (0, 0, 0))
    return pl.BlockSpec((1, N_MOD, D_MODEL), lambda b, i: (b, 0, 0))


def _row_tile(t, pref=512):
    return pref if t % pref == 0 else t


def _ffn(x, mod, ng, wgu, wd, sub):
    bx, t, _ = x.shape
    tm = _row_tile(t)
    return pl.pallas_call(
        functools.partial(_ffn_kernel, sub=sub),
        grid=(bx, t // tm),
        in_specs=[_tok_spec(tm, D_MODEL), _mod_spec(mod), _resident((3, D_MODEL)),
                  _resident(wgu.shape), _resident(wd.shape)],
        out_specs=_tok_spec(tm, D_MODEL),
        out_shape=jax.ShapeDtypeStruct(x.shape, F32),
        scratch_shapes=[pltpu.VMEM((tm, D_FF), BF16)],
        compiler_params=_params("parallel", "parallel"),
        name="ffn",
    )(x, mod, ng, wgu, wd)


def _attn_out_ffn(x, oa, ob, mod, ng, wo, wgu, wd):
    bx, t, _ = x.shape
    tm = _row_tile(t)
    return pl.pallas_call(
        _attn_out_ffn_kernel,
        grid=(bx, t // tm),
        in_specs=[_tok_spec(tm, D_MODEL), _tok_spec(tm, oa.shape[-1]), _tok_spec(tm, ob.shape[-1]),
                  _mod_spec(mod), _resident((3, D_MODEL)), _resident(wo.shape),
                  _resident(wgu.shape), _resident(wd.shape)],
        out_specs=_tok_spec(tm, D_MODEL),
        out_shape=jax.ShapeDtypeStruct(x.shape, F32),
        scratch_shapes=[pltpu.VMEM((tm, D_FF), BF16)],
        compiler_params=_params("parallel", "parallel"),
        name="attn_out_ffn",
    )(x, oa, ob, mod, ng, wo, wgu, wd)


def _gla_out_ffn(x, of, ob, gt, gn, mod, ng, wo, wgu, wd):
    bx, t, _ = x.shape
    tm = _row_tile(t)
    return pl.pallas_call(
        _gla_out_ffn_kernel,
        grid=(bx, t // tm),
        in_specs=[_tok_spec(tm, D_MODEL), _tok_spec(tm, GLA_DV), _tok_spec(tm, GLA_DV), _tok_spec(tm, GLA_DV),
                  _resident(gn.shape), _mod_spec(mod), _resident((3, D_MODEL)), _resident(wo.shape),
                  _resident(wgu.shape), _resident(wd.shape)],
        out_specs=_tok_spec(tm, D_MODEL),
        out_shape=jax.ShapeDtypeStruct(x.shape, F32),
        scratch_shapes=[pltpu.VMEM((tm, D_FF), BF16), pltpu.VMEM((tm, GLA_DV), BF16)],
        compiler_params=_params("parallel", "parallel"),
        name="gla_out_ffn",
    )(x, of, ob, gt, gn, mod, ng, wo, wgu, wd)


def _lane(shape):
    return lax.broadcasted_iota(jnp.int32, shape, 1)


def _rope(x, cos, sin, lane, half):
    partner = jnp.where((lane & half) == 0, pltpu.roll(x, LANES - half, 1), pltpu.roll(x, half, 1))
    return x * cos + partner * sin


def _proj0_kernel(x_ref, mod_ref, ng_ref, win_ref, wuk_ref, wvt_ref, wuq_ref, gv_ref, cm_ref, sm_ref, cw_ref, sw_ref,
                  qa_ref, ka_ref, va_ref, qw_ref, kw_ref, vw_ref):
    hb, _ = _pre_mod(x_ref[0], mod_ref, ng_ref, 1)
    z = _dot(hb, win_ref[...])
    tm = z.shape[0]
    lane = _lane((tm, LANES))
    lo_half = lane < 64
    cm, sm, cw, sw = cm_ref[...], sm_ref[...], cw_ref[...], sw_ref[...]
    g_kva, g_qa0, g_qa1 = gv_ref[0:1, :], gv_ref[1:2, :], gv_ref[2:3, :]
    g_kn, g_kr, g_q, g_wq, g_wk = gv_ref[3:4, :], gv_ref[4:5, :], gv_ref[5:6, :], gv_ref[6:7, :], gv_ref[7:8, :]

    def seg_rinv(v, n0, n1):
        sq = v * v
        s0 = jnp.sum(jnp.where(lo_half, sq, 0.0), axis=-1, keepdims=True)
        s1 = jnp.sum(jnp.where(lo_half, 0.0, sq), axis=-1, keepdims=True)
        return jnp.where(lo_half, lax.rsqrt(s0 * (1.0 / n0) + EPS), lax.rsqrt(s1 * (1.0 / n1) + EPS))

    ckv = z[:, P0_CKV:P0_CKV + MLA_KV_RANK]
    ckv = ckv * lax.rsqrt(jnp.mean(ckv * ckv, axis=-1, keepdims=True) + EPS) * g_kva
    ckv = ckv.astype(BF16)
    kv = _dot(ckv, wuk_ref[...])
    vt = _dot_nt(wvt_ref[...], ckv)
    vrow = lax.broadcasted_iota(jnp.int32, vt.shape, 0)
    va_ref[0, 0] = jnp.where((vrow & (LANES - 1)) == MLA_V_DIM, 1.0, vt).astype(BF16)
    kr = z[:, P0_KR:P0_KR + LANES]
    kr = kr * lax.rsqrt(jnp.sum(kr * kr, axis=-1, keepdims=True) * (1.0 / MLA_ROPE_DIM) + EPS) * g_kr
    kr = _rope(kr, cm, sm, lane, 16)
    for h in range(MLA_HEADS):
        kn = kv[:, h * LANES:(h + 1) * LANES]
        rinv = lax.rsqrt(jnp.sum(kn * kn, axis=-1, keepdims=True) * (1.0 / MLA_NOPE_DIM) + EPS)
        ka_ref[0, :, h * LANES:(h + 1) * LANES] = (kn * rinv * g_kn + kr).astype(BF16)

    cq = z[:, P0_CQ:P0_CQ + MLA_Q_RANK]
    rq = lax.rsqrt(jnp.mean(cq * cq, axis=-1, keepdims=True) + EPS)
    cqn = jnp.concatenate([cq[:, :LANES] * rq * g_qa0, cq[:, LANES:] * rq * g_qa1], axis=-1).astype(BF16)
    q = _dot(cqn, wuq_ref[...])
    for h in range(MLA_HEADS):
        qh = q[:, h * LANES:(h + 1) * LANES]
        qh = qh * seg_rinv(qh, MLA_NOPE_DIM, MLA_ROPE_DIM) * g_q
        qa_ref[0, :, h * LANES:(h + 1) * LANES] = _rope(qh, cm, sm, lane, 16).astype(BF16)

    for j in range(WIN_HEADS // 2):
        qj = z[:, P0_WQ + j * LANES:P0_WQ + (j + 1) * LANES]
        qj = qj * seg_rinv(qj, WIN_HEAD_DIM, WIN_HEAD_DIM) * g_wq
        qw_ref[0, :, j * LANES:(j + 1) * LANES] = _rope(qj, cw, sw, lane, 32).astype(BF16)
    wk = z[:, P0_WK:P0_WK + LANES]
    wk = wk * seg_rinv(wk, WIN_HEAD_DIM, WIN_HEAD_DIM) * g_wk
    kw_ref[0] = _rope(wk, cw, sw, lane, 32).astype(BF16)
    vw_ref[0] = z[:, P0_WV:P0_WV + LANES].astype(BF16)


def _proj0(x, mod, ng, win, wuk, wvt, wuq, gv, tabs):
    bx, t, _ = x.shape
    tm = _row_tile(t, MLA_TK)
    ntab = tabs[0].shape[0] // tm
    tab_spec = pl.BlockSpec((tm, LANES), lambda b, i: (i % ntab, 0))
    widths = (MLA_HEADS * LANES, MLA_HEADS * LANES, None,
              WIN_HEADS * WIN_HEAD_DIM, WIN_KV_HEADS * WIN_HEAD_DIM, WIN_KV_HEADS * WIN_HEAD_DIM)
    vt_spec = pl.BlockSpec((1, 1, MLA_HEADS * LANES, tm), lambda b, i: (b, i, 0, 0))
    vt_shape = jax.ShapeDtypeStruct((bx, t // tm, MLA_HEADS * LANES, tm), BF16)
    return pl.pallas_call(
        _proj0_kernel,
        grid=(bx, t // tm),
        in_specs=[_tok_spec(tm, D_MODEL), _mod_spec(mod), _resident((3, D_MODEL)), _resident(win.shape),
                  _resident(wuk.shape), _resident(wvt.shape), _resident(wuq.shape), _resident(gv.shape)]
        + [tab_spec] * 4,
        out_specs=[vt_spec if w is None else _tok_spec(tm, w) for w in widths],
        out_shape=[vt_shape if w is None else jax.ShapeDtypeStruct((bx, t, w), BF16) for w in widths],
        compiler_params=_params("parallel", "parallel"),
        name="proj0",
    )(x, mod, ng, win, wuk, wvt, wuq, gv, *tabs)


def _mla_attn_kernel(*refs, has_latent):
    if has_latent:
        q_ref, kc_ref, vc_ref, k_ref, v_ref, o_ref, m_ref, acc_ref, s_ref, mc_ref = refs
    else:
        q_ref, kc_ref, vc_ref, o_ref, m_ref, acc_ref = refs

    def scores(kblk, hh):
        return _dot_nt(kblk[:, hh * LANES:(hh + 1) * LANES], q_ref[0, :, hh * LANES:(hh + 1) * LANES])

    def qk(j, slot):
        kblk = k_ref[0, pl.ds(pl.multiple_of(j * MLA_TK, MLA_TK), MLA_TK), :]
        for hh in range(2):
            st = scores(kblk, hh)
            s_ref[slot, hh] = st
            mc_ref[slot, hh] = jnp.max(st, axis=0, keepdims=True)

    def process(j, slot):
        vt = v_ref[0, j]
        for hh in range(2):
            m_prev = m_ref[hh]
            m_new = jnp.maximum(m_prev, mc_ref[slot, hh])
            p = jnp.exp2(s_ref[slot, hh] - m_new).astype(BF16)
            pv = _dot(vt[hh * LANES:(hh + 1) * LANES], p)
            acc_ref[hh] = acc_ref[hh] * jnp.exp2(m_prev - m_new) + pv
            m_ref[hh] = m_new

    def first_tile():
        sts = [scores(kc_ref[0], hh) for hh in range(2)]
        if has_latent:
            qk(0, 0)
        for hh in range(2):
            m_new = jnp.max(sts[hh], axis=0, keepdims=True)
            p = jnp.exp2(sts[hh] - m_new).astype(BF16)
            acc_ref[hh] = _dot(vc_ref[0, hh * LANES:(hh + 1) * LANES], p)
            m_ref[hh] = m_new

    first_tile()
    if has_latent:
        n = v_ref.shape[1]

        def body(i, carry):
            qk(2 * i + 1, 1)
            process(2 * i, 0)
            qk(2 * i + 2, 0)
            process(2 * i + 1, 1)
            return carry
        lax.fori_loop(0, n // 2 - 1, body, 0)
        qk(n - 1, 1)
        process(n - 2, 0)
        process(n - 1, 1)
    a0, a1 = acc_ref[0], acc_ref[1]
    ot = jnp.concatenate([a0[:MLA_V_DIM] / a0[MLA_V_DIM:MLA_V_DIM + 1],
                          a1[:MLA_V_DIM] / a1[MLA_V_DIM:MLA_V_DIM + 1]], axis=0)
    o_ref[0] = ot.T.astype(BF16)


def _mla_attn(q, kc, vtc, k=None, vt=None):
    bsz, t, _ = q.shape
    nc = kc.shape[1]
    has_latent = k is not None
    tq = _row_tile(t, MLA_TQ)
    in_specs = [pl.BlockSpec((1, tq, 2 * LANES), lambda b, h, i: (b, i, h)),
                pl.BlockSpec((1, nc, 2 * LANES), lambda b, h, i: (b, 0, h)),
                pl.BlockSpec((1, 2 * LANES, nc), lambda b, h, i: (b, h, 0))]
    args = [q, kc, vtc]
    scratch = [pltpu.VMEM((2, 1, tq), F32), pltpu.VMEM((2, LANES, tq), F32)]
    if has_latent:
        n = k.shape[1]
        nt, _, tk = vt.shape[1:]
        assert tk == MLA_TK and nt * tk == n and nt % 2 == 0, (n, nt, tk)
        in_specs += [pl.BlockSpec((1, n, 2 * LANES), lambda b, h, i: (b, 0, h)),
                     pl.BlockSpec((1, nt, 2 * LANES, tk), lambda b, h, i: (b, 0, h, 0))]
        args += [k, vt]
        scratch += [pltpu.VMEM((2, 2, tk, tq), F32), pltpu.VMEM((2, 2, 1, tq), F32)]
    return pl.pallas_call(
        functools.partial(_mla_attn_kernel, has_latent=has_latent),
        grid=(bsz, MLA_HEADS // 2, t // tq),
        in_specs=in_specs,
        out_specs=pl.BlockSpec((1, tq, LANES), lambda b, h, i: (b, i, h)),
        out_shape=jax.ShapeDtypeStruct((bsz, t, MLA_HEADS * MLA_V_DIM), BF16),
        scratch_shapes=scratch,
        compiler_params=_params("parallel", "parallel", "arbitrary"),
        name="mla_attn" if has_latent else "mla_attn_ctx",
    )(*args)


def _win_attn_kernel(*refs, has_window, seq):
    if has_window:
        sink_ref, q_ref, kc_ref, vc_ref, kp_ref, kq_ref, kn_ref, vp_ref, vq_ref, vn_ref, o_ref = refs
        kall = jnp.concatenate([kc_ref[0], kp_ref[0], kq_ref[0], kn_ref[0]], axis=0)
        vall = jnp.concatenate([vc_ref[0], vp_ref[0], vq_ref[0], vn_ref[0]], axis=0)
    else:
        sink_ref, q_ref, kc_ref, vc_ref, o_ref = refs
        kall, vall = kc_ref[0], vc_ref[0]
    tq = q_ref.shape[1]
    nk = kall.shape[0]
    nc = kc_ref.shape[1]
    klane = _lane(kall.shape) < 64
    kzero = jnp.zeros_like(kall)
    ksel = (jnp.where(klane, kall, kzero), jnp.where(klane, kzero, kall))
    vsel = (jnp.where(klane, vall, kzero), jnp.where(klane, kzero, vall))
    if has_window:
        i = pl.program_id(1)
        col = lax.broadcasted_iota(jnp.int32, (tq, nk), 1)
        row = lax.broadcasted_iota(jnp.int32, (tq, nk), 0)
        rel = col - (nc + WIN_BLOCK) - row
        kpos = i * WIN_BLOCK + col - (nc + WIN_BLOCK)
        valid = (col < nc) | ((jnp.abs(rel) <= WINDOW) & (kpos >= 0) & (kpos < seq))
    for j in range(WIN_GROUP):
        qj = q_ref[0, :, j * LANES:(j + 1) * LANES]
        out = None
        for n in range(WIN_KV_HEADS):
            sink = sink_ref[n * WIN_GROUP + j]
            s = _dot_nt(qj, ksel[n])
            if has_window:
                s = jnp.where(valid, s, NEG_INF)
            m = jnp.maximum(jnp.max(s, axis=-1, keepdims=True), sink)
            p = jnp.exp(s - m)
            denom = jnp.sum(p, axis=-1, keepdims=True) + jnp.exp(sink - m)
            o = _dot(p.astype(BF16), vsel[n]) / denom
            out = o if out is None else out + o
        o_ref[0, :, j * LANES:(j + 1) * LANES] = out.astype(BF16)


def _win_attn(sink, q, kc, vc, k=None, v=None):
    bsz, t, _ = q.shape
    nc = kc.shape[1]
    has_window = k is not None
    nb = t // WIN_BLOCK
    kvw = WIN_KV_HEADS * WIN_HEAD_DIM
    in_specs = [pl.BlockSpec(memory_space=pltpu.SMEM),
                pl.BlockSpec((1, WIN_BLOCK, WIN_HEADS * WIN_HEAD_DIM), lambda b, i: (b, i, 0)),
                pl.BlockSpec((1, nc, kvw), lambda b, i: (b, 0, 0)),
                pl.BlockSpec((1, nc, kvw), lambda b, i: (b, 0, 0))]
    args = [sink, q, kc, vc]
    if has_window:
        prev_spec = pl.BlockSpec((1, WIN_BLOCK, kvw), lambda b, i: (b, jnp.maximum(i - 1, 0), 0))
        cur_spec = pl.BlockSpec((1, WIN_BLOCK, kvw), lambda b, i: (b, i, 0))
        next_spec = pl.BlockSpec((1, WIN_BLOCK, kvw), lambda b, i: (b, jnp.minimum(i + 1, nb - 1), 0))
        in_specs += [prev_spec, cur_spec, next_spec] * 2
        args += [k, k, k, v, v, v]
    return pl.pallas_call(
        functools.partial(_win_attn_kernel, has_window=has_window, seq=t),
        grid=(bsz, nb),
        in_specs=in_specs,
        out_specs=pl.BlockSpec((1, WIN_BLOCK, WIN_HEADS * WIN_HEAD_DIM), lambda b, i: (b, i, 0)),
        out_shape=jax.ShapeDtypeStruct((bsz, t, WIN_HEADS * WIN_HEAD_DIM), BF16),
        compiler_params=_params("parallel", "parallel"),
        name="win_attn" if has_window else "win_attn_ctx",
    )(*args)


def _log_sigmoid(x):
    return jnp.minimum(x, 0.0) - jnp.log1p(jnp.exp(-jnp.abs(x)))


def _proj1_kernel(x_ref, mod_ref, ng_ref, win_ref, wgk_ref, bgk_ref, k_ref, v_ref, q_ref, gt_ref, gf_ref, gb_ref):
    hb, _ = _pre_mod(x_ref[0], mod_ref, ng_ref, 1)
    z = _dot(hb, win_ref[...])
    k_ref[0] = z[:, P1_K:P1_K + GLA_DK]
    v_ref[0] = z[:, P1_V:P1_V + GLA_DV].astype(BF16)
    q_ref[0] = z[:, P1_Q:P1_Q + GLA_DK] * (GLA_DK_HEAD ** -0.5)
    gt_ref[0] = _silu(z[:, P1_G:P1_G + GLA_DV])
    pre = _dot(z[:, P1_LOW:P1_LOW + LANES].astype(BF16), wgk_ref[...]) + bgk_ref[...]
    gate = _log_sigmoid(pre) * (1.0 / GLA_GATE_NORM)
    gf_ref[0] = gate[:, :GLA_DK]
    gb_ref[0] = gate[:, GLA_DK:]


def _proj1(x, mod, ng, win, wgk, bgk):
    bx, t, _ = x.shape
    tm = _row_tile(t)
    outs = ((GLA_DK, F32), (GLA_DV, BF16), (GLA_DK, F32), (GLA_DV, F32), (GLA_DK, F32), (GLA_DK, F32))
    return pl.pallas_call(
        _proj1_kernel,
        grid=(bx, t // tm),
        in_specs=[_tok_spec(tm, D_MODEL), _mod_spec(mod), _resident((3, D_MODEL)), _resident(win.shape),
                  _resident(wgk.shape), _resident(bgk.shape)],
        out_specs=[_tok_spec(tm, w) for w, _ in outs],
        out_shape=[jax.ShapeDtypeStruct((bx, t, w), d) for w, d in outs],
        compiler_params=_params("parallel", "parallel"),
        name="proj1",
    )(x, mod, ng, win, wgk, bgk)


def _gla_chunk(q_ref, k_ref, v_ref, g_ref, o_ref, st_ref, r0, keep, ones, backward):
    rows = slice(r0, r0 + GLA_CHUNK)
    g = g_ref[0, rows, :]
    g_hi = g.astype(BF16)
    g_lo = (g - g_hi.astype(F32)).astype(BF16)
    bb = _dot(ones, g_hi) + _dot(ones, g_lo)
    b_end = bb[0:1, :] if backward else bb[GLA_CHUNK - 1:GLA_CHUNK, :]
    q = q_ref[0, rows, :]
    k = k_ref[0, rows, :]
    q_dec = (q * jnp.exp(bb)).astype(BF16)
    k_inv = (k * jnp.exp(-bb)).astype(BF16)
    k_end = (k * jnp.exp(b_end - bb)).astype(BF16)
    decay = jnp.exp(b_end)
    for h in range(GLA_HEADS):
        ks = slice(h * GLA_DK_HEAD, (h + 1) * GLA_DK_HEAD)
        vs = slice(h * GLA_DV_HEAD, (h + 1) * GLA_DV_HEAD)
        vh = v_ref[0, rows, vs]
        a = jnp.where(keep, _dot_nt(q_dec[:, ks], k_inv[:, ks]), 0.0).astype(BF16)
        st = st_ref[h]
        o_ref[0, rows, vs] = _dot(a, vh) + _dot_nt(q_dec[:, ks], st.astype(BF16))
        st_ref[h] = st * decay[:, ks] + _dot_tn(vh, k_end[:, ks])


def _gla_scan_kernel(qf_ref, kf_ref, vf_ref, gf_ref, qb_ref, kb_ref, vb_ref, gb_ref, s0f_ref, s0b_ref,
                     of_ref, ob_ref, sf_ref, sb_ref, stf_ref, stb_ref, *, n_sub):
    i = pl.program_id(1)

    @pl.when(i == 0)
    def _():
        stf_ref[...] = s0f_ref[0]
        stb_ref[...] = s0b_ref[0]

    row = lax.broadcasted_iota(jnp.int32, (GLA_CHUNK, GLA_CHUNK), 0)
    col = lax.broadcasted_iota(jnp.int32, (GLA_CHUNK, GLA_CHUNK), 1)
    lower = col <= row
    upper = col >= row
    ones_lo = jnp.where(lower, 1.0, 0.0).astype(BF16)
    ones_up = jnp.where(upper, 1.0, 0.0).astype(BF16)
    for c in range(n_sub):
        _gla_chunk(qf_ref, kf_ref, vf_ref, gf_ref, of_ref, stf_ref, c * GLA_CHUNK, lower, ones_lo, False)
    for c in reversed(range(n_sub)):
        _gla_chunk(qb_ref, kb_ref, vb_ref, gb_ref, ob_ref, stb_ref, c * GLA_CHUNK, upper, ones_up, True)

    @pl.when(i == pl.num_programs(1) - 1)
    def _():
        sf_ref[0] = stf_ref[...]
        sb_ref[0] = stb_ref[...]


def _gla_scan(q, k, v, gf, gb, s0f, s0b):
    bsz, t, _ = q.shape
    tb = 256 if t % 256 == 0 else GLA_CHUNK
    nblk = t // tb
    fwd = lambda w: pl.BlockSpec((1, tb, w), lambda b, i: (b, i, 0))
    bwd = lambda w: pl.BlockSpec((1, tb, w), lambda b, i: (b, nblk - 1 - i, 0))
    st_spec = pl.BlockSpec((1, GLA_HEADS, GLA_DV_HEAD, GLA_DK_HEAD), lambda b, i: (b, 0, 0, 0))
    st_shape = jax.ShapeDtypeStruct((bsz, GLA_HEADS, GLA_DV_HEAD, GLA_DK_HEAD), F32)
    return pl.pallas_call(
        functools.partial(_gla_scan_kernel, n_sub=tb // GLA_CHUNK),
        grid=(bsz, nblk),
        in_specs=[fwd(GLA_DK), fwd(GLA_DK), fwd(GLA_DV), fwd(GLA_DK),
                  bwd(GLA_DK), bwd(GLA_DK), bwd(GLA_DV), bwd(GLA_DK), st_spec, st_spec],
        out_specs=[fwd(GLA_DV), bwd(GLA_DV), st_spec, st_spec],
        out_shape=[jax.ShapeDtypeStruct((bsz, t, GLA_DV), F32)] * 2 + [st_shape] * 2,
        scratch_shapes=[pltpu.VMEM((GLA_HEADS, GLA_DV_HEAD, GLA_DK_HEAD), F32)] * 2,
        compiler_params=_params("parallel", "arbitrary"),
        name="gla_scan",
    )(q, k, v, gf, q, k, v, gb, s0f, s0b)


def _deinterleave(n):
    return np.concatenate([np.arange(0, n, 2), np.arange(1, n, 2)])


def _gather_cols(w, idx):
    idx = np.asarray(idx)
    cols = jnp.take(w, jnp.asarray(np.maximum(idx, 0)), axis=-1)
    return jnp.where(jnp.asarray(idx >= 0), cols, 0.0)


def _layer0_layout(p):
    ckv0, kr0 = 0, MLA_KV_RANK
    wk0 = kr0 + MLA_ROPE_DIM
    wv0 = wk0 + WIN_KV_HEADS * WIN_HEAD_DIM
    cq0 = wv0 + WIN_KV_HEADS * WIN_HEAD_DIM
    wq0 = cq0 + MLA_Q_RANK
    de64, de32 = _deinterleave(WIN_HEAD_DIM), _deinterleave(MLA_ROPE_DIM)
    idx = np.full((P0_COLS,), -1, np.int64)
    idx[P0_CKV:P0_CKV + MLA_KV_RANK] = ckv0 + np.arange(MLA_KV_RANK)
    for n in range(WIN_KV_HEADS):
        idx[P0_WK + n * 64:P0_WK + (n + 1) * 64] = wk0 + n * 64 + de64
    idx[P0_WV:P0_WV + 128] = wv0 + np.arange(128)
    idx[P0_CQ:P0_CQ + MLA_Q_RANK] = cq0 + np.arange(MLA_Q_RANK)
    for j in range(WIN_GROUP):
        for n in range(WIN_KV_HEADS):
            lo = P0_WQ + j * LANES + n * 64
            idx[lo:lo + 64] = wq0 + (n * WIN_GROUP + j) * 64 + de64
    idx[P0_KR + 64:P0_KR + 96] = kr0 + de32
    w_in = _gather_cols(p["w_in"], idx).astype(BF16)

    per = MLA_NOPE_DIM + MLA_V_DIM
    kidx = np.full((MLA_HEADS * LANES,), -1, np.int64)
    qidx = np.full((MLA_HEADS * LANES,), -1, np.int64)
    vidx = np.full((MLA_HEADS * LANES,), -1, np.int64)
    for h in range(MLA_HEADS):
        kidx[h * LANES:h * LANES + 64] = h * per + np.arange(64)
        vidx[h * LANES:h * LANES + 64] = h * per + MLA_NOPE_DIM + np.arange(64)
        qidx[h * LANES:h * LANES + 64] = h * MLA_QK_DIM + np.arange(64)
        qidx[h * LANES + 64:h * LANES + 96] = h * MLA_QK_DIM + MLA_NOPE_DIM + de32
    w_uk = _gather_cols(p["mla_w_ukv"], kidx).astype(BF16)
    w_vt = _gather_cols(p["mla_w_ukv"], vidx).T.astype(BF16)
    w_uq = _gather_cols(p["mla_w_uq"], qidx).astype(BF16)

    z32, z64 = jnp.zeros((32,), F32), jnp.zeros((64,), F32)
    g_qr, g_kr = p["mla_g_qr"][de32], p["mla_g_kr"][de32]
    g_wq, g_wk = p["win_g_q"][de64], p["win_g_k"][de64]
    gv = jnp.stack([
        p["mla_g_kva"],
        p["mla_g_qa"][:LANES], p["mla_g_qa"][LANES:],
        jnp.concatenate([p["mla_g_kn"], z64]),
        jnp.concatenate([z64, g_kr, z32]),
        jnp.concatenate([p["mla_g_qn"], g_qr, z32]) * (MLA_SCALE * LOG2E),
        jnp.concatenate([g_wq, g_wq]) * WIN_SCALE,
        jnp.concatenate([g_wk, g_wk]),
    ])

    out_rows = np.arange(D_MODEL)
    base = MLA_HEADS * MLA_V_DIM
    for j in range(WIN_GROUP):
        for n in range(WIN_KV_HEADS):
            lo = base + j * LANES + n * 64
            out_rows[lo:lo + 64] = base + (n * WIN_GROUP + j) * 64 + np.arange(64)
    w_out = p["w_out"][jnp.asarray(out_rows)].astype(BF16)
    return w_in, w_uk, w_vt, w_uq, gv, w_out


def _layer1_layout(p):
    k0, v0 = 0, GLA_DK
    lf0 = v0 + GLA_DV
    lb0 = lf0 + GLA_LOWRANK
    q0 = lb0 + GLA_LOWRANK
    g0 = q0 + GLA_DK
    idx = np.full((P1_COLS,), -1, np.int64)
    idx[P1_K:P1_K + GLA_DK] = k0 + np.arange(GLA_DK)
    idx[P1_V:P1_V + GLA_DV] = v0 + np.arange(GLA_DV)
    idx[P1_Q:P1_Q + GLA_DK] = q0 + np.arange(GLA_DK)
    idx[P1_G:P1_G + GLA_DV] = g0 + np.arange(GLA_DV)
    idx[P1_LOW:P1_LOW + GLA_LOWRANK] = lf0 + np.arange(GLA_LOWRANK)
    idx[P1_LOW + GLA_LOWRANK:P1_LOW + 2 * GLA_LOWRANK] = lb0 + np.arange(GLA_LOWRANK)
    w_in = _gather_cols(p["w_in"], idx).astype(BF16)
    wgk = jnp.zeros((LANES, 2 * GLA_DK), F32)
    wgk = wgk.at[:GLA_LOWRANK, :GLA_DK].set(p["w_gk_f"])
    wgk = wgk.at[GLA_LOWRANK:2 * GLA_LOWRANK, GLA_DK:].set(p["w_gk_b"])
    bgk = jnp.concatenate([p["b_gk_f"], p["b_gk_b"]]).reshape(1, 2 * GLA_DK)
    return w_in, wgk.astype(BF16), bgk


def _rope_tables(seq):
    t = jnp.arange(seq)
    row = (t // GRID_W).astype(F32)
    col = (t % GRID_W).astype(F32)

    def angles(rot_dim):
        n_freq = rot_dim // 4
        inv = ROPE_BASE ** (-jnp.arange(n_freq, dtype=F32) / n_freq)
        return jnp.concatenate([row[:, None] * inv, col[:, None] * inv], axis=-1)

    am, aw = angles(MLA_ROPE_DIM), angles(WIN_HEAD_DIM)
    one64, one32 = jnp.ones((seq, 64), F32), jnp.ones((seq, 32), F32)
    cm = jnp.concatenate([one64, jnp.cos(am), jnp.cos(am), one32], axis=-1)
    sm = jnp.concatenate([0 * one64, -jnp.sin(am), jnp.sin(am), 0 * one32], axis=-1)
    cw = jnp.concatenate([jnp.cos(aw)] * 4, axis=-1)
    sw = jnp.concatenate([-jnp.sin(aw), jnp.sin(aw)] * 2, axis=-1)
    return cm, sm, cw, sw


def _identity_tables(rows):
    one, zero = jnp.ones((rows, LANES), F32), jnp.zeros((rows, LANES), F32)
    return one, zero, one, zero


def kernel(x, c, ctx, c_ctx, l0_norm_g, l0_w_mod, l0_b_mod, l0_ffn1_w_gu, l0_ffn1_w_down, l0_ffn2_w_gu, l0_ffn2_w_down, l0_w_in, l0_mla_g_qa, l0_mla_g_kva, l0_mla_w_uq, l0_mla_w_ukv, l0_mla_g_qn, l0_mla_g_qr, l0_mla_g_kn, l0_mla_g_kr, l0_win_g_q, l0_win_g_k, l0_win_sink, l0_w_out, l1_norm_g, l1_w_mod, l1_b_mod, l1_ffn1_w_gu, l1_ffn1_w_down, l1_ffn2_w_gu, l1_ffn2_w_down, l1_w_in, l1_w_gk_f, l1_b_gk_f, l1_w_gk_b, l1_b_gk_b, l1_g_norm, l1_w_out):
    bsz, seq, _ = x.shape
    n_ctx = ctx.shape[1]
    out_dtype = x.dtype
    x = x.astype(F32)
    xc = ctx.astype(F32)

    cc = jnp.zeros((16, D_MODEL), F32).at[:bsz].set(c).at[bsz].set(c_ctx)

    def mods(w_mod, b_mod):
        tab = _mod_table(cc, w_mod, b_mod).reshape(16, N_MOD, D_MODEL)
        return tab[:bsz], tab[bsz:bsz + 1]

    p0 = dict(w_in=l0_w_in, mla_g_qa=l0_mla_g_qa, mla_g_kva=l0_mla_g_kva, mla_w_uq=l0_mla_w_uq,
              mla_w_ukv=l0_mla_w_ukv, mla_g_qn=l0_mla_g_qn, mla_g_qr=l0_mla_g_qr, mla_g_kn=l0_mla_g_kn,
              mla_g_kr=l0_mla_g_kr, win_g_q=l0_win_g_q, win_g_k=l0_win_g_k, w_out=l0_w_out)
    w_in0, w_uk, w_vt, w_uq, gv, w_out0 = _layer0_layout(p0)
    mod, mod_c = mods(l0_w_mod, l0_b_mod)
    wgu1, wd1 = l0_ffn1_w_gu.astype(BF16), l0_ffn1_w_down.astype(BF16)
    wgu2, wd2 = l0_ffn2_w_gu.astype(BF16), l0_ffn2_w_down.astype(BF16)
    sink = l0_win_sink.astype(F32).reshape(WIN_HEADS)

    x = _ffn(x, mod, l0_norm_g, wgu1, wd1, 0)
    xc = _ffn(xc, mod_c, l0_norm_g, wgu1, wd1, 0)
    qa, ka, vat, qw, kw, vw = _proj0(x, mod, l0_norm_g, w_in0, w_uk, w_vt, w_uq, gv, _rope_tables(seq))
    qa_c, ka_c, vat_c, qw_c, kw_c, vw_c = _proj0(xc, mod_c, l0_norm_g, w_in0, w_uk, w_vt, w_uq, gv,
                                                 _identity_tables(n_ctx))
    vat_c = vat_c.reshape(bsz, MLA_HEADS * LANES, n_ctx)
    oa = _mla_attn(qa, ka_c, vat_c, ka, vat)
    ow = _win_attn(sink, qw, kw_c, vw_c, kw, vw)
    oa_c = _mla_attn(qa_c, ka_c, vat_c)
    ow_c = _win_attn(sink, qw_c, kw_c, vw_c)
    x = _attn_out_ffn(x, oa, ow, mod, l0_norm_g, w_out0, wgu2, wd2)
    xc = _attn_out_ffn(xc, oa_c, ow_c, mod_c, l0_norm_g, w_out0, wgu2, wd2)

    w_in1, wgk, bgk = _layer1_layout(dict(w_in=l1_w_in, w_gk_f=l1_w_gk_f, b_gk_f=l1_b_gk_f,
                                          w_gk_b=l1_w_gk_b, b_gk_b=l1_b_gk_b))
    mod, mod_c = mods(l1_w_mod, l1_b_mod)
    wgu1, wd1 = l1_ffn1_w_gu.astype(BF16), l1_ffn1_w_down.astype(BF16)
    wgu2, wd2 = l1_ffn2_w_gu.astype(BF16), l1_ffn2_w_down.astype(BF16)

    x = _ffn(x, mod, l1_norm_g, wgu1, wd1, 0)
    xc = _ffn(xc, mod_c, l1_norm_g, wgu1, wd1, 0)
    k1, v1, q1, gt1, gf1, gb1 = _proj1(x, mod, l1_norm_g, w_in1, wgk, bgk)
    k1c, v1c, q1c, _, gf1c, gb1c = _proj1(xc, mod_c, l1_norm_g, w_in1, wgk, bgk)
    zero_state = jnp.zeros((bsz, GLA_HEADS, GLA_DV_HEAD, GLA_DK_HEAD), F32)
    _, _, s_f, s_b = _gla_scan(q1c, k1c, v1c, gf1c, gb1c, zero_state, zero_state)
    o_f, o_b, _, _ = _gla_scan(q1, k1, v1, gf1, gb1, s_f, s_b)
    x = _gla_out_ffn(x, o_f, o_b, gt1, l1_g_norm.reshape(1, GLA_DV_HEAD), mod, l1_norm_g,
                     l1_w_out.astype(BF16), wgu2, wd2)
    return x.astype(out_dtype)
```

```python
import functools

import numpy as np
import jax
import jax.numpy as jnp
from jax import lax
from jax.experimental import pallas as pl
from jax.experimental.pallas import tpu as pltpu

F32 = jnp.float32
BF16 = jnp.bfloat16

D_MODEL = 1024
GRID_W = 64
N_MOD = 9
EPS = 1e-6
ROPE_BASE = 10000.0
NEG_INF = -1e30
D_FF = 2816
LOG2E = 1.4426950408889634

MLA_HEADS = 8
MLA_Q_RANK = 256
MLA_KV_RANK = 128
MLA_NOPE_DIM = 64
MLA_ROPE_DIM = 32
MLA_V_DIM = 64
MLA_QK_DIM = MLA_NOPE_DIM + MLA_ROPE_DIM
MLA_SCALE = MLA_QK_DIM ** -0.5

WIN_HEADS = 8
WIN_KV_HEADS = 2
WIN_GROUP = WIN_HEADS // WIN_KV_HEADS
WIN_HEAD_DIM = 64
WINDOW = 128
WIN_BLOCK = 128
WIN_SCALE = WIN_HEAD_DIM ** -0.5

GLA_HEADS = 4
GLA_DK = D_MODEL // 2
GLA_DV = D_MODEL
GLA_DK_HEAD = GLA_DK // GLA_HEADS
GLA_DV_HEAD = GLA_DV // GLA_HEADS
GLA_LOWRANK = 16
GLA_GATE_NORM = 16.0
GLA_CHUNK = 64

LANES = 128
FF_CHUNK = 256
MLA_TQ = 1024
MLA_TK = 512
WIN_TQ = 512
VMEM_LIMIT = 58 * 1024 * 1024

P0_CKV, P0_WK, P0_CQ, P0_WQ, P0_KR, P0_COLS = 0, 128, 256, 512, 1024, 1152
P1_K, P1_V, P1_Q, P1_G, P1_LOW, P1_COLS = 0, 512, 1536, 2048, 3072, 3200


def _dot(a, b):
    return jnp.dot(a, b, preferred_element_type=F32)


def _dot_nt(a, b):
    return lax.dot_general(a, b, (((1,), (1,)), ((), ())), preferred_element_type=F32)


def _dot_tn(a, b):
    return lax.dot_general(a, b, (((0,), (0,)), ((), ())), preferred_element_type=F32)


def _params(*sem):
    return pltpu.CompilerParams(dimension_semantics=sem, vmem_limit_bytes=VMEM_LIMIT)


def _resident(shape):
    nd = len(shape)
    return pl.BlockSpec(shape, lambda *_: (0,) * nd, pipeline_mode=pl.Buffered(1))


def _silu(x):
    return x / (1.0 + jnp.exp(-x))


def _pre_mod(x, mod_ref, ng_ref, sub):
    shift = mod_ref[0, 3 * sub:3 * sub + 1, :]
    scale = mod_ref[0, 3 * sub + 1:3 * sub + 2, :]
    gate = mod_ref[0, 3 * sub + 2:3 * sub + 3, :]
    rinv = lax.rsqrt(jnp.mean(x * x, axis=-1, keepdims=True) + EPS)
    h = x * rinv * ng_ref[sub:sub + 1, :] * (1.0 + scale) + shift
    return h.astype(BF16), gate


def _swiglu(hb, wgu_ref, wd_ref, act_ref):
    for c in range(D_FF // FF_CHUNK):
        lo = c * FF_CHUNK
        g = _dot(hb, wgu_ref[:, lo:lo + FF_CHUNK])
        u = _dot(hb, wgu_ref[:, D_FF + lo:D_FF + lo + FF_CHUNK])
        act_ref[:, lo:lo + FF_CHUNK] = (_silu(g) * u).astype(BF16)
    return _dot(act_ref[...], wd_ref[...])


def _mod_kernel(c_ref, w_ref, b_ref, o_ref):
    a = _silu(c_ref[...]).astype(BF16)
    o_ref[...] = _dot(a, w_ref[...].astype(BF16)) + b_ref[...]


def _mod_table(cc, w_mod, b_mod):
    rows = cc.shape[0]
    n = w_mod.shape[1]
    bn = 1152
    return pl.pallas_call(
        _mod_kernel,
        grid=(n // bn,),
        in_specs=[pl.BlockSpec((rows, D_MODEL), lambda j: (0, 0)),
                  pl.BlockSpec((D_MODEL, bn), lambda j: (0, j)),
                  pl.BlockSpec((1, bn), lambda j: (0, j))],
        out_specs=pl.BlockSpec((rows, bn), lambda j: (0, j)),
        out_shape=jax.ShapeDtypeStruct((rows, n), F32),
        compiler_params=_params("arbitrary"),
        name="mod_table",
    )(cc, w_mod, b_mod.reshape(1, n))


def _ffn_kernel(x_ref, mod_ref, ng_ref, wgu_ref, wd_ref, o_ref, act_ref, *, sub):
    x = x_ref[0]
    hb, gate = _pre_mod(x, mod_ref, ng_ref, sub)
    o_ref[0] = x + 0.5 * gate * _swiglu(hb, wgu_ref, wd_ref, act_ref)


def _attn_out_ffn_kernel(x_ref, oa_ref, ob_ref, mod_ref, ng_ref, wo_ref, wgu_ref, wd_ref, o_ref, act_ref):
    half = oa_ref.shape[-1]
    y = _dot(oa_ref[0], wo_ref[:half, :]) + _dot(ob_ref[0], wo_ref[half:, :])
    x = x_ref[0] + mod_ref[0, 5:6, :] * y
    hb, gate = _pre_mod(x, mod_ref, ng_ref, 2)
    o_ref[0] = x + 0.5 * gate * _swiglu(hb, wgu_ref, wd_ref, act_ref)


def _gla_out_ffn_kernel(x_ref, of_ref, ob_ref, gt_ref, gn_ref, mod_ref, ng_ref, wo_ref, wgu_ref, wd_ref,
                        o_ref, act_ref, on_ref):
    for h in range(GLA_HEADS):
        hs = slice(h * GLA_DV_HEAD, (h + 1) * GLA_DV_HEAD)
        o = of_ref[0, :, hs] + ob_ref[0, :, hs]
        rinv = lax.rsqrt(jnp.mean(o * o, axis=-1, keepdims=True) + EPS)
        on_ref[:, hs] = (o * rinv * gn_ref[...] * gt_ref[0, :, hs]).astype(BF16)
    x = x_ref[0] + mod_ref[0, 5:6, :] * _dot(on_ref[...], wo_ref[...])
    hb, gate = _pre_mod(x, mod_ref, ng_ref, 2)
    o_ref[0] = x + 0.5 * gate * _swiglu(hb, wgu_ref, wd_ref, act_ref)


def _tok_spec(tm, width):
    return pl.BlockSpec((1, tm, width), lambda b, i: (b, i, 0))


def _mod_spec(mod):
    if mod.shape[0] == 1:
        return pl.BlockSpec((1, N_MOD, D_MODEL), lambda b, i: (0, 0, 0))
    return pl.BlockSpec((1, N_MOD, D_MODEL), lambda b, i: (b, 0, 0))


def _row_tile(t, pref=512):
    return pref if t % pref == 0 else t


def _ffn(x, mod, ng, wgu, wd, sub):
    bx, t, _ = x.shape
    tm = _row_tile(t)
    return pl.pallas_call(
        functools.partial(_ffn_kernel, sub=sub),
        grid=(bx, t // tm),
        in_specs=[_tok_spec(tm, D_MODEL), _mod_spec(mod), _resident((3, D_MODEL)),
                  _resident(wgu.shape), _resident(wd.shape)],
        out_specs=_tok_spec(tm, D_MODEL),
        out_shape=jax.ShapeDtypeStruct(x.shape, F32),
        scratch_shapes=[pltpu.VMEM((tm, D_FF), BF16)],
        compiler_params=_params("parallel", "parallel"),
        name="ffn",
    )(x, mod, ng, wgu, wd)


def _attn_out_ffn(x, oa, ob, mod, ng, wo, wgu, wd):
    bx, t, _ = x.shape
    tm = _row_tile(t)
    return pl.pallas_call(
        _attn_out_ffn_kernel,
        grid=(bx, t // tm),
        in_specs=[_tok_spec(tm, D_MODEL), _tok_spec(tm, oa.shape[-1]), _tok_spec(tm, ob.shape[-1]),
                  _mod_spec(mod), _resident((3, D_MODEL)), _resident(wo.shape),
                  _resident(wgu.shape), _resident(wd.shape)],
        out_specs=_tok_spec(tm, D_MODEL),
        out_shape=jax.ShapeDtypeStruct(x.shape, F32),
        scratch_shapes=[pltpu.VMEM((tm, D_FF), BF16)],
        compiler_params=_params("parallel", "parallel"),
        name="attn_out_ffn",
    )(x, oa, ob, mod, ng, wo, wgu, wd)


def _gla_out_ffn(x, of, ob, gt, gn, mod, ng, wo, wgu, wd):
    bx, t, _ = x.shape
    tm = _row_tile(t)
    return pl.pallas_call(
        _gla_out_ffn_kernel,
        grid=(bx, t // tm),
        in_specs=[_tok_spec(tm, D_MODEL), _tok_spec(tm, GLA_DV), _tok_spec(tm, GLA_DV), _tok_spec(tm, GLA_DV),
                  _resident(gn.shape), _mod_spec(mod), _resident((3, D_MODEL)), _resident(wo.shape),
                  _resident(wgu.shape), _resident(wd.shape)],
        out_specs=_tok_spec(tm, D_MODEL),
        out_shape=jax.ShapeDtypeStruct(x.shape, F32),
        scratch_shapes=[pltpu.VMEM((tm, D_FF), BF16), pltpu.VMEM((tm, GLA_DV), BF16)],
        compiler_params=_params("parallel", "parallel"),
        name="gla_out_ffn",
    )(x, of, ob, gt, gn, mod, ng, wo, wgu, wd)


def _lane(shape):
    return lax.broadcasted_iota(jnp.int32, shape, 1)


def _rope(x, cos, sin):
    return x * cos + pltpu.roll(x, LANES // 2, 1) * sin


def _ones_row(vt):
    row = lax.broadcasted_iota(jnp.int32, vt.shape, 0)
    return jnp.where((row & (LANES - 1)) == MLA_V_DIM, 1.0, vt).astype(BF16)


def _proj0_kernel(x_ref, mod_ref, ng_ref, win_ref, wuk_ref, wvt_ref, wuq_ref, wwvt_ref, seg_ref, gv_ref,
                  cm_ref, sm_ref, cw_ref, sw_ref, qa_ref, ka_ref, va_ref, qw_ref, kw_ref, vw_ref):
    hb, _ = _pre_mod(x_ref[0], mod_ref, ng_ref, 1)
    z = _dot(hb, win_ref[...])
    tm = z.shape[0]
    cm, sm, cw, sw = cm_ref[...], sm_ref[...], cw_ref[...], sw_ref[...]
    g_kva, g_qa, g_kn2 = gv_ref[0:1, :LANES], gv_ref[1:2, :], gv_ref[2:3, :]
    g_kr, g_q2, g_wq2 = gv_ref[3:4, :LANES], gv_ref[4:5, :], gv_ref[5:6, :]
    g_wk, n_q2 = gv_ref[6:7, :LANES], gv_ref[7:8, :]

    def seg_rinv(v, seg, inv_n):
        return lax.rsqrt(_dot((v * v).astype(BF16), seg) * inv_n + EPS)

    ckv = z[:, P0_CKV:P0_CKV + MLA_KV_RANK]
    ckv = (ckv * lax.rsqrt(jnp.mean(ckv * ckv, axis=-1, keepdims=True) + EPS) * g_kva).astype(BF16)
    kv = _dot(ckv, wuk_ref[...])
    va_ref[0, 0] = _ones_row(_dot_nt(wvt_ref[...], ckv))
    kr = z[:, P0_KR:P0_KR + LANES]
    kr = kr * lax.rsqrt(jnp.sum(kr * kr, axis=-1, keepdims=True) * (1.0 / MLA_ROPE_DIM) + EPS) * g_kr
    kr = _rope(kr, cm, sm)
    kr2 = jnp.concatenate([kr, kr], axis=-1)
    for p in range(MLA_HEADS // 2):
        cols = slice(p * 2 * LANES, (p + 1) * 2 * LANES)
        k2 = kv[:, cols]
        ka_ref[0, :, cols] = (k2 * seg_rinv(k2, seg_ref[0], 1.0 / MLA_NOPE_DIM) * g_kn2 + kr2).astype(BF16)

    cq = z[:, P0_CQ:P0_CQ + MLA_Q_RANK]
    cq = (cq * lax.rsqrt(jnp.mean(cq * cq, axis=-1, keepdims=True) + EPS) * g_qa).astype(BF16)
    q = _dot(cq, wuq_ref[...])
    for p in range(MLA_HEADS // 2):
        q2 = q[:, p * 2 * LANES:(p + 1) * 2 * LANES]
        q2 = q2 * seg_rinv(q2, seg_ref[0], n_q2) * g_q2
        for h in range(2):
            lo = (2 * p + h) * LANES
            qa_ref[0, :, lo:lo + LANES] = _rope(q2[:, h * LANES:(h + 1) * LANES], cm, sm).astype(BF16)

    for p in range(WIN_GROUP // 2):
        w2 = z[:, P0_WQ + p * 2 * LANES:P0_WQ + (p + 1) * 2 * LANES]
        w2 = w2 * seg_rinv(w2, seg_ref[1], 1.0 / WIN_HEAD_DIM) * g_wq2
        for h in range(2):
            lo = (2 * p + h) * LANES
            qw_ref[0, :, lo:lo + LANES] = _rope(w2[:, h * LANES:(h + 1) * LANES], cw, sw).astype(BF16)
    wk = z[:, P0_WK:P0_WK + LANES]
    wk = wk * seg_rinv(wk, seg_ref[1, :LANES, :LANES], 1.0 / WIN_HEAD_DIM) * g_wk
    kw_ref[0] = _rope(wk, cw, sw).astype(BF16)
    vwt = _ones_row(_dot_nt(wwvt_ref[...], hb))
    for c in range(tm // WIN_BLOCK):
        vw_ref[0, c] = vwt[:, c * WIN_BLOCK:(c + 1) * WIN_BLOCK]


def _proj0(x, mod, ng, win, wuk, wvt, wuq, wwvt, seg, gv, tabs):
    bx, t, _ = x.shape
    tm = _row_tile(t, MLA_TK)
    ntab = tabs[0].shape[0] // tm
    tab_spec = pl.BlockSpec((tm, LANES), lambda b, i: (i % ntab, 0))
    nwb = tm // WIN_BLOCK
    out_specs = [_tok_spec(tm, MLA_HEADS * LANES), _tok_spec(tm, MLA_HEADS * LANES),
                 pl.BlockSpec((1, 1, MLA_HEADS * LANES, tm), lambda b, i: (b, i, 0, 0)),
                 _tok_spec(tm, WIN_HEADS * WIN_HEAD_DIM), _tok_spec(tm, LANES),
                 pl.BlockSpec((1, nwb, 2 * LANES, WIN_BLOCK), lambda b, i: (b, i, 0, 0))]
    out_shape = [jax.ShapeDtypeStruct((bx, t, MLA_HEADS * LANES), BF16),
                 jax.ShapeDtypeStruct((bx, t, MLA_HEADS * LANES), BF16),
                 jax.ShapeDtypeStruct((bx, t // tm, MLA_HEADS * LANES, tm), BF16),
                 jax.ShapeDtypeStruct((bx, t, WIN_HEADS * WIN_HEAD_DIM), BF16),
                 jax.ShapeDtypeStruct((bx, t, LANES), BF16),
                 jax.ShapeDtypeStruct((bx, t // WIN_BLOCK, 2 * LANES, WIN_BLOCK), BF16)]
    return pl.pallas_call(
        _proj0_kernel,
        grid=(bx, t // tm),
        in_specs=[_tok_spec(tm, D_MODEL), _mod_spec(mod), _resident((3, D_MODEL)), _resident(win.shape),
                  _resident(wuk.shape), _resident(wvt.shape), _resident(wuq.shape), _resident(wwvt.shape),
                  _resident(seg.shape), _resident(gv.shape)] + [tab_spec] * 4,
        out_specs=out_specs,
        out_shape=out_shape,
        compiler_params=_params("parallel", "parallel"),
        name="proj0",
    )(x, mod, ng, win, wuk, wvt, wuq, wwvt, seg, gv, *tabs)


def _mla_attn_kernel(*refs, has_latent):
    if has_latent:
        q_ref, kc_ref, vc_ref, k_ref, v_ref, o_ref, m_ref, acc_ref, s_ref, mc_ref = refs
    else:
        q_ref, kc_ref, vc_ref, o_ref, m_ref, acc_ref = refs

    def scores(kblk, hh):
        return _dot_nt(kblk[:, hh * LANES:(hh + 1) * LANES], q_ref[0, :, hh * LANES:(hh + 1) * LANES])

    def qk(j, slot):
        kblk = k_ref[0, pl.ds(pl.multiple_of(j * MLA_TK, MLA_TK), MLA_TK), :]
        for hh in range(2):
            st = scores(kblk, hh)
            s_ref[slot, hh] = st
            mc_ref[slot, hh] = jnp.max(st, axis=0, keepdims=True)

    def process(j, slot):
        vt = v_ref[0, j]
        for hh in range(2):
            m_prev = m_ref[hh]
            m_new = jnp.maximum(m_prev, mc_ref[slot, hh])
            p = jnp.exp2(s_ref[slot, hh] - m_new).astype(BF16)
            pv = _dot(vt[hh * LANES:(hh + 1) * LANES], p)
            acc_ref[hh] = acc_ref[hh] * jnp.exp2(m_prev - m_new) + pv
            m_ref[hh] = m_new

    def first_tile():
        sts = [scores(kc_ref[0], hh) for hh in range(2)]
        if has_latent:
            qk(0, 0)
        for hh in range(2):
            m_new = jnp.max(sts[hh], axis=0, keepdims=True)
            p = jnp.exp2(sts[hh] - m_new).astype(BF16)
            acc_ref[hh] = _dot(vc_ref[0, hh * LANES:(hh + 1) * LANES], p)
            m_ref[hh] = m_new

    first_tile()
    if has_latent:
        n = v_ref.shape[1]

        def body(i, carry):
            qk(2 * i + 1, 1)
            process(2 * i, 0)
            qk(2 * i + 2, 0)
            process(2 * i + 1, 1)
            return carry
        lax.fori_loop(0, n // 2 - 1, body, 0)
        qk(n - 1, 1)
        process(n - 2, 0)
        process(n - 1, 1)
    a0, a1 = acc_ref[0], acc_ref[1]
    ot = jnp.concatenate([a0[:MLA_V_DIM] / a0[MLA_V_DIM:MLA_V_DIM + 1],
                          a1[:MLA_V_DIM] / a1[MLA_V_DIM:MLA_V_DIM + 1]], axis=0)
    o_ref[0] = ot.T.astype(BF16)


def _mla_attn(q, kc, vtc, k=None, vt=None):
    bsz, t, _ = q.shape
    nc = kc.shape[1]
    has_latent = k is not None
    tq = _row_tile(t, MLA_TQ)
    in_specs = [pl.BlockSpec((1, tq, 2 * LANES), lambda b, h, i: (b, i, h)),
                pl.BlockSpec((1, nc, 2 * LANES), lambda b, h, i: (b, 0, h)),
                pl.BlockSpec((1, 2 * LANES, nc), lambda b, h, i: (b, h, 0))]
    args = [q, kc, vtc]
    scratch = [pltpu.VMEM((2, 1, tq), F32), pltpu.VMEM((2, LANES, tq), F32)]
    if has_latent:
        n = k.shape[1]
        nt, _, tk = vt.shape[1:]
        assert tk == MLA_TK and nt * tk == n and nt % 2 == 0, (n, nt, tk)
        in_specs += [pl.BlockSpec((1, n, 2 * LANES), lambda b, h, i: (b, 0, h)),
                     pl.BlockSpec((1, nt, 2 * LANES, tk), lambda b, h, i: (b, 0, h, 0))]
        args += [k, vt]
        scratch += [pltpu.VMEM((2, 2, tk, tq), F32), pltpu.VMEM((2, 2, 1, tq), F32)]
    return pl.pallas_call(
        functools.partial(_mla_attn_kernel, has_latent=has_latent),
        grid=(bsz, MLA_HEADS // 2, t // tq),
        in_specs=in_specs,
        out_specs=pl.BlockSpec((1, tq, LANES), lambda b, h, i: (b, i, h)),
        out_shape=jax.ShapeDtypeStruct((bsz, t, MLA_HEADS * MLA_V_DIM), BF16),
        scratch_shapes=scratch,
        compiler_params=_params("parallel", "parallel", "arbitrary"),
        name="mla_attn" if has_latent else "mla_attn_ctx",
    )(*args)


def _win_attn_kernel(*refs, has_window, nb):
    if has_window:
        sink_ref, q_ref, kc_ref, vc_ref, k_ref, vt_ref, bias_ref, o_ref, s_ref, mc_ref = refs
    else:
        sink_ref, q_ref, kc_ref, vc_ref, o_ref, s_ref, mc_ref = refs
    qb = q_ref.shape[1] // WIN_BLOCK
    nct = vc_ref.shape[1]
    i = pl.program_id(1)
    units = [(blk, n) for blk in range(qb) for n in range(WIN_KV_HEADS)]
    cache = {}

    def block_operands(blk):
        if blk in cache:
            return cache[blk]
        vts = [vc_ref[0, t] for t in range(nct)]
        bias = None
        if has_window:
            g = i * qb + blk
            start = jnp.clip(g - 1, 0, nb - 3)
            kwin = k_ref[0, pl.ds(pl.multiple_of(start * WIN_BLOCK, WIN_BLOCK), 3 * WIN_BLOCK), :]
            kall = jnp.concatenate([kc_ref[0], kwin], axis=0)
            vts += [vt_ref[0, start + t] for t in range(3)]
            bias = jnp.concatenate([bias_ref[g - start]] * WIN_GROUP, axis=1)
        else:
            kall = kc_ref[0]
        rows = slice(blk * WIN_BLOCK, (blk + 1) * WIN_BLOCK)
        qs = jnp.concatenate([q_ref[0, rows, j * LANES:(j + 1) * LANES] for j in range(WIN_GROUP)], axis=0)
        cache[blk] = (kall, jnp.concatenate(vts, axis=1), qs, bias)
        return cache[blk]

    def qk(u, slot):
        blk, n = units[u]
        kall, _, qs, bias = block_operands(blk)
        head0 = (_lane(qs.shape) & 32) == 0
        zero = jnp.zeros_like(qs)
        st = _dot_nt(kall, jnp.where(head0, qs, zero) if n == 0 else jnp.where(head0, zero, qs))
        if bias is not None:
            st = st + bias
        s_ref[slot] = st
        mc_ref[slot] = jnp.max(st, axis=0, keepdims=True)

    outs = {}

    def process(u, slot):
        blk, n = units[u]
        vall = block_operands(blk)[1]
        sink = sink_ref[n]
        m = jnp.maximum(mc_ref[slot], sink)
        p = jnp.exp2(s_ref[slot] - m).astype(BF16)
        pv = _dot(vall[n * LANES:(n + 1) * LANES], p)
        denom = pv[WIN_HEAD_DIM:WIN_HEAD_DIM + 1] + jnp.exp2(sink - m)
        outs[(blk, n)] = pv[:WIN_HEAD_DIM] / denom
        if n == WIN_KV_HEADS - 1:
            rows = slice(blk * WIN_BLOCK, (blk + 1) * WIN_BLOCK)
            for j in range(WIN_GROUP):
                cols = slice(j * WIN_BLOCK, (j + 1) * WIN_BLOCK)
                both = jnp.concatenate([outs[(blk, 0)][:, cols], outs[(blk, 1)][:, cols]], axis=0)
                o_ref[0, rows, j * LANES:(j + 1) * LANES] = both.T.astype(BF16)

    qk(0, 0)
    for u in range(len(units)):
        if u + 1 < len(units):
            qk(u + 1, (u + 1) % 2)
        process(u, u % 2)


def _band_bias(nc):
    krow = np.arange(nc + 3 * WIN_BLOCK)[:, None]
    qcol = np.arange(WIN_BLOCK)[None, :]
    out = []
    for off in range(3):
        rel = (krow - nc) - (qcol + off * WIN_BLOCK)
        out.append((krow < nc) | (np.abs(rel) <= WINDOW))
    return jnp.asarray(np.where(np.stack(out), 0.0, NEG_INF), F32)


def _win_attn(sinkrow, q, kc, vtc, k=None, vt=None):
    bsz, t, _ = q.shape
    nc = kc.shape[1]
    has_window = k is not None
    nb = t // WIN_BLOCK
    tq = _row_tile(t, WIN_TQ)
    kvw = WIN_KV_HEADS * WIN_HEAD_DIM
    nk = nc + (3 * WIN_BLOCK if has_window else 0)
    in_specs = [pl.BlockSpec((WIN_KV_HEADS, 1, WIN_GROUP * WIN_BLOCK), lambda b, i: (0, 0, 0)),
                pl.BlockSpec((1, tq, WIN_HEADS * WIN_HEAD_DIM), lambda b, i: (b, i, 0)),
                pl.BlockSpec((1, nc, kvw), lambda b, i: (b, 0, 0)),
                pl.BlockSpec((1, nc // WIN_BLOCK, 2 * LANES, WIN_BLOCK), lambda b, i: (b, 0, 0, 0))]
    args = [sinkrow, q, kc, vtc]
    if has_window:
        assert nb >= 3, nb
        in_specs += [pl.BlockSpec((1, t, kvw), lambda b, i: (b, 0, 0)),
                     pl.BlockSpec((1, nb, 2 * LANES, WIN_BLOCK), lambda b, i: (b, 0, 0, 0)),
                     pl.BlockSpec((3, nk, WIN_BLOCK), lambda b, i: (0, 0, 0))]
        args += [k, vt, _band_bias(nc)]
    return pl.pallas_call(
        functools.partial(_win_attn_kernel, has_window=has_window, nb=nb),
        grid=(bsz, t // tq),
        in_specs=in_specs,
        out_specs=pl.BlockSpec((1, tq, WIN_HEADS * WIN_HEAD_DIM), lambda b, i: (b, i, 0)),
        out_shape=jax.ShapeDtypeStruct((bsz, t, WIN_HEADS * WIN_HEAD_DIM), BF16),
        scratch_shapes=[pltpu.VMEM((2, nk, WIN_GROUP * WIN_BLOCK), F32),
                        pltpu.VMEM((2, 1, WIN_GROUP * WIN_BLOCK), F32)],
        compiler_params=_params("parallel", "parallel"),
        name="win_attn" if has_window else "win_attn_ctx",
    )(*args)


def _log_sigmoid(x):
    return jnp.minimum(x, 0.0) - jnp.log1p(jnp.exp(-jnp.abs(x)))


def _proj1_kernel(x_ref, mod_ref, ng_ref, win_ref, wgk_ref, bgk_ref, k_ref, v_ref, q_ref, gt_ref, gf_ref, gb_ref):
    hb, _ = _pre_mod(x_ref[0], mod_ref, ng_ref, 1)
    z = _dot(hb, win_ref[...])
    k_ref[0] = z[:, P1_K:P1_K + GLA_DK]
    v_ref[0] = z[:, P1_V:P1_V + GLA_DV].astype(BF16)
    q_ref[0] = z[:, P1_Q:P1_Q + GLA_DK] * (GLA_DK_HEAD ** -0.5)
    gt_ref[0] = _silu(z[:, P1_G:P1_G + GLA_DV])
    pre = _dot(z[:, P1_LOW:P1_LOW + LANES].astype(BF16), wgk_ref[...]) + bgk_ref[...]
    gate = _log_sigmoid(pre) * (1.0 / GLA_GATE_NORM)
    gf_ref[0] = gate[:, :GLA_DK]
    gb_ref[0] = gate[:, GLA_DK:]


def _proj1(x, mod, ng, win, wgk, bgk):
    bx, t, _ = x.shape
    tm = _row_tile(t)
    outs = ((GLA_DK, F32), (GLA_DV, BF16), (GLA_DK, F32), (GLA_DV, F32), (GLA_DK, F32), (GLA_DK, F32))
    return pl.pallas_call(
        _proj1_kernel,
        grid=(bx, t // tm),
        in_specs=[_tok_spec(tm, D_MODEL), _mod_spec(mod), _resident((3, D_MODEL)), _resident(win.shape),
                  _resident(wgk.shape), _resident(bgk.shape)],
        out_specs=[_tok_spec(tm, w) for w, _ in outs],
        out_shape=[jax.ShapeDtypeStruct((bx, t, w), d) for w, d in outs],
        compiler_params=_params("parallel", "parallel"),
        name="proj1",
    )(x, mod, ng, win, wgk, bgk)


def _gla_chunk(q_ref, k_ref, v_ref, g_ref, o_ref, st_ref, r0, keep, ones, backward):
    rows = slice(r0, r0 + GLA_CHUNK)
    g = g_ref[0, rows, :]
    g_hi = g.astype(BF16)
    g_lo = (g - g_hi.astype(F32)).astype(BF16)
    bb = _dot(ones, g_hi) + _dot(ones, g_lo)
    b_end = bb[0:1, :] if backward else bb[GLA_CHUNK - 1:GLA_CHUNK, :]
    q = q_ref[0, rows, :]
    k = k_ref[0, rows, :]
    q_dec = (q * jnp.exp(bb)).astype(BF16)
    k_inv = (k * jnp.exp(-bb)).astype(BF16)
    k_end = (k * jnp.exp(b_end - bb)).astype(BF16)
    decay = jnp.exp(b_end)
    for h in range(GLA_HEADS):
        ks = slice(h * GLA_DK_HEAD, (h + 1) * GLA_DK_HEAD)
        vs = slice(h * GLA_DV_HEAD, (h + 1) * GLA_DV_HEAD)
        vh = v_ref[0, rows, vs]
        a = jnp.where(keep, _dot_nt(q_dec[:, ks], k_inv[:, ks]), 0.0).astype(BF16)
        st = st_ref[h]
        o_ref[0, rows, vs] = _dot(a, vh) + _dot_nt(q_dec[:, ks], st.astype(BF16))
        st_ref[h] = st * decay[:, ks] + _dot_tn(vh, k_end[:, ks])


def _gla_scan_kernel(qf_ref, kf_ref, vf_ref, gf_ref, qb_ref, kb_ref, vb_ref, gb_ref, s0f_ref, s0b_ref,
                     of_ref, ob_ref, sf_ref, sb_ref, stf_ref, stb_ref, *, n_sub):
    i = pl.program_id(1)

    @pl.when(i == 0)
    def _():
        stf_ref[...] = s0f_ref[0]
        stb_ref[...] = s0b_ref[0]

    row = lax.broadcasted_iota(jnp.int32, (GLA_CHUNK, GLA_CHUNK), 0)
    col = lax.broadcasted_iota(jnp.int32, (GLA_CHUNK, GLA_CHUNK), 1)
    lower = col <= row
    upper = col >= row
    ones_lo = jnp.where(lower, 1.0, 0.0).astype(BF16)
    ones_up = jnp.where(upper, 1.0, 0.0).astype(BF16)
    for c in range(n_sub):
        _gla_chunk(qf_ref, kf_ref, vf_ref, gf_ref, of_ref, stf_ref, c * GLA_CHUNK, lower, ones_lo, False)
        _gla_chunk(qb_ref, kb_ref, vb_ref, gb_ref, ob_ref, stb_ref, (n_sub - 1 - c) * GLA_CHUNK, upper, ones_up, True)

    @pl.when(i == pl.num_programs(1) - 1)
    def _():
        sf_ref[0] = stf_ref[...]
        sb_ref[0] = stb_ref[...]


def _gla_scan(q, k, v, gf, gb, s0f, s0b):
    bsz, t, _ = q.shape
    tb = 256 if t % 256 == 0 else GLA_CHUNK
    nblk = t // tb
    fwd = lambda w: pl.BlockSpec((1, tb, w), lambda b, i: (b, i, 0))
    bwd = lambda w: pl.BlockSpec((1, tb, w), lambda b, i: (b, nblk - 1 - i, 0))
    st_spec = pl.BlockSpec((1, GLA_HEADS, GLA_DV_HEAD, GLA_DK_HEAD), lambda b, i: (b, 0, 0, 0))
    st_shape = jax.ShapeDtypeStruct((bsz, GLA_HEADS, GLA_DV_HEAD, GLA_DK_HEAD), F32)
    return pl.pallas_call(
        functools.partial(_gla_scan_kernel, n_sub=tb // GLA_CHUNK),
        grid=(bsz, nblk),
        in_specs=[fwd(GLA_DK), fwd(GLA_DK), fwd(GLA_DV), fwd(GLA_DK),
                  bwd(GLA_DK), bwd(GLA_DK), bwd(GLA_DV), bwd(GLA_DK), st_spec, st_spec],
        out_specs=[fwd(GLA_DV), bwd(GLA_DV), st_spec, st_spec],
        out_shape=[jax.ShapeDtypeStruct((bsz, t, GLA_DV), F32)] * 2 + [st_shape] * 2,
        scratch_shapes=[pltpu.VMEM((GLA_HEADS, GLA_DV_HEAD, GLA_DK_HEAD), F32)] * 2,
        compiler_params=_params("parallel", "arbitrary"),
        name="gla_scan",
    )(q, k, v, gf, q, k, v, gb, s0f, s0b)


def _deinterleave(n):
    return np.concatenate([np.arange(0, n, 2), np.arange(1, n, 2)])


def _gather_cols(w, idx):
    idx = np.asarray(idx)
    cols = jnp.take(w, jnp.asarray(np.maximum(idx, 0)), axis=-1)
    return jnp.where(jnp.asarray(idx >= 0), cols, 0.0)


def _mla_slot(nope, rope):
    pad = np.full((16,), -1, np.int64)
    return np.concatenate([nope[:32], rope[:16], pad, nope[32:], rope[16:], pad])


def _win_slot(a, b):
    return np.concatenate([a[:32], b[:32], a[32:], b[32:]])


def _take(vec, idx):
    return jnp.where(jnp.asarray(idx >= 0), vec[jnp.asarray(np.maximum(idx, 0))], 0.0)


def _layer0_layout(p):
    ckv0, kr0 = 0, MLA_KV_RANK
    wk0 = kr0 + MLA_ROPE_DIM
    wv0 = wk0 + WIN_KV_HEADS * WIN_HEAD_DIM
    cq0 = wv0 + WIN_KV_HEADS * WIN_HEAD_DIM
    wq0 = cq0 + MLA_Q_RANK
    de64, de32 = _deinterleave(WIN_HEAD_DIM), _deinterleave(MLA_ROPE_DIM)
    none64, none32 = np.full((64,), -1, np.int64), np.full((32,), -1, np.int64)

    idx = np.full((P0_COLS,), -1, np.int64)
    idx[P0_CKV:P0_CKV + MLA_KV_RANK] = ckv0 + np.arange(MLA_KV_RANK)
    idx[P0_WK:P0_WK + LANES] = _win_slot(wk0 + de64, wk0 + 64 + de64)
    idx[P0_CQ:P0_CQ + MLA_Q_RANK] = cq0 + np.arange(MLA_Q_RANK)
    for j in range(WIN_GROUP):
        idx[P0_WQ + j * LANES:P0_WQ + (j + 1) * LANES] = _win_slot(wq0 + j * 64 + de64,
                                                                   wq0 + (WIN_GROUP + j) * 64 + de64)
    idx[P0_KR:P0_KR + LANES] = _mla_slot(none64, kr0 + de32)
    w_in = _gather_cols(p["w_in"], idx).astype(BF16)
    vidx = np.full((WIN_KV_HEADS * LANES,), -1, np.int64)
    for n in range(WIN_KV_HEADS):
        vidx[n * LANES:n * LANES + 64] = wv0 + n * 64 + np.arange(64)
    w_wvt = _gather_cols(p["w_in"], vidx).T.astype(BF16)

    per = MLA_NOPE_DIM + MLA_V_DIM
    kidx, qidx, vidx = [], [], []
    for h in range(MLA_HEADS):
        kidx.append(_mla_slot(h * per + np.arange(64), none32))
        qidx.append(_mla_slot(h * MLA_QK_DIM + np.arange(64), h * MLA_QK_DIM + MLA_NOPE_DIM + de32))
        vidx.append(np.concatenate([h * per + MLA_NOPE_DIM + np.arange(64), none64]))
    w_uk = _gather_cols(p["mla_w_ukv"], np.concatenate(kidx)).astype(BF16)
    w_vt = _gather_cols(p["mla_w_ukv"], np.concatenate(vidx)).T.astype(BF16)
    w_uq = _gather_cols(p["mla_w_uq"], np.concatenate(qidx)).astype(BF16)

    lane = np.arange(LANES)
    nope, rot = (lane & 32) == 0, (lane & 48) == 32
    seg_a = (nope[:, None] & nope[None, :]) | (rot[:, None] & rot[None, :])
    seg_b = nope[:, None] == nope[None, :]
    two = lambda m: np.kron(np.eye(2), m.astype(np.float32))
    seg = jnp.asarray(np.stack([two(seg_a), two(seg_b)]), BF16)

    a64, a32 = np.arange(64), np.arange(32)
    twice = lambda v: jnp.concatenate([v, v])
    wide = lambda v: jnp.concatenate([v, jnp.zeros((LANES,), F32)])
    g_wq = _take(p["win_g_q"], _win_slot(de64, de64)) * (WIN_SCALE * LOG2E)
    g_wk = _take(p["win_g_k"], _win_slot(de64, de64))
    g_qslot = jnp.concatenate([p["mla_g_qn"], p["mla_g_qr"]])
    inv_n = jnp.concatenate([jnp.full((64,), 1.0 / MLA_NOPE_DIM, F32), jnp.full((32,), 1.0 / MLA_ROPE_DIM, F32)])
    gv = jnp.stack([
        wide(p["mla_g_kva"]),
        p["mla_g_qa"],
        twice(_take(p["mla_g_kn"], _mla_slot(a64, none32))),
        wide(_take(p["mla_g_kr"], _mla_slot(none64, de32))),
        twice(_take(g_qslot, _mla_slot(a64, 64 + de32))) * (MLA_SCALE * LOG2E),
        twice(g_wq),
        wide(g_wk),
        twice(_take(inv_n, _mla_slot(a64, 64 + a32))),
    ])

    out_rows = np.arange(D_MODEL)
    base = MLA_HEADS * MLA_V_DIM
    for j in range(WIN_GROUP):
        for n in range(WIN_KV_HEADS):
            lo = base + j * LANES + n * 64
            out_rows[lo:lo + 64] = base + (n * WIN_GROUP + j) * 64 + np.arange(64)
    w_out = p["w_out"][jnp.asarray(out_rows)].astype(BF16)
    return (w_in, w_uk, w_vt, w_uq, w_wvt, seg, gv), w_out


def _layer1_layout(p):
    k0, v0 = 0, GLA_DK
    lf0 = v0 + GLA_DV
    lb0 = lf0 + GLA_LOWRANK
    q0 = lb0 + GLA_LOWRANK
    g0 = q0 + GLA_DK
    idx = np.full((P1_COLS,), -1, np.int64)
    idx[P1_K:P1_K + GLA_DK] = k0 + np.arange(GLA_DK)
    idx[P1_V:P1_V + GLA_DV] = v0 + np.arange(GLA_DV)
    idx[P1_Q:P1_Q + GLA_DK] = q0 + np.arange(GLA_DK)
    idx[P1_G:P1_G + GLA_DV] = g0 + np.arange(GLA_DV)
    idx[P1_LOW:P1_LOW + GLA_LOWRANK] = lf0 + np.arange(GLA_LOWRANK)
    idx[P1_LOW + GLA_LOWRANK:P1_LOW + 2 * GLA_LOWRANK] = lb0 + np.arange(GLA_LOWRANK)
    w_in = _gather_cols(p["w_in"], idx).astype(BF16)
    wgk = jnp.zeros((LANES, 2 * GLA_DK), F32)
    wgk = wgk.at[:GLA_LOWRANK, :GLA_DK].set(p["w_gk_f"])
    wgk = wgk.at[GLA_LOWRANK:2 * GLA_LOWRANK, GLA_DK:].set(p["w_gk_b"])
    bgk = jnp.concatenate([p["b_gk_f"], p["b_gk_b"]]).reshape(1, 2 * GLA_DK)
    return w_in, wgk.astype(BF16), bgk


def _rope_tables(seq):
    t = jnp.arange(seq)
    row = (t // GRID_W).astype(F32)
    col = (t % GRID_W).astype(F32)

    def angles(rot_dim):
        n_freq = rot_dim // 4
        inv = ROPE_BASE ** (-jnp.arange(n_freq, dtype=F32) / n_freq)
        return jnp.concatenate([row[:, None] * inv, col[:, None] * inv], axis=-1)

    am, aw = angles(MLA_ROPE_DIM), angles(WIN_HEAD_DIM)
    one32, one16 = jnp.ones((seq, 32), F32), jnp.ones((seq, 16), F32)
    cm = jnp.concatenate([one32, jnp.cos(am), one16] * 2, axis=-1)
    sm = jnp.concatenate([0 * one32, -jnp.sin(am), 0 * one16, 0 * one32, jnp.sin(am), 0 * one16], axis=-1)
    cw = jnp.concatenate([jnp.cos(aw)] * 4, axis=-1)
    sw = jnp.concatenate([-jnp.sin(aw), -jnp.sin(aw), jnp.sin(aw), jnp.sin(aw)], axis=-1)
    return cm, sm, cw, sw


def _identity_tables(rows):
    one, zero = jnp.ones((rows, LANES), F32), jnp.zeros((rows, LANES), F32)
    return one, zero, one, zero


def kernel(x, c, ctx, c_ctx, l0_norm_g, l0_w_mod, l0_b_mod, l0_ffn1_w_gu, l0_ffn1_w_down, l0_ffn2_w_gu, l0_ffn2_w_down, l0_w_in, l0_mla_g_qa, l0_mla_g_kva, l0_mla_w_uq, l0_mla_w_ukv, l0_mla_g_qn, l0_mla_g_qr, l0_mla_g_kn, l0_mla_g_kr, l0_win_g_q, l0_win_g_k, l0_win_sink, l0_w_out, l1_norm_g, l1_w_mod, l1_b_mod, l1_ffn1_w_gu, l1_ffn1_w_down, l1_ffn2_w_gu, l1_ffn2_w_down, l1_w_in, l1_w_gk_f, l1_b_gk_f, l1_w_gk_b, l1_b_gk_b, l1_g_norm, l1_w_out):
    bsz, seq, _ = x.shape
    n_ctx = ctx.shape[1]
    out_dtype = x.dtype
    x = x.astype(F32)
    xc = ctx.astype(F32)

    cc = jnp.zeros((16, D_MODEL), F32).at[:bsz].set(c).at[bsz].set(c_ctx)

    def mods(w_mod, b_mod):
        tab = _mod_table(cc, w_mod, b_mod).reshape(16, N_MOD, D_MODEL)
        return tab[:bsz], tab[bsz:bsz + 1]

    p0 = dict(w_in=l0_w_in, mla_g_qa=l0_mla_g_qa, mla_g_kva=l0_mla_g_kva, mla_w_uq=l0_mla_w_uq,
              mla_w_ukv=l0_mla_w_ukv, mla_g_qn=l0_mla_g_qn, mla_g_qr=l0_mla_g_qr, mla_g_kn=l0_mla_g_kn,
              mla_g_kr=l0_mla_g_kr, win_g_q=l0_win_g_q, win_g_k=l0_win_g_k, w_out=l0_w_out)
    proj_w, w_out0 = _layer0_layout(p0)
    mod, mod_c = mods(l0_w_mod, l0_b_mod)
    wgu1, wd1 = l0_ffn1_w_gu.astype(BF16), l0_ffn1_w_down.astype(BF16)
    wgu2, wd2 = l0_ffn2_w_gu.astype(BF16), l0_ffn2_w_down.astype(BF16)
    sinkrow = jnp.repeat(l0_win_sink.astype(F32) * LOG2E, WIN_BLOCK, axis=1).reshape(WIN_KV_HEADS, 1, -1)

    x = _ffn(x, mod, l0_norm_g, wgu1, wd1, 0)
    xc = _ffn(xc, mod_c, l0_norm_g, wgu1, wd1, 0)
    qa, ka, vat, qw, kw, vwt = _proj0(x, mod, l0_norm_g, *proj_w, _rope_tables(seq))
    qa_c, ka_c, vat_c, qw_c, kw_c, vwt_c = _proj0(xc, mod_c, l0_norm_g, *proj_w, _identity_tables(n_ctx))
    vat_c = vat_c.reshape(bsz, MLA_HEADS * LANES, n_ctx)
    oa = _mla_attn(qa, ka_c, vat_c, ka, vat)
    ow = _win_attn(sinkrow, qw, kw_c, vwt_c, kw, vwt)
    oa_c = _mla_attn(qa_c, ka_c, vat_c)
    ow_c = _win_attn(sinkrow, qw_c, kw_c, vwt_c)
    x = _attn_out_ffn(x, oa, ow, mod, l0_norm_g, w_out0, wgu2, wd2)
    xc = _attn_out_ffn(xc, oa_c, ow_c, mod_c, l0_norm_g, w_out0, wgu2, wd2)

    w_in1, wgk, bgk = _layer1_layout(dict(w_in=l1_w_in, w_gk_f=l1_w_gk_f, b_gk_f=l1_b_gk_f,
                                          w_gk_b=l1_w_gk_b, b_gk_b=l1_b_gk_b))
    mod, mod_c = mods(l1_w_mod, l1_b_mod)
    wgu1, wd1 = l1_ffn1_w_gu.astype(BF16), l1_ffn1_w_down.astype(BF16)
    wgu2, wd2 = l1_ffn2_w_gu.astype(BF16), l1_ffn2_w_down.astype(BF16)

    x = _ffn(x, mod, l1_norm_g, wgu1, wd1, 0)
    xc = _ffn(xc, mod_c, l1_norm_g, wgu1, wd1, 0)
    k1, v1, q1, gt1, gf1, gb1 = _proj1(x, mod, l1_norm_g, w_in1, wgk, bgk)
    k1c, v1c, q1c, _, gf1c, gb1c = _proj1(xc, mod_c, l1_norm_g, w_in1, wgk, bgk)
    zero_state = jnp.zeros((bsz, GLA_HEADS, GLA_DV_HEAD, GLA_DK_HEAD), F32)
    _, _, s_f, s_b = _gla_scan(q1c, k1c, v1c, gf1c, gb1c, zero_state, zero_state)
    o_f, o_b, _, _ = _gla_scan(q1, k1, v1, gf1, gb1, s_f, s_b)
    x = _gla_out_ffn(x, o_f, o_b, gt1, l1_g_norm.reshape(1, GLA_DV_HEAD), mod, l1_norm_g,
                     l1_w_out.astype(BF16), wgu2, wd2)
    return x.astype(out_dtype)
```

```python
import functools

import numpy as np
import jax
import jax.numpy as jnp
from jax import lax
from jax.experimental import pallas as pl
from jax.experimental.pallas import tpu as pltpu

F32 = jnp.float32
BF16 = jnp.bfloat16

D_MODEL = 1024
GRID_W = 64
N_MOD = 9
EPS = 1e-6
ROPE_BASE = 10000.0
NEG_INF = -1e30
D_FF = 2816
LOG2E = 1.4426950408889634

MLA_HEADS = 8
MLA_Q_RANK = 256
MLA_KV_RANK = 128
MLA_NOPE_DIM = 64
MLA_ROPE_DIM = 32
MLA_V_DIM = 64
MLA_QK_DIM = MLA_NOPE_DIM + MLA_ROPE_DIM
MLA_SCALE = MLA_QK_DIM ** -0.5

WIN_HEADS = 8
WIN_KV_HEADS = 2
WIN_GROUP = WIN_HEADS // WIN_KV_HEADS
WIN_HEAD_DIM = 64
WINDOW = 128
WIN_BLOCK = 128
WIN_SCALE = WIN_HEAD_DIM ** -0.5

GLA_HEADS = 4
GLA_DK = D_MODEL // 2
GLA_DV = D_MODEL
GLA_DK_HEAD = GLA_DK // GLA_HEADS
GLA_DV_HEAD = GLA_DV // GLA_HEADS
GLA_LOWRANK = 16
GLA_GATE_NORM = 16.0
GLA_CHUNK = 64
GLA_BLOCK = 256

LANES = 128
FF_CHUNK = 256
MLA_TQ = 1024
MLA_TK = 512
WIN_TQ = 512
VMEM_LIMIT = 58 * 1024 * 1024

P0_CKV, P0_KR, P0_CQ, P0_WQ, P0_WK, P0_COLS = 0, 128, 256, 512, 1024, 1152
P1_K, P1_V, P1_Q, P1_LOW, P1_G, P1_COLS = 0, 512, 1536, 2048, 2176, 3200


def _dot(a, b):
    return jnp.dot(a, b, preferred_element_type=F32)


def _dot_nt(a, b):
    return lax.dot_general(a, b, (((1,), (1,)), ((), ())), preferred_element_type=F32)


def _dot_tn(a, b):
    return lax.dot_general(a, b, (((0,), (0,)), ((), ())), preferred_element_type=F32)


def _params(*sem):
    return pltpu.CompilerParams(dimension_semantics=sem, vmem_limit_bytes=VMEM_LIMIT)


def _resident(shape):
    nd = len(shape)
    return pl.BlockSpec(shape, lambda *_: (0,) * nd, pipeline_mode=pl.Buffered(1))


def _silu(x):
    return x / (1.0 + jnp.exp(-x))


def _pre_mod(x, mod_ref, ng_ref, sub):
    shift = mod_ref[0, 3 * sub:3 * sub + 1, :]
    scale = mod_ref[0, 3 * sub + 1:3 * sub + 2, :]
    gate = mod_ref[0, 3 * sub + 2:3 * sub + 3, :]
    rinv = lax.rsqrt(jnp.mean(x * x, axis=-1, keepdims=True) + EPS)
    h = x * rinv * ng_ref[sub:sub + 1, :] * (1.0 + scale) + shift
    return h.astype(BF16), gate


def _swiglu(hb, wgu_ref, wd_ref, act_ref):
    for c in range(D_FF // FF_CHUNK):
        lo = c * FF_CHUNK
        g = _dot(hb, wgu_ref[:, lo:lo + FF_CHUNK])
        u = _dot(hb, wgu_ref[:, D_FF + lo:D_FF + lo + FF_CHUNK])
        act_ref[:, lo:lo + FF_CHUNK] = (_silu(g) * u).astype(BF16)
    return _dot(act_ref[...], wd_ref[...])


def _mod_kernel(c_ref, w_ref, b_ref, o_ref):
    a = _silu(c_ref[...]).astype(BF16)
    o_ref[...] = _dot(a, w_ref[...].astype(BF16)) + b_ref[...]


def _mod_table(cc, w_mod, b_mod):
    rows = cc.shape[0]
    n = w_mod.shape[1]
    bn = 1152
    return pl.pallas_call(
        _mod_kernel,
        grid=(n // bn,),
        in_specs=[pl.BlockSpec((rows, D_MODEL), lambda j: (0, 0)),
                  pl.BlockSpec((D_MODEL, bn), lambda j: (0, j)),
                  pl.BlockSpec((1, bn), lambda j: (0, j))],
        out_specs=pl.BlockSpec((rows, bn), lambda j: (0, j)),
        out_shape=jax.ShapeDtypeStruct((rows, n), F32),
        compiler_params=_params("arbitrary"),
        name="mod_table",
    )(cc, w_mod, b_mod.reshape(1, n))


def _ffn_kernel(x_ref, mod_ref, ng_ref, wgu_ref, wd_ref, o_ref, act_ref, *, sub):
    x = x_ref[0]
    hb, gate = _pre_mod(x, mod_ref, ng_ref, sub)
    o_ref[0] = x + 0.5 * gate * _swiglu(hb, wgu_ref, wd_ref, act_ref)


def _attn_out_ffn_kernel(x_ref, oa_ref, ob_ref, mod_ref, ng_ref, wo_ref, wgu_ref, wd_ref, o_ref, act_ref):
    half = oa_ref.shape[-1]
    y = _dot(oa_ref[0], wo_ref[:half, :]) + _dot(ob_ref[0], wo_ref[half:, :])
    x = x_ref[0] + mod_ref[0, 5:6, :] * y
    hb, gate = _pre_mod(x, mod_ref, ng_ref, 2)
    o_ref[0] = x + 0.5 * gate * _swiglu(hb, wgu_ref, wd_ref, act_ref)


def _gla_out_ffn_kernel(x_ref, of_ref, ob_ref, gt_ref, gn_ref, mod_ref, ng_ref, wo_ref, wgu_ref, wd_ref,
                        o_ref, act_ref, on_ref):
    for h in range(GLA_HEADS):
        hs = slice(h * GLA_DV_HEAD, (h + 1) * GLA_DV_HEAD)
        o = of_ref[0, :, hs] + ob_ref[0, :, hs]
        rinv = lax.rsqrt(jnp.mean(o * o, axis=-1, keepdims=True) + EPS)
        on_ref[:, hs] = (o * rinv * gn_ref[...] * gt_ref[0, :, hs]).astype(BF16)
    x = x_ref[0] + mod_ref[0, 5:6, :] * _dot(on_ref[...], wo_ref[...])
    hb, gate = _pre_mod(x, mod_ref, ng_ref, 2)
    o_ref[0] = x + 0.5 * gate * _swiglu(hb, wgu_ref, wd_ref, act_ref)


def _tok_spec(tm, width):
    return pl.BlockSpec((1, tm, width), lambda b, i: (b, i, 0))


def _mod_spec(mod):
    if mod.shape[0] == 1:
        return pl.BlockSpec((1, N_MOD, D_MODEL), lambda b, i: (0, 0, 0))
    return pl.BlockSpec((1, N_MOD, D_MODEL), lambda b, i: (b, 0, 0))


def _row_tile(t, pref=512):
    return pref if t % pref == 0 else t


def _ffn(x, mod, ng, wgu, wd, sub):
    bx, t, _ = x.shape
    tm = _row_tile(t)
    return pl.pallas_call(
        functools.partial(_ffn_kernel, sub=sub),
        grid=(bx, t // tm),
        in_specs=[_tok_spec(tm, D_MODEL), _mod_spec(mod), _resident((3, D_MODEL)),
                  _resident(wgu.shape), _resident(wd.shape)],
        out_specs=_tok_spec(tm, D_MODEL),
        out_shape=jax.ShapeDtypeStruct(x.shape, F32),
        scratch_shapes=[pltpu.VMEM((tm, D_FF), BF16)],
        compiler_params=_params("parallel", "parallel"),
        name="ffn",
    )(x, mod, ng, wgu, wd)


def _attn_out_ffn(x, oa, ob, mod, ng, wo, wgu, wd):
    bx, t, _ = x.shape
    tm = _row_tile(t)
    return pl.pallas_call(
        _attn_out_ffn_kernel,
        grid=(bx, t // tm),
        in_specs=[_tok_spec(tm, D_MODEL), _tok_spec(tm, oa.shape[-1]), _tok_spec(tm, ob.shape[-1]),
                  _mod_spec(mod), _resident((3, D_MODEL)), _resident(wo.shape),
                  _resident(wgu.shape), _resident(wd.shape)],
        out_specs=_tok_spec(tm, D_MODEL),
        out_shape=jax.ShapeDtypeStruct(x.shape, F32),
        scratch_shapes=[pltpu.VMEM((tm, D_FF), BF16)],
        compiler_params=_params("parallel", "parallel"),
        name="attn_out_ffn",
    )(x, oa, ob, mod, ng, wo, wgu, wd)


def _gla_out_ffn(x, of, ob, gt, gn, mod, ng, wo, wgu, wd):
    bx, t, _ = x.shape
    tm = _row_tile(t)
    return pl.pallas_call(
        _gla_out_ffn_kernel,
        grid=(bx, t // tm),
        in_specs=[_tok_spec(tm, D_MODEL), _tok_spec(tm, GLA_DV), _tok_spec(tm, GLA_DV), _tok_spec(tm, GLA_DV),
                  _resident(gn.shape), _mod_spec(mod), _resident((3, D_MODEL)), _resident(wo.shape),
                  _resident(wgu.shape), _resident(wd.shape)],
        out_specs=_tok_spec(tm, D_MODEL),
        out_shape=jax.ShapeDtypeStruct(x.shape, F32),
        scratch_shapes=[pltpu.VMEM((tm, D_FF), BF16), pltpu.VMEM((tm, GLA_DV), BF16)],
        compiler_params=_params("parallel", "parallel"),
        name="gla_out_ffn",
    )(x, of, ob, gt, gn, mod, ng, wo, wgu, wd)


def _lane(shape):
    return lax.broadcasted_iota(jnp.int32, shape, 1)


def _rope(x, cos, sin):
    return x * cos + pltpu.roll(x, LANES // 2, 1) * sin


def _ones_row(vt):
    row = lax.broadcasted_iota(jnp.int32, vt.shape, 0)
    return jnp.where((row & (LANES - 1)) == MLA_V_DIM, 1.0, vt).astype(BF16)


def _proj0_kernel(x_ref, mod_ref, ng_ref, win_ref, wuk_ref, wvt_ref, wuq_ref, wwvt_ref, seg_ref, gv_ref,
                  cm_ref, sm_ref, cw_ref, sw_ref, qa_ref, ka_ref, va_ref, qw_ref, kw_ref, vw_ref):
    hb, _ = _pre_mod(x_ref[0], mod_ref, ng_ref, 1)
    tm = hb.shape[0]
    z = _dot(hb, win_ref[...])
    zk = z[:, P0_CKV:P0_CQ]
    cm, sm, cw, sw = cm_ref[...], sm_ref[...], cw_ref[...], sw_ref[...]
    g_kva, g_qa, g_kn2 = gv_ref[0:1, :LANES], gv_ref[1:2, :], gv_ref[2:3, :]
    g_kr, g_q2, g_wq2 = gv_ref[3:4, :LANES], gv_ref[4:5, :], gv_ref[5:6, :]
    g_wk, n_q2 = gv_ref[6:7, :LANES], gv_ref[7:8, :]

    def seg_rinv(v, seg, inv_n):
        return lax.rsqrt(_dot((v * v).astype(BF16), seg) * inv_n + EPS)

    ckv = zk[:, :MLA_KV_RANK]
    ckv = (ckv * lax.rsqrt(jnp.mean(ckv * ckv, axis=-1, keepdims=True) + EPS) * g_kva).astype(BF16)
    kv = _dot(ckv, wuk_ref[...])
    va_ref[0, 0] = _ones_row(_dot_nt(wvt_ref[...], ckv))
    kr = zk[:, MLA_KV_RANK:]
    kr = kr * lax.rsqrt(jnp.sum(kr * kr, axis=-1, keepdims=True) * (1.0 / MLA_ROPE_DIM) + EPS) * g_kr
    kr = _rope(kr, cm, sm)
    kr2 = jnp.concatenate([kr, kr], axis=-1)
    for p in range(MLA_HEADS // 2):
        cols = slice(p * 2 * LANES, (p + 1) * 2 * LANES)
        k2 = kv[:, cols]
        ka_ref[0, :, cols] = (k2 * seg_rinv(k2, seg_ref[0], 1.0 / MLA_NOPE_DIM) * g_kn2 + kr2).astype(BF16)

    zw = z[:, P0_WQ:P0_WK]
    for p in range(WIN_GROUP // 2):
        w2 = zw[:, p * 2 * LANES:(p + 1) * 2 * LANES]
        w2 = w2 * seg_rinv(w2, seg_ref[1], 1.0 / WIN_HEAD_DIM) * g_wq2
        for h in range(2):
            lo = (2 * p + h) * LANES
            qw_ref[0, :, lo:lo + LANES] = _rope(w2[:, h * LANES:(h + 1) * LANES], cw, sw).astype(BF16)
    wk = z[:, P0_WK:P0_COLS]
    wk = wk * seg_rinv(wk, seg_ref[1, :LANES, :LANES], 1.0 / WIN_HEAD_DIM) * g_wk
    kw_ref[0] = _rope(wk, cw, sw).astype(BF16)
    vwt = _ones_row(_dot_nt(wwvt_ref[...], hb))
    for c in range(tm // WIN_BLOCK):
        vw_ref[0, c] = vwt[:, c * WIN_BLOCK:(c + 1) * WIN_BLOCK]

    cq = z[:, P0_CQ:P0_WQ]
    cq = (cq * lax.rsqrt(jnp.mean(cq * cq, axis=-1, keepdims=True) + EPS) * g_qa).astype(BF16)
    q = _dot(cq, wuq_ref[...])
    for p in range(MLA_HEADS // 2):
        q2 = q[:, p * 2 * LANES:(p + 1) * 2 * LANES]
        q2 = q2 * seg_rinv(q2, seg_ref[0], n_q2) * g_q2
        for h in range(2):
            lo = (2 * p + h) * LANES
            qa_ref[0, :, lo:lo + LANES] = _rope(q2[:, h * LANES:(h + 1) * LANES], cm, sm).astype(BF16)


def _proj0(x, mod, ng, win, wuk, wvt, wuq, wwvt, seg, gv, tabs):
    bx, t, _ = x.shape
    tm = _row_tile(t, MLA_TK)
    ntab = tabs[0].shape[0] // tm
    tab_spec = pl.BlockSpec((tm, LANES), lambda b, i: (i % ntab, 0))
    nwb = tm // WIN_BLOCK
    out_specs = [_tok_spec(tm, MLA_HEADS * LANES), _tok_spec(tm, MLA_HEADS * LANES),
                 pl.BlockSpec((1, 1, MLA_HEADS * LANES, tm), lambda b, i: (b, i, 0, 0)),
                 _tok_spec(tm, WIN_HEADS * WIN_HEAD_DIM), _tok_spec(tm, LANES),
                 pl.BlockSpec((1, nwb, 2 * LANES, WIN_BLOCK), lambda b, i: (b, i, 0, 0))]
    out_shape = [jax.ShapeDtypeStruct((bx, t, MLA_HEADS * LANES), BF16),
                 jax.ShapeDtypeStruct((bx, t, MLA_HEADS * LANES), BF16),
                 jax.ShapeDtypeStruct((bx, t // tm, MLA_HEADS * LANES, tm), BF16),
                 jax.ShapeDtypeStruct((bx, t, WIN_HEADS * WIN_HEAD_DIM), BF16),
                 jax.ShapeDtypeStruct((bx, t, LANES), BF16),
                 jax.ShapeDtypeStruct((bx, t // WIN_BLOCK, 2 * LANES, WIN_BLOCK), BF16)]
    return pl.pallas_call(
        _proj0_kernel,
        grid=(bx, t // tm),
        in_specs=[_tok_spec(tm, D_MODEL), _mod_spec(mod), _resident((3, D_MODEL)), _resident(win.shape),
                  _resident(wuk.shape), _resident(wvt.shape), _resident(wuq.shape), _resident(wwvt.shape),
                  _resident(seg.shape), _resident(gv.shape)] + [tab_spec] * 4,
        out_specs=out_specs,
        out_shape=out_shape,
        compiler_params=_params("parallel", "parallel"),
        name="proj0",
    )(x, mod, ng, win, wuk, wvt, wuq, wwvt, seg, gv, *tabs)


def _mla_attn_kernel(*refs, has_latent):
    if has_latent:
        q_ref, kc_ref, vc_ref, k_ref, v_ref, o_ref, m_ref, acc_ref, s_ref, mc_ref = refs
    else:
        q_ref, kc_ref, vc_ref, o_ref, m_ref, acc_ref = refs

    def head(hh):
        return slice(hh * LANES, (hh + 1) * LANES)

    def scores(kblk, hh):
        return _dot_nt(kblk, q_ref[0, :, head(hh)])

    def qk(j, slot, hh):
        st = scores(k_ref[0, pl.ds(pl.multiple_of(j * MLA_TK, MLA_TK), MLA_TK), head(hh)], hh)
        s_ref[slot, hh] = st
        mc_ref[slot, hh] = jnp.max(st, axis=0, keepdims=True)

    def process(j, slot, hh):
        m_prev = m_ref[hh]
        m_new = jnp.maximum(m_prev, mc_ref[slot, hh])
        p = jnp.exp2(s_ref[slot, hh] - m_new).astype(BF16)
        pv = _dot(v_ref[0, j, head(hh), :], p)
        acc_ref[hh] = acc_ref[hh] * jnp.exp2(m_prev - m_new) + pv
        m_ref[hh] = m_new

    def ctx_softmax(st, hh):
        m_new = jnp.max(st, axis=0, keepdims=True)
        p = jnp.exp2(st - m_new).astype(BF16)
        acc_ref[hh] = _dot(vc_ref[0, head(hh)], p)
        m_ref[hh] = m_new

    def pair(nxt, cur, slot):
        for hh in range(2):
            qk(nxt, 1 - slot, hh)
            process(cur, slot, hh)

    sts = [scores(kc_ref[0, :, head(hh)], hh) for hh in range(2)]
    if has_latent:
        n = v_ref.shape[1]
        for hh in range(2):
            qk(0, 0, hh)
            ctx_softmax(sts[hh], hh)

        def body(i, carry):
            pair(2 * i + 1, 2 * i, 0)
            pair(2 * i + 2, 2 * i + 1, 1)
            return carry
        lax.fori_loop(0, n // 2 - 1, body, 0)
        pair(n - 1, n - 2, 0)
        for hh in range(2):
            process(n - 1, 1, hh)
    else:
        for hh in range(2):
            ctx_softmax(sts[hh], hh)
    a0, a1 = acc_ref[0], acc_ref[1]
    ot = jnp.concatenate([a0[:MLA_V_DIM] / a0[MLA_V_DIM:MLA_V_DIM + 1],
                          a1[:MLA_V_DIM] / a1[MLA_V_DIM:MLA_V_DIM + 1]], axis=0)
    o_ref[0] = ot.T.astype(BF16)


def _mla_attn(q, kc, vtc, k=None, vt=None):
    bsz, t, _ = q.shape
    nc = kc.shape[1]
    has_latent = k is not None
    tq = _row_tile(t, MLA_TQ)
    in_specs = [pl.BlockSpec((1, tq, 2 * LANES), lambda b, h, i: (b, i, h)),
                pl.BlockSpec((1, nc, 2 * LANES), lambda b, h, i: (b, 0, h)),
                pl.BlockSpec((1, 2 * LANES, nc), lambda b, h, i: (b, h, 0))]
    args = [q, kc, vtc]
    scratch = [pltpu.VMEM((2, 1, tq), F32), pltpu.VMEM((2, LANES, tq), F32)]
    if has_latent:
        n = k.shape[1]
        nt, _, tk = vt.shape[1:]
        assert tk == MLA_TK and nt * tk == n and nt % 2 == 0, (n, nt, tk)
        in_specs += [pl.BlockSpec((1, n, 2 * LANES), lambda b, h, i: (b, 0, h)),
                     pl.BlockSpec((1, nt, 2 * LANES, tk), lambda b, h, i: (b, 0, h, 0))]
        args += [k, vt]
        scratch += [pltpu.VMEM((2, 2, tk, tq), F32), pltpu.VMEM((2, 2, 1, tq), F32)]
    return pl.pallas_call(
        functools.partial(_mla_attn_kernel, has_latent=has_latent),
        grid=(bsz, MLA_HEADS // 2, t // tq),
        in_specs=in_specs,
        out_specs=pl.BlockSpec((1, tq, LANES), lambda b, h, i: (b, i, h)),
        out_shape=jax.ShapeDtypeStruct((bsz, t, MLA_HEADS * MLA_V_DIM), BF16),
        scratch_shapes=scratch,
        compiler_params=_params("parallel", "parallel", "arbitrary"),
        name="mla_attn" if has_latent else "mla_attn_ctx",
    )(*args)


def _win_attn_kernel(*refs, has_window, nb):
    if has_window:
        sink_ref, q_ref, kc_ref, vc_ref, k_ref, vt_ref, bias_ref, o_ref, s_ref, mc_ref = refs
    else:
        sink_ref, q_ref, kc_ref, vc_ref, o_ref, s_ref, mc_ref = refs
    qb = q_ref.shape[1] // WIN_BLOCK
    nct = vc_ref.shape[1]
    i = pl.program_id(1)
    units = [(blk, n) for blk in range(qb) for n in range(WIN_KV_HEADS)]
    cache = {}

    def block_operands(blk):
        if blk in cache:
            return cache[blk]
        vts = [vc_ref[0, t] for t in range(nct)]
        bias = None
        if has_window:
            g = i * qb + blk
            start = jnp.clip(g - 1, 0, nb - 3)
            kwin = k_ref[0, pl.ds(pl.multiple_of(start * WIN_BLOCK, WIN_BLOCK), 3 * WIN_BLOCK), :]
            kall = jnp.concatenate([kc_ref[0], kwin], axis=0)
            vts += [vt_ref[0, start + t] for t in range(3)]
            bias = jnp.concatenate([bias_ref[g - start]] * WIN_GROUP, axis=1)
        else:
            kall = kc_ref[0]
        rows = slice(blk * WIN_BLOCK, (blk + 1) * WIN_BLOCK)
        qs = jnp.concatenate([q_ref[0, rows, j * LANES:(j + 1) * LANES] for j in range(WIN_GROUP)], axis=0)
        cache[blk] = (kall, jnp.concatenate(vts, axis=1), qs, bias)
        return cache[blk]

    def qk(u, slot):
        blk, n = units[u]
        kall, _, qs, bias = block_operands(blk)
        head0 = (_lane(qs.shape) & 32) == 0
        zero = jnp.zeros_like(qs)
        st = _dot_nt(kall, jnp.where(head0, qs, zero) if n == 0 else jnp.where(head0, zero, qs))
        if bias is not None:
            st = st + bias
        s_ref[slot] = st
        mc_ref[slot] = jnp.max(st, axis=0, keepdims=True)

    outs = {}

    def process(u, slot):
        blk, n = units[u]
        vall = block_operands(blk)[1]
        sink = sink_ref[n]
        m = jnp.maximum(mc_ref[slot], sink)
        p = jnp.exp2(s_ref[slot] - m).astype(BF16)
        pv = _dot(vall[n * LANES:(n + 1) * LANES], p)
        denom = pv[WIN_HEAD_DIM:WIN_HEAD_DIM + 1] + jnp.exp2(sink - m)
        outs[(blk, n)] = pv[:WIN_HEAD_DIM] / denom
        if n == WIN_KV_HEADS - 1:
            rows = slice(blk * WIN_BLOCK, (blk + 1) * WIN_BLOCK)
            for j in range(WIN_GROUP):
                cols = slice(j * WIN_BLOCK, (j + 1) * WIN_BLOCK)
                both = jnp.concatenate([outs[(blk, 0)][:, cols], outs[(blk, 1)][:, cols]], axis=0)
                o_ref[0, rows, j * LANES:(j + 1) * LANES] = both.T.astype(BF16)

    qk(0, 0)
    for u in range(len(units)):
        if u + 1 < len(units):
            qk(u + 1, (u + 1) % 2)
        process(u, u % 2)


def _band_bias(nc):
    krow = np.arange(nc + 3 * WIN_BLOCK)[:, None]
    qcol = np.arange(WIN_BLOCK)[None, :]
    out = []
    for off in range(3):
        rel = (krow - nc) - (qcol + off * WIN_BLOCK)
        out.append((krow < nc) | (np.abs(rel) <= WINDOW))
    return jnp.asarray(np.where(np.stack(out), 0.0, NEG_INF), F32)


def _win_attn(sinkrow, q, kc, vtc, k=None, vt=None):
    bsz, t, _ = q.shape
    nc = kc.shape[1]
    has_window = k is not None
    nb = t // WIN_BLOCK
    tq = _row_tile(t, WIN_TQ)
    kvw = WIN_KV_HEADS * WIN_HEAD_DIM
    nk = nc + (3 * WIN_BLOCK if has_window else 0)
    in_specs = [pl.BlockSpec((WIN_KV_HEADS, 1, WIN_GROUP * WIN_BLOCK), lambda b, i: (0, 0, 0)),
                pl.BlockSpec((1, tq, WIN_HEADS * WIN_HEAD_DIM), lambda b, i: (b, i, 0)),
                pl.BlockSpec((1, nc, kvw), lambda b, i: (b, 0, 0)),
                pl.BlockSpec((1, nc // WIN_BLOCK, 2 * LANES, WIN_BLOCK), lambda b, i: (b, 0, 0, 0))]
    args = [sinkrow, q, kc, vtc]
    if has_window:
        assert nb >= 3, nb
        in_specs += [pl.BlockSpec((1, t, kvw), lambda b, i: (b, 0, 0)),
                     pl.BlockSpec((1, nb, 2 * LANES, WIN_BLOCK), lambda b, i: (b, 0, 0, 0)),
                     pl.BlockSpec((3, nk, WIN_BLOCK), lambda b, i: (0, 0, 0))]
        args += [k, vt, _band_bias(nc)]
    return pl.pallas_call(
        functools.partial(_win_attn_kernel, has_window=has_window, nb=nb),
        grid=(bsz, t // tq),
        in_specs=in_specs,
        out_specs=pl.BlockSpec((1, tq, WIN_HEADS * WIN_HEAD_DIM), lambda b, i: (b, i, 0)),
        out_shape=jax.ShapeDtypeStruct((bsz, t, WIN_HEADS * WIN_HEAD_DIM), BF16),
        scratch_shapes=[pltpu.VMEM((2, nk, WIN_GROUP * WIN_BLOCK), F32),
                        pltpu.VMEM((2, 1, WIN_GROUP * WIN_BLOCK), F32)],
        compiler_params=_params("parallel", "parallel"),
        name="win_attn" if has_window else "win_attn_ctx",
    )(*args)


def _log_sigmoid(x):
    return jnp.minimum(x, 0.0) - jnp.log(1.0 + jnp.exp(-jnp.abs(x)))


def _proj1_kernel(x_ref, mod_ref, ng_ref, win_ref, wgk_ref, bgk_ref, k_ref, v_ref, q_ref, gt_ref, gf_ref, gb_ref):
    hb, _ = _pre_mod(x_ref[0], mod_ref, ng_ref, 1)
    ql = _dot(hb, win_ref[:, P1_Q:P1_LOW + LANES])
    q_ref[0] = ql[:, :GLA_DK] * (GLA_DK_HEAD ** -0.5)
    low = ql[:, GLA_DK:].astype(BF16)
    k_ref[0] = _dot(hb, win_ref[:, P1_K:P1_K + GLA_DK])
    pre_f = _dot(low, wgk_ref[:, :GLA_DK]) + bgk_ref[:, :GLA_DK]
    gf_ref[0] = _log_sigmoid(pre_f) * (1.0 / GLA_GATE_NORM)
    v_ref[0] = _dot(hb, win_ref[:, P1_V:P1_V + GLA_DV]).astype(BF16)
    pre_b = _dot(low, wgk_ref[:, GLA_DK:]) + bgk_ref[:, GLA_DK:]
    gb_ref[0] = _log_sigmoid(pre_b) * (1.0 / GLA_GATE_NORM)
    gt_ref[0] = _silu(_dot(hb, win_ref[:, P1_G:P1_G + GLA_DV]))


def _proj1(x, mod, ng, win, wgk, bgk):
    bx, t, _ = x.shape
    tm = _row_tile(t)
    outs = ((GLA_DK, F32), (GLA_DV, BF16), (GLA_DK, F32), (GLA_DV, F32), (GLA_DK, F32), (GLA_DK, F32))
    return pl.pallas_call(
        _proj1_kernel,
        grid=(bx, t // tm),
        in_specs=[_tok_spec(tm, D_MODEL), _mod_spec(mod), _resident((3, D_MODEL)), _resident(win.shape),
                  _resident(wgk.shape), _resident(bgk.shape)],
        out_specs=[_tok_spec(tm, w) for w, _ in outs],
        out_shape=[jax.ShapeDtypeStruct((bx, t, w), d) for w, d in outs],
        compiler_params=_params("parallel", "parallel"),
        name="proj1",
    )(x, mod, ng, win, wgk, bgk)


def _gla_block(q_ref, k_ref, v_ref, g_ref, o_ref, st_ref, bi, cum, keep, backward):
    t = GLA_BLOCK
    g = g_ref[bi]
    g_hi = g.astype(BF16)
    g_lo = (g - g_hi.astype(F32)).astype(BF16)
    both = _dot(cum, g_hi) + _dot(cum, g_lo)
    bb, tot = both[:t], both[t:]
    k = k_ref[bi]
    q_dec = (q_ref[bi] * jnp.exp(bb)).astype(BF16)
    k_inv = (k * jnp.exp(-bb)).astype(BF16)
    k_end = (k * jnp.exp(tot - bb)).astype(BF16)
    decay = jnp.exp(tot)
    n_sub = t // GLA_CHUNK
    order = range(n_sub - 1, -1, -1) if backward else range(n_sub)
    for h in range(GLA_HEADS):
        ks = slice(h * GLA_DK_HEAD, (h + 1) * GLA_DK_HEAD)
        vs = slice(h * GLA_DV_HEAD, (h + 1) * GLA_DV_HEAD)
        vh = v_ref[bi, :, vs]
        a = jnp.where(keep, _dot_nt(q_dec[:, ks], k_inv[:, ks]), 0.0).astype(BF16)
        o_intra = _dot(a, vh)
        st = st_ref[bi, h]
        for c in order:
            rows = slice(c * GLA_CHUNK, (c + 1) * GLA_CHUNK)
            o_ref[bi, rows, vs] = o_intra[rows] + _dot_nt(q_dec[rows, ks], st.astype(BF16))
            st = st * decay[c * GLA_CHUNK:c * GLA_CHUNK + 1, ks] + _dot_tn(vh[rows], k_end[rows, ks])
        st_ref[bi, h] = st


def _gla_scan_kernel(qf_ref, kf_ref, vf_ref, gf_ref, qb_ref, kb_ref, vb_ref, gb_ref, s0f_ref, s0b_ref, cum_ref,
                     of_ref, ob_ref, sf_ref, sb_ref, stf_ref, stb_ref):
    i = pl.program_id(1)

    @pl.when(i == 0)
    def _():
        stf_ref[...] = s0f_ref[...]
        stb_ref[...] = s0b_ref[...]

    row = lax.broadcasted_iota(jnp.int32, (GLA_BLOCK, GLA_BLOCK), 0)
    col = lax.broadcasted_iota(jnp.int32, (GLA_BLOCK, GLA_BLOCK), 1)
    same = (row // GLA_CHUNK) == (col // GLA_CHUNK)
    for bi in range(qf_ref.shape[0]):
        _gla_block(qf_ref, kf_ref, vf_ref, gf_ref, of_ref, stf_ref, bi, cum_ref[0], same & (col <= row), False)
        _gla_block(qb_ref, kb_ref, vb_ref, gb_ref, ob_ref, stb_ref, bi, cum_ref[1], same & (col >= row), True)

    @pl.when(i == pl.num_programs(1) - 1)
    def _():
        sf_ref[...] = stf_ref[...]
        sb_ref[...] = stb_ref[...]


def _cum_matrices():
    r = np.arange(GLA_BLOCK)[:, None]
    c = np.arange(GLA_BLOCK)[None, :]
    same = (r // GLA_CHUNK) == (c // GLA_CHUNK)
    fwd = np.concatenate([same & (c <= r), same], axis=0)
    bwd = np.concatenate([same & (c >= r), same], axis=0)
    return jnp.asarray(np.stack([fwd, bwd]), BF16)


def _gla_scan(q, k, v, gf, gb, s0f, s0b):
    bsz, t, _ = q.shape
    tb = GLA_BLOCK
    assert t % tb == 0, t
    nblk = t // tb
    nbat = 2 if bsz % 2 == 0 else 1
    fwd = lambda w: pl.BlockSpec((nbat, tb, w), lambda b, i: (b, i, 0))
    bwd = lambda w: pl.BlockSpec((nbat, tb, w), lambda b, i: (b, nblk - 1 - i, 0))
    st_spec = pl.BlockSpec((nbat, GLA_HEADS, GLA_DV_HEAD, GLA_DK_HEAD), lambda b, i: (b, 0, 0, 0))
    st_shape = jax.ShapeDtypeStruct((bsz, GLA_HEADS, GLA_DV_HEAD, GLA_DK_HEAD), F32)
    cum = _cum_matrices()
    return pl.pallas_call(
        _gla_scan_kernel,
        grid=(bsz // nbat, nblk),
        in_specs=[fwd(GLA_DK), fwd(GLA_DK), fwd(GLA_DV), fwd(GLA_DK),
                  bwd(GLA_DK), bwd(GLA_DK), bwd(GLA_DV), bwd(GLA_DK), st_spec, st_spec, _resident(cum.shape)],
        out_specs=[fwd(GLA_DV), bwd(GLA_DV), st_spec, st_spec],
        out_shape=[jax.ShapeDtypeStruct((bsz, t, GLA_DV), F32)] * 2 + [st_shape] * 2,
        scratch_shapes=[pltpu.VMEM((nbat, GLA_HEADS, GLA_DV_HEAD, GLA_DK_HEAD), F32)] * 2,
        compiler_params=_params("parallel", "arbitrary"),
        name="gla_scan",
    )(q, k, v, gf, q, k, v, gb, s0f, s0b, cum)


def _deinterleave(n):
    return np.concatenate([np.arange(0, n, 2), np.arange(1, n, 2)])


def _gather_cols(w, idx):
    idx = np.asarray(idx)
    cols = jnp.take(w, jnp.asarray(np.maximum(idx, 0)), axis=-1)
    return jnp.where(jnp.asarray(idx >= 0), cols, 0.0)


def _mla_slot(nope, rope):
    pad = np.full((16,), -1, np.int64)
    return np.concatenate([nope[:32], rope[:16], pad, nope[32:], rope[16:], pad])


def _win_slot(a, b):
    return np.concatenate([a[:32], b[:32], a[32:], b[32:]])


def _take(vec, idx):
    return jnp.where(jnp.asarray(idx >= 0), vec[jnp.asarray(np.maximum(idx, 0))], 0.0)


def _layer0_layout(p):
    ckv0, kr0 = 0, MLA_KV_RANK
    wk0 = kr0 + MLA_ROPE_DIM
    wv0 = wk0 + WIN_KV_HEADS * WIN_HEAD_DIM
    cq0 = wv0 + WIN_KV_HEADS * WIN_HEAD_DIM
    wq0 = cq0 + MLA_Q_RANK
    de64, de32 = _deinterleave(WIN_HEAD_DIM), _deinterleave(MLA_ROPE_DIM)
    none64, none32 = np.full((64,), -1, np.int64), np.full((32,), -1, np.int64)

    idx = np.full((P0_COLS,), -1, np.int64)
    idx[P0_CKV:P0_CKV + MLA_KV_RANK] = ckv0 + np.arange(MLA_KV_RANK)
    idx[P0_WK:P0_WK + LANES] = _win_slot(wk0 + de64, wk0 + 64 + de64)
    idx[P0_CQ:P0_CQ + MLA_Q_RANK] = cq0 + np.arange(MLA_Q_RANK)
    for j in range(WIN_GROUP):
        idx[P0_WQ + j * LANES:P0_WQ + (j + 1) * LANES] = _win_slot(wq0 + j * 64 + de64,
                                                                   wq0 + (WIN_GROUP + j) * 64 + de64)
    idx[P0_KR:P0_KR + LANES] = _mla_slot(none64, kr0 + de32)
    w_in = _gather_cols(p["w_in"], idx).astype(BF16)
    vidx = np.full((WIN_KV_HEADS * LANES,), -1, np.int64)
    for n in range(WIN_KV_HEADS):
        vidx[n * LANES:n * LANES + 64] = wv0 + n * 64 + np.arange(64)
    w_wvt = _gather_cols(p["w_in"], vidx).T.astype(BF16)

    per = MLA_NOPE_DIM + MLA_V_DIM
    kidx, qidx, vidx = [], [], []
    for h in range(MLA_HEADS):
        kidx.append(_mla_slot(h * per + np.arange(64), none32))
        qidx.append(_mla_slot(h * MLA_QK_DIM + np.arange(64), h * MLA_QK_DIM + MLA_NOPE_DIM + de32))
        vidx.append(np.concatenate([h * per + MLA_NOPE_DIM + np.arange(64), none64]))
    w_uk = _gather_cols(p["mla_w_ukv"], np.concatenate(kidx)).astype(BF16)
    w_vt = _gather_cols(p["mla_w_ukv"], np.concatenate(vidx)).T.astype(BF16)
    w_uq = _gather_cols(p["mla_w_uq"], np.concatenate(qidx)).astype(BF16)

    lane = np.arange(LANES)
    nope, rot = (lane & 32) == 0, (lane & 48) == 32
    seg_a = (nope[:, None] & nope[None, :]) | (rot[:, None] & rot[None, :])
    seg_b = nope[:, None] == nope[None, :]
    two = lambda m: np.kron(np.eye(2), m.astype(np.float32))
    seg = jnp.asarray(np.stack([two(seg_a), two(seg_b)]), BF16)

    a64, a32 = np.arange(64), np.arange(32)
    twice = lambda v: jnp.concatenate([v, v])
    wide = lambda v: jnp.concatenate([v, jnp.zeros((LANES,), F32)])
    g_wq = _take(p["win_g_q"], _win_slot(de64, de64)) * (WIN_SCALE * LOG2E)
    g_wk = _take(p["win_g_k"], _win_slot(de64, de64))
    g_qslot = jnp.concatenate([p["mla_g_qn"], p["mla_g_qr"]])
    inv_n = jnp.concatenate([jnp.full((64,), 1.0 / MLA_NOPE_DIM, F32), jnp.full((32,), 1.0 / MLA_ROPE_DIM, F32)])
    gv = jnp.stack([
        wide(p["mla_g_kva"]),
        p["mla_g_qa"],
        twice(_take(p["mla_g_kn"], _mla_slot(a64, none32))),
        wide(_take(p["mla_g_kr"], _mla_slot(none64, de32))),
        twice(_take(g_qslot, _mla_slot(a64, 64 + de32))) * (MLA_SCALE * LOG2E),
        twice(g_wq),
        wide(g_wk),
        twice(_take(inv_n, _mla_slot(a64, 64 + a32))),
    ])

    out_rows = np.arange(D_MODEL)
    base = MLA_HEADS * MLA_V_DIM
    for j in range(WIN_GROUP):
        for n in range(WIN_KV_HEADS):
            lo = base + j * LANES + n * 64
            out_rows[lo:lo + 64] = base + (n * WIN_GROUP + j) * 64 + np.arange(64)
    w_out = p["w_out"][jnp.asarray(out_rows)].astype(BF16)
    return (w_in, w_uk, w_vt, w_uq, w_wvt, seg, gv), w_out


def _layer1_layout(p):
    k0, v0 = 0, GLA_DK
    lf0 = v0 + GLA_DV
    lb0 = lf0 + GLA_LOWRANK
    q0 = lb0 + GLA_LOWRANK
    g0 = q0 + GLA_DK
    idx = np.full((P1_COLS,), -1, np.int64)
    idx[P1_K:P1_K + GLA_DK] = k0 + np.arange(GLA_DK)
    idx[P1_V:P1_V + GLA_DV] = v0 + np.arange(GLA_DV)
    idx[P1_Q:P1_Q + GLA_DK] = q0 + np.arange(GLA_DK)
    idx[P1_G:P1_G + GLA_DV] = g0 + np.arange(GLA_DV)
    idx[P1_LOW:P1_LOW + GLA_LOWRANK] = lf0 + np.arange(GLA_LOWRANK)
    idx[P1_LOW + GLA_LOWRANK:P1_LOW + 2 * GLA_LOWRANK] = lb0 + np.arange(GLA_LOWRANK)
    w_in = _gather_cols(p["w_in"], idx).astype(BF16)
    wgk = jnp.zeros((LANES, 2 * GLA_DK), F32)
    wgk = wgk.at[:GLA_LOWRANK, :GLA_DK].set(p["w_gk_f"])
    wgk = wgk.at[GLA_LOWRANK:2 * GLA_LOWRANK, GLA_DK:].set(p["w_gk_b"])
    bgk = jnp.concatenate([p["b_gk_f"], p["b_gk_b"]]).reshape(1, 2 * GLA_DK)
    return w_in, wgk.astype(BF16), bgk


def _rope_tables(seq):
    t = jnp.arange(seq)
    row = (t // GRID_W).astype(F32)
    col = (t % GRID_W).astype(F32)

    def angles(rot_dim):
        n_freq = rot_dim // 4
        inv = ROPE_BASE ** (-jnp.arange(n_freq, dtype=F32) / n_freq)
        return jnp.concatenate([row[:, None] * inv, col[:, None] * inv], axis=-1)

    am, aw = angles(MLA_ROPE_DIM), angles(WIN_HEAD_DIM)
    one32, one16 = jnp.ones((seq, 32), F32), jnp.ones((seq, 16), F32)
    cm = jnp.concatenate([one32, jnp.cos(am), one16] * 2, axis=-1)
    sm = jnp.concatenate([0 * one32, -jnp.sin(am), 0 * one16, 0 * one32, jnp.sin(am), 0 * one16], axis=-1)
    cw = jnp.concatenate([jnp.cos(aw)] * 4, axis=-1)
    sw = jnp.concatenate([-jnp.sin(aw), -jnp.sin(aw), jnp.sin(aw), jnp.sin(aw)], axis=-1)
    return cm, sm, cw, sw


def _identity_tables(rows):
    one, zero = jnp.ones((rows, LANES), F32), jnp.zeros((rows, LANES), F32)
    return one, zero, one, zero


def kernel(x, c, ctx, c_ctx, l0_norm_g, l0_w_mod, l0_b_mod, l0_ffn1_w_gu, l0_ffn1_w_down, l0_ffn2_w_gu, l0_ffn2_w_down, l0_w_in, l0_mla_g_qa, l0_mla_g_kva, l0_mla_w_uq, l0_mla_w_ukv, l0_mla_g_qn, l0_mla_g_qr, l0_mla_g_kn, l0_mla_g_kr, l0_win_g_q, l0_win_g_k, l0_win_sink, l0_w_out, l1_norm_g, l1_w_mod, l1_b_mod, l1_ffn1_w_gu, l1_ffn1_w_down, l1_ffn2_w_gu, l1_ffn2_w_down, l1_w_in, l1_w_gk_f, l1_b_gk_f, l1_w_gk_b, l1_b_gk_b, l1_g_norm, l1_w_out):
    bsz, seq, _ = x.shape
    n_ctx = ctx.shape[1]
    out_dtype = x.dtype
    x = x.astype(F32)
    xc = ctx.astype(F32)

    cc = jnp.zeros((16, D_MODEL), F32).at[:bsz].set(c).at[bsz].set(c_ctx)

    def mods(w_mod, b_mod):
        tab = _mod_table(cc, w_mod, b_mod).reshape(16, N_MOD, D_MODEL)
        return tab[:bsz], tab[bsz:bsz + 1]

    p0 = dict(w_in=l0_w_in, mla_g_qa=l0_mla_g_qa, mla_g_kva=l0_mla_g_kva, mla_w_uq=l0_mla_w_uq,
              mla_w_ukv=l0_mla_w_ukv, mla_g_qn=l0_mla_g_qn, mla_g_qr=l0_mla_g_qr, mla_g_kn=l0_mla_g_kn,
              mla_g_kr=l0_mla_g_kr, win_g_q=l0_win_g_q, win_g_k=l0_win_g_k, w_out=l0_w_out)
    proj_w, w_out0 = _layer0_layout(p0)
    mod, mod_c = mods(l0_w_mod, l0_b_mod)
    wgu1, wd1 = l0_ffn1_w_gu.astype(BF16), l0_ffn1_w_down.astype(BF16)
    wgu2, wd2 = l0_ffn2_w_gu.astype(BF16), l0_ffn2_w_down.astype(BF16)
    sinkrow = jnp.repeat(l0_win_sink.astype(F32) * LOG2E, WIN_BLOCK, axis=1).reshape(WIN_KV_HEADS, 1, -1)

    x = _ffn(x, mod, l0_norm_g, wgu1, wd1, 0)
    xc = _ffn(xc, mod_c, l0_norm_g, wgu1, wd1, 0)
    qa, ka, vat, qw, kw, vwt = _proj0(x, mod, l0_norm_g, *proj_w, _rope_tables(seq))
    qa_c, ka_c, vat_c, qw_c, kw_c, vwt_c = _proj0(xc, mod_c, l0_norm_g, *proj_w, _identity_tables(n_ctx))
    vat_c = vat_c.reshape(bsz, MLA_HEADS * LANES, n_ctx)
    oa = _mla_attn(qa, ka_c, vat_c, ka, vat)
    ow = _win_attn(sinkrow, qw, kw_c, vwt_c, kw, vwt)
    oa_c = _mla_attn(qa_c, ka_c, vat_c)
    ow_c = _win_attn(sinkrow, qw_c, kw_c, vwt_c)
    x = _attn_out_ffn(x, oa, ow, mod, l0_norm_g, w_out0, wgu2, wd2)
    xc = _attn_out_ffn(xc, oa_c, ow_c, mod_c, l0_norm_g, w_out0, wgu2, wd2)

    w_in1, wgk, bgk = _layer1_layout(dict(w_in=l1_w_in, w_gk_f=l1_w_gk_f, b_gk_f=l1_b_gk_f,
                                          w_gk_b=l1_w_gk_b, b_gk_b=l1_b_gk_b))
    mod, mod_c = mods(l1_w_mod, l1_b_mod)
    wgu1, wd1 = l1_ffn1_w_gu.astype(BF16), l1_ffn1_w_down.astype(BF16)
    wgu2, wd2 = l1_ffn2_w_gu.astype(BF16), l1_ffn2_w_down.astype(BF16)

    x = _ffn(x, mod, l1_norm_g, wgu1, wd1, 0)
    xc = _ffn(xc, mod_c, l1_norm_g, wgu1, wd1, 0)
    k1, v1, q1, gt1, gf1, gb1 = _proj1(x, mod, l1_norm_g, w_in1, wgk, bgk)
    k1c, v1c, q1c, _, gf1c, gb1c = _proj1(xc, mod_c, l1_norm_g, w_in1, wgk, bgk)
    zero_state = jnp.zeros((bsz, GLA_HEADS, GLA_DV_HEAD, GLA_DK_HEAD), F32)
    _, _, s_f, s_b = _gla_scan(q1c, k1c, v1c, gf1c, gb1c, zero_state, zero_state)
    o_f, o_b, _, _ = _gla_scan(q1, k1, v1, gf1, gb1, s_f, s_b)
    x = _gla_out_ffn(x, o_f, o_b, gt1, l1_g_norm.reshape(1, GLA_DV_HEAD), mod, l1_norm_g,
                     l1_w_out.astype(BF16), wgu2, wd2)
    return x.astype(out_dtype)
```

```python
import functools

import numpy as np
import jax
import jax.numpy as jnp
from jax import lax
from jax.experimental import pallas as pl
from jax.experimental.pallas import tpu as pltpu

F32 = jnp.float32
BF16 = jnp.bfloat16

D_MODEL = 1024
GRID_W = 64
N_MOD = 9
EPS = 1e-6
ROPE_BASE = 10000.0
NEG_INF = -1e30
D_FF = 2816
LOG2E = 1.4426950408889634

MLA_HEADS = 8
MLA_Q_RANK = 256
MLA_KV_RANK = 128
MLA_NOPE_DIM = 64
MLA_ROPE_DIM = 32
MLA_V_DIM = 64
MLA_QK_DIM = MLA_NOPE_DIM + MLA_ROPE_DIM
MLA_SCALE = MLA_QK_DIM ** -0.5

WIN_HEADS = 8
WIN_KV_HEADS = 2
WIN_GROUP = WIN_HEADS // WIN_KV_HEADS
WIN_HEAD_DIM = 64
WINDOW = 128
WIN_BLOCK = 128
WIN_SCALE = WIN_HEAD_DIM ** -0.5

GLA_HEADS = 4
GLA_DK = D_MODEL // 2
GLA_DV = D_MODEL
GLA_DK_HEAD = GLA_DK // GLA_HEADS
GLA_DV_HEAD = GLA_DV // GLA_HEADS
GLA_LOWRANK = 16
GLA_GATE_NORM = 16.0
GLA_CHUNK = 64
GLA_BLOCK = 256

LANES = 128
FF_CHUNK = 256
FFN_TM = 512
MLA_TQ = 1024
MLA_TK = 512
WIN_TQ = 1024
VMEM_LIMIT = 58 * 1024 * 1024

P0_CKV, P0_KR, P0_CQ, P0_WQ, P0_WK, P0_COLS = 0, 128, 256, 512, 1024, 1152
P1_K, P1_V, P1_Q, P1_LOW, P1_G, P1_COLS = 0, 512, 1536, 2048, 2176, 3200


def _dot(a, b):
    return jnp.dot(a, b, preferred_element_type=F32)


def _dot_nt(a, b):
    return lax.dot_general(a, b, (((1,), (1,)), ((), ())), preferred_element_type=F32)


def _dot_tn(a, b):
    return lax.dot_general(a, b, (((0,), (0,)), ((), ())), preferred_element_type=F32)


def _params(*sem):
    return pltpu.CompilerParams(dimension_semantics=sem, vmem_limit_bytes=VMEM_LIMIT)


def _resident(shape):
    nd = len(shape)
    return pl.BlockSpec(shape, lambda *_: (0,) * nd, pipeline_mode=pl.Buffered(1))


def _silu(x):
    return x / (1.0 + jnp.exp(-x))


def _pre_mod(x, mod_ref, ng_ref, sub):
    shift = mod_ref[0, 3 * sub:3 * sub + 1, :]
    scale = mod_ref[0, 3 * sub + 1:3 * sub + 2, :]
    gate = mod_ref[0, 3 * sub + 2:3 * sub + 3, :]
    rinv = lax.rsqrt(jnp.mean(x * x, axis=-1, keepdims=True) + EPS)
    h = x * rinv * ng_ref[sub:sub + 1, :] * (1.0 + scale) + shift
    return h.astype(BF16), gate


def _swiglu(hb, wgu_ref, wd_ref, act_ref):
    for c in range(D_FF // FF_CHUNK):
        lo = c * FF_CHUNK
        g = _dot(hb, wgu_ref[:, lo:lo + FF_CHUNK])
        u = _dot(hb, wgu_ref[:, D_FF + lo:D_FF + lo + FF_CHUNK])
        act_ref[:, lo:lo + FF_CHUNK] = (_silu(g) * u).astype(BF16)
    return _dot(act_ref[...], wd_ref[...])


def _mod_kernel(c_ref, w_ref, b_ref, o_ref):
    a = _silu(c_ref[...]).astype(BF16)
    o_ref[...] = _dot(a, w_ref[...].astype(BF16)) + b_ref[...]


def _mod_table(cc, w_mod, b_mod):
    rows = cc.shape[0]
    n = w_mod.shape[1]
    bn = 1152
    return pl.pallas_call(
        _mod_kernel,
        grid=(n // bn,),
        in_specs=[pl.BlockSpec((rows, D_MODEL), lambda j: (0, 0)),
                  pl.BlockSpec((D_MODEL, bn), lambda j: (0, j)),
                  pl.BlockSpec((1, bn), lambda j: (0, j))],
        out_specs=pl.BlockSpec((rows, bn), lambda j: (0, j)),
        out_shape=jax.ShapeDtypeStruct((rows, n), F32),
        compiler_params=_params("arbitrary"),
        name="mod_table",
    )(cc, w_mod, b_mod.reshape(1, n))


def _ffn_kernel(x_ref, mod_ref, ng_ref, wgu_ref, wd_ref, o_ref, act_ref, *, sub):
    x = x_ref[0]
    hb, gate = _pre_mod(x, mod_ref, ng_ref, sub)
    o_ref[0] = x + 0.5 * gate * _swiglu(hb, wgu_ref, wd_ref, act_ref)


def _attn_out_ffn_kernel(x_ref, oa_ref, ob_ref, mod_ref, ng_ref, wo_ref, wgu_ref, wd_ref, o_ref, act_ref):
    half = oa_ref.shape[-1]
    y = _dot(oa_ref[0], wo_ref[:half, :]) + _dot(ob_ref[0], wo_ref[half:, :])
    x = x_ref[0] + mod_ref[0, 5:6, :] * y
    hb, gate = _pre_mod(x, mod_ref, ng_ref, 2)
    o_ref[0] = x + 0.5 * gate * _swiglu(hb, wgu_ref, wd_ref, act_ref)


def _gla_out_ffn_kernel(x_ref, of_ref, ob_ref, gt_ref, gn_ref, mod_ref, ng_ref, wo_ref, wgu_ref, wd_ref,
                        o_ref, act_ref, on_ref):
    for h in range(GLA_HEADS):
        hs = slice(h * GLA_DV_HEAD, (h + 1) * GLA_DV_HEAD)
        o = of_ref[0, :, hs] + ob_ref[0, :, hs]
        rinv = lax.rsqrt(jnp.mean(o * o, axis=-1, keepdims=True) + EPS)
        on_ref[:, hs] = (o * rinv * gn_ref[...] * gt_ref[0, :, hs]).astype(BF16)
    x = x_ref[0] + mod_ref[0, 5:6, :] * _dot(on_ref[...], wo_ref[...])
    hb, gate = _pre_mod(x, mod_ref, ng_ref, 2)
    o_ref[0] = x + 0.5 * gate * _swiglu(hb, wgu_ref, wd_ref, act_ref)


def _tok_spec(tm, width):
    return pl.BlockSpec((1, tm, width), lambda b, i: (b, i, 0))


def _mod_spec(mod):
    if mod.shape[0] == 1:
        return pl.BlockSpec((1, N_MOD, D_MODEL), lambda b, i: (0, 0, 0))
    return pl.BlockSpec((1, N_MOD, D_MODEL), lambda b, i: (b, 0, 0))


def _row_tile(t, pref=512):
    return pref if t % pref == 0 else t


def _ffn(x, mod, ng, wgu, wd, sub):
    bx, t, _ = x.shape
    tm = _row_tile(t, FFN_TM)
    return pl.pallas_call(
        functools.partial(_ffn_kernel, sub=sub),
        grid=(bx, t // tm),
        in_specs=[_tok_spec(tm, D_MODEL), _mod_spec(mod), _resident((3, D_MODEL)),
                  _resident(wgu.shape), _resident(wd.shape)],
        out_specs=_tok_spec(tm, D_MODEL),
        out_shape=jax.ShapeDtypeStruct(x.shape, F32),
        scratch_shapes=[pltpu.VMEM((tm, D_FF), BF16)],
        compiler_params=_params("parallel", "parallel"),
        name="ffn",
    )(x, mod, ng, wgu, wd)


def _attn_out_ffn(x, oa, ob, mod, ng, wo, wgu, wd):
    bx, t, _ = x.shape
    tm = _row_tile(t)
    return pl.pallas_call(
        _attn_out_ffn_kernel,
        grid=(bx, t // tm),
        in_specs=[_tok_spec(tm, D_MODEL), _tok_spec(tm, oa.shape[-1]), _tok_spec(tm, ob.shape[-1]),
                  _mod_spec(mod), _resident((3, D_MODEL)), _resident(wo.shape),
                  _resident(wgu.shape), _resident(wd.shape)],
        out_specs=_tok_spec(tm, D_MODEL),
        out_shape=jax.ShapeDtypeStruct(x.shape, F32),
        scratch_shapes=[pltpu.VMEM((tm, D_FF), BF16)],
        compiler_params=_params("parallel", "parallel"),
        name="attn_out_ffn",
    )(x, oa, ob, mod, ng, wo, wgu, wd)


def _gla_out_ffn(x, of, ob, gt, gn, mod, ng, wo, wgu, wd):
    bx, t, _ = x.shape
    tm = _row_tile(t)
    return pl.pallas_call(
        _gla_out_ffn_kernel,
        grid=(bx, t // tm),
        in_specs=[_tok_spec(tm, D_MODEL), _tok_spec(tm, GLA_DV), _tok_spec(tm, GLA_DV), _tok_spec(tm, GLA_DV),
                  _resident(gn.shape), _mod_spec(mod), _resident((3, D_MODEL)), _resident(wo.shape),
                  _resident(wgu.shape), _resident(wd.shape)],
        out_specs=_tok_spec(tm, D_MODEL),
        out_shape=jax.ShapeDtypeStruct(x.shape, F32),
        scratch_shapes=[pltpu.VMEM((tm, D_FF), BF16), pltpu.VMEM((tm, GLA_DV), BF16)],
        compiler_params=_params("parallel", "parallel"),
        name="gla_out_ffn",
    )(x, of, ob, gt, gn, mod, ng, wo, wgu, wd)


def _lane(shape):
    return lax.broadcasted_iota(jnp.int32, shape, 1)


def _rope(x, cos, sin):
    return x * cos + pltpu.roll(x, LANES // 2, 1) * sin


def _ones_row(vt):
    row = lax.broadcasted_iota(jnp.int32, vt.shape, 0)
    return jnp.where((row & (LANES - 1)) == MLA_V_DIM, 1.0, vt).astype(BF16)


def _proj0_kernel(x_ref, mod_ref, ng_ref, win_ref, wuk_ref, wvt_ref, wuq_ref, wwvt_ref, seg_ref, gv_ref,
                  cm_ref, sm_ref, cw_ref, sw_ref, qa_ref, ka_ref, va_ref, qw_ref, kw_ref, vw_ref):
    hb, _ = _pre_mod(x_ref[0], mod_ref, ng_ref, 1)
    tm = hb.shape[0]
    z = _dot(hb, win_ref[...])
    zk = z[:, P0_CKV:P0_CQ]
    cm, sm, cw, sw = cm_ref[...], sm_ref[...], cw_ref[...], sw_ref[...]
    g_kva, g_qa, g_kn2 = gv_ref[0:1, :LANES], gv_ref[1:2, :], gv_ref[2:3, :]
    g_kr, g_q2, g_wq2 = gv_ref[3:4, :LANES], gv_ref[4:5, :], gv_ref[5:6, :]
    g_wk, n_q2 = gv_ref[6:7, :LANES], gv_ref[7:8, :]

    def seg_rinv(v, seg, inv_n):
        return lax.rsqrt(_dot((v * v).astype(BF16), seg) * inv_n + EPS)

    ckv = zk[:, :MLA_KV_RANK]
    ckv = (ckv * lax.rsqrt(jnp.mean(ckv * ckv, axis=-1, keepdims=True) + EPS) * g_kva).astype(BF16)
    kv = _dot(ckv, wuk_ref[...])
    kr = zk[:, MLA_KV_RANK:]
    kr = kr * lax.rsqrt(jnp.sum(kr * kr, axis=-1, keepdims=True) * (1.0 / MLA_ROPE_DIM) + EPS) * g_kr
    kr = _rope(kr, cm, sm)
    kr2 = jnp.concatenate([kr, kr], axis=-1)
    for p in range(MLA_HEADS // 2):
        cols = slice(p * 2 * LANES, (p + 1) * 2 * LANES)
        k2 = kv[:, cols]
        ka_ref[0, :, cols] = (k2 * seg_rinv(k2, seg_ref[0], 1.0 / MLA_NOPE_DIM) * g_kn2 + kr2).astype(BF16)

    zw = z[:, P0_WQ:P0_WK]
    for p in range(WIN_GROUP // 2):
        w2 = zw[:, p * 2 * LANES:(p + 1) * 2 * LANES]
        w2 = w2 * seg_rinv(w2, seg_ref[1], 1.0 / WIN_HEAD_DIM) * g_wq2
        for h in range(2):
            lo = (2 * p + h) * LANES
            qw_ref[0, :, lo:lo + LANES] = _rope(w2[:, h * LANES:(h + 1) * LANES], cw, sw).astype(BF16)
    wk = z[:, P0_WK:P0_COLS]
    wk = wk * seg_rinv(wk, seg_ref[1, :LANES, :LANES], 1.0 / WIN_HEAD_DIM) * g_wk
    kw_ref[0] = _rope(wk, cw, sw).astype(BF16)

    cq = z[:, P0_CQ:P0_WQ]
    cq = (cq * lax.rsqrt(jnp.mean(cq * cq, axis=-1, keepdims=True) + EPS) * g_qa).astype(BF16)
    q = _dot(cq, wuq_ref[...])
    for p in range(MLA_HEADS // 2):
        q2 = q[:, p * 2 * LANES:(p + 1) * 2 * LANES]
        q2 = q2 * seg_rinv(q2, seg_ref[0], n_q2) * g_q2
        for h in range(2):
            lo = (2 * p + h) * LANES
            qa_ref[0, :, lo:lo + LANES] = _rope(q2[:, h * LANES:(h + 1) * LANES], cm, sm).astype(BF16)

    va_ref[0, 0] = _ones_row(_dot_nt(wvt_ref[...], ckv))
    vwt = _ones_row(_dot_nt(wwvt_ref[...], hb))
    for c in range(tm // WIN_BLOCK):
        vw_ref[0, c] = vwt[:, c * WIN_BLOCK:(c + 1) * WIN_BLOCK]


def _proj0(x, mod, ng, win, wuk, wvt, wuq, wwvt, seg, gv, tabs):
    bx, t, _ = x.shape
    tm = _row_tile(t, MLA_TK)
    ntab = tabs[0].shape[0] // tm
    tab_spec = pl.BlockSpec((tm, LANES), lambda b, i: (i % ntab, 0))
    nwb = tm // WIN_BLOCK
    out_specs = [_tok_spec(tm, MLA_HEADS * LANES), _tok_spec(tm, MLA_HEADS * LANES),
                 pl.BlockSpec((1, 1, MLA_HEADS * LANES, tm), lambda b, i: (b, i, 0, 0)),
                 _tok_spec(tm, WIN_HEADS * WIN_HEAD_DIM), _tok_spec(tm, LANES),
                 pl.BlockSpec((1, nwb, 2 * LANES, WIN_BLOCK), lambda b, i: (b, i, 0, 0))]
    out_shape = [jax.ShapeDtypeStruct((bx, t, MLA_HEADS * LANES), BF16),
                 jax.ShapeDtypeStruct((bx, t, MLA_HEADS * LANES), BF16),
                 jax.ShapeDtypeStruct((bx, t // tm, MLA_HEADS * LANES, tm), BF16),
                 jax.ShapeDtypeStruct((bx, t, WIN_HEADS * WIN_HEAD_DIM), BF16),
                 jax.ShapeDtypeStruct((bx, t, LANES), BF16),
                 jax.ShapeDtypeStruct((bx, t // WIN_BLOCK, 2 * LANES, WIN_BLOCK), BF16)]
    return pl.pallas_call(
        _proj0_kernel,
        grid=(bx, t // tm),
        in_specs=[_tok_spec(tm, D_MODEL), _mod_spec(mod), _resident((3, D_MODEL)), _resident(win.shape),
                  _resident(wuk.shape), _resident(wvt.shape), _resident(wuq.shape), _resident(wwvt.shape),
                  _resident(seg.shape), _resident(gv.shape)] + [tab_spec] * 4,
        out_specs=out_specs,
        out_shape=out_shape,
        compiler_params=_params("parallel", "parallel"),
        name="proj0",
    )(x, mod, ng, win, wuk, wvt, wuq, wwvt, seg, gv, *tabs)


def _mla_attn_kernel(*refs, has_latent):
    if has_latent:
        q_ref, kc_ref, vc_ref, k_ref, v_ref, o_ref, m_ref, acc_ref, s_ref, mc_ref = refs
    else:
        q_ref, kc_ref, vc_ref, o_ref, m_ref, acc_ref = refs

    def head(hh):
        return slice(hh * LANES, (hh + 1) * LANES)

    def scores(kblk, hh):
        return _dot_nt(kblk, q_ref[0, :, head(hh)])

    def qk(j, slot, hh):
        st = scores(k_ref[0, pl.ds(pl.multiple_of(j * MLA_TK, MLA_TK), MLA_TK), head(hh)], hh)
        s_ref[slot, hh] = st
        mc_ref[slot, hh] = jnp.max(st, axis=0, keepdims=True)

    def process(j, slot, hh):
        m_prev = m_ref[hh]
        m_new = jnp.maximum(m_prev, mc_ref[slot, hh])
        p = jnp.exp2(s_ref[slot, hh] - m_new).astype(BF16)
        pv = _dot(v_ref[0, j, head(hh), :], p)
        acc_ref[hh] = acc_ref[hh] * jnp.exp2(m_prev - m_new) + pv
        m_ref[hh] = m_new

    def ctx_softmax(st, hh):
        m_new = jnp.max(st, axis=0, keepdims=True)
        p = jnp.exp2(st - m_new).astype(BF16)
        acc_ref[hh] = _dot(vc_ref[0, head(hh)], p)
        m_ref[hh] = m_new

    def pair(nxt, cur, slot):
        for hh in range(2):
            qk(nxt, 1 - slot, hh)
            process(cur, slot, hh)

    sts = [scores(kc_ref[0, :, head(hh)], hh) for hh in range(2)]
    if has_latent:
        n = v_ref.shape[1]
        for hh in range(2):
            qk(0, 0, hh)
            ctx_softmax(sts[hh], hh)

        def body(i, carry):
            pair(2 * i + 1, 2 * i, 0)
            pair(2 * i + 2, 2 * i + 1, 1)
            return carry
        lax.fori_loop(0, n // 2 - 1, body, 0)
        pair(n - 1, n - 2, 0)
        for hh in range(2):
            process(n - 1, 1, hh)
    else:
        for hh in range(2):
            ctx_softmax(sts[hh], hh)
    a0, a1 = acc_ref[0], acc_ref[1]
    ot = jnp.concatenate([a0[:MLA_V_DIM] / a0[MLA_V_DIM:MLA_V_DIM + 1],
                          a1[:MLA_V_DIM] / a1[MLA_V_DIM:MLA_V_DIM + 1]], axis=0)
    o_ref[0] = ot.T.astype(BF16)


def _mla_attn(q, kc, vtc, k=None, vt=None):
    bsz, t, _ = q.shape
    nc = kc.shape[1]
    has_latent = k is not None
    tq = _row_tile(t, MLA_TQ)
    in_specs = [pl.BlockSpec((1, tq, 2 * LANES), lambda b, h, i: (b, i, h)),
                pl.BlockSpec((1, nc, 2 * LANES), lambda b, h, i: (b, 0, h)),
                pl.BlockSpec((1, 2 * LANES, nc), lambda b, h, i: (b, h, 0))]
    args = [q, kc, vtc]
    scratch = [pltpu.VMEM((2, 1, tq), F32), pltpu.VMEM((2, LANES, tq), F32)]
    if has_latent:
        n = k.shape[1]
        nt, _, tk = vt.shape[1:]
        assert tk == MLA_TK and nt * tk == n and nt % 2 == 0, (n, nt, tk)
        in_specs += [pl.BlockSpec((1, n, 2 * LANES), lambda b, h, i: (b, 0, h)),
                     pl.BlockSpec((1, nt, 2 * LANES, tk), lambda b, h, i: (b, 0, h, 0))]
        args += [k, vt]
        scratch += [pltpu.VMEM((2, 2, tk, tq), F32), pltpu.VMEM((2, 2, 1, tq), F32)]
    return pl.pallas_call(
        functools.partial(_mla_attn_kernel, has_latent=has_latent),
        grid=(bsz, MLA_HEADS // 2, t // tq),
        in_specs=in_specs,
        out_specs=pl.BlockSpec((1, tq, LANES), lambda b, h, i: (b, i, h)),
        out_shape=jax.ShapeDtypeStruct((bsz, t, MLA_HEADS * MLA_V_DIM), BF16),
        scratch_shapes=scratch,
        compiler_params=_params("parallel", "parallel", "arbitrary"),
        name="mla_attn" if has_latent else "mla_attn_ctx",
    )(*args)


def _win_attn_kernel(*refs, has_window, nb):
    if has_window:
        sink_ref, q_ref, kc_ref, vc_ref, k_ref, vt_ref, bias_ref, o_ref, s_ref, mc_ref = refs
    else:
        sink_ref, q_ref, kc_ref, vc_ref, o_ref, s_ref, mc_ref = refs
    qb = q_ref.shape[1] // WIN_BLOCK
    nc = kc_ref.shape[1]
    nct = vc_ref.shape[1]
    i = pl.program_id(1)
    units = [(blk, n) for blk in range(qb) for n in range(WIN_KV_HEADS)]
    cache = {}

    def block_operands(blk):
        if blk in cache:
            return cache[blk]
        vts = [vc_ref[0, t] for t in range(nct)]
        bias = None
        if has_window:
            g = i * qb + blk
            start = jnp.clip(g - 1, 0, nb - 3)
            kwin = k_ref[0, pl.ds(pl.multiple_of(start * WIN_BLOCK, WIN_BLOCK), 3 * WIN_BLOCK), :]
            kall = jnp.concatenate([kc_ref[0], kwin], axis=0)
            vts += [vt_ref[0, start + t] for t in range(3)]
            bias = jnp.concatenate([bias_ref[g - start]] * WIN_GROUP, axis=1)
        else:
            kall = kc_ref[0]
        rows = slice(blk * WIN_BLOCK, (blk + 1) * WIN_BLOCK)
        qs = jnp.concatenate([q_ref[0, rows, j * LANES:(j + 1) * LANES] for j in range(WIN_GROUP)], axis=0)
        cache[blk] = (kall, jnp.concatenate(vts, axis=1), qs, bias)
        return cache[blk]

    def qk(u, slot):
        blk, n = units[u]
        kall, _, qs, bias = block_operands(blk)
        head0 = (_lane(qs.shape) & 32) == 0
        zero = jnp.zeros_like(qs)
        st = _dot_nt(kall, jnp.where(head0, qs, zero) if n == 0 else jnp.where(head0, zero, qs))
        if bias is not None:
            st = jnp.concatenate([st[:nc], st[nc:] + bias], axis=0)
        s_ref[slot] = st
        mc_ref[slot] = jnp.max(st, axis=0, keepdims=True)

    outs = {}

    def process(u, slot):
        blk, n = units[u]
        vall = block_operands(blk)[1]
        sink = sink_ref[n]
        m = jnp.maximum(mc_ref[slot], sink)
        p = jnp.exp2(s_ref[slot] - m).astype(BF16)
        pv = _dot(vall[n * LANES:(n + 1) * LANES], p)
        denom = pv[WIN_HEAD_DIM:WIN_HEAD_DIM + 1] + jnp.exp2(sink - m)
        outs[(blk, n)] = pv[:WIN_HEAD_DIM] / denom
        if n == WIN_KV_HEADS - 1:
            rows = slice(blk * WIN_BLOCK, (blk + 1) * WIN_BLOCK)
            for j in range(WIN_GROUP):
                cols = slice(j * WIN_BLOCK, (j + 1) * WIN_BLOCK)
                both = jnp.concatenate([outs[(blk, 0)][:, cols], outs[(blk, 1)][:, cols]], axis=0)
                o_ref[0, rows, j * LANES:(j + 1) * LANES] = both.T.astype(BF16)

    qk(0, 0)
    for u in range(len(units)):
        if u + 1 < len(units):
            qk(u + 1, (u + 1) % 2)
        process(u, u % 2)


def _band_bias():
    krow = np.arange(3 * WIN_BLOCK)[:, None]
    qcol = np.arange(WIN_BLOCK)[None, :]
    keep = [np.abs(krow - (qcol + off * WIN_BLOCK)) <= WINDOW for off in range(3)]
    return jnp.asarray(np.where(np.stack(keep), 0.0, NEG_INF), F32)


def _win_attn(sinkrow, q, kc, vtc, k=None, vt=None):
    bsz, t, _ = q.shape
    nc = kc.shape[1]
    has_window = k is not None
    nb = t // WIN_BLOCK
    tq = _row_tile(t, WIN_TQ)
    kvw = WIN_KV_HEADS * WIN_HEAD_DIM
    nk = nc + (3 * WIN_BLOCK if has_window else 0)
    in_specs = [pl.BlockSpec((WIN_KV_HEADS, 1, WIN_GROUP * WIN_BLOCK), lambda b, i: (0, 0, 0)),
                pl.BlockSpec((1, tq, WIN_HEADS * WIN_HEAD_DIM), lambda b, i: (b, i, 0)),
                pl.BlockSpec((1, nc, kvw), lambda b, i: (b, 0, 0)),
                pl.BlockSpec((1, nc // WIN_BLOCK, 2 * LANES, WIN_BLOCK), lambda b, i: (b, 0, 0, 0))]
    args = [sinkrow, q, kc, vtc]
    if has_window:
        assert nb >= 3, nb
        in_specs += [pl.BlockSpec((1, t, kvw), lambda b, i: (b, 0, 0)),
                     pl.BlockSpec((1, nb, 2 * LANES, WIN_BLOCK), lambda b, i: (b, 0, 0, 0)),
                     pl.BlockSpec((3, 3 * WIN_BLOCK, WIN_BLOCK), lambda b, i: (0, 0, 0))]
        args += [k, vt, _band_bias()]
    return pl.pallas_call(
        functools.partial(_win_attn_kernel, has_window=has_window, nb=nb),
        grid=(bsz, t // tq),
        in_specs=in_specs,
        out_specs=pl.BlockSpec((1, tq, WIN_HEADS * WIN_HEAD_DIM), lambda b, i: (b, i, 0)),
        out_shape=jax.ShapeDtypeStruct((bsz, t, WIN_HEADS * WIN_HEAD_DIM), BF16),
        scratch_shapes=[pltpu.VMEM((2, nk, WIN_GROUP * WIN_BLOCK), F32),
                        pltpu.VMEM((2, 1, WIN_GROUP * WIN_BLOCK), F32)],
        compiler_params=_params("parallel", "parallel"),
        name="win_attn" if has_window else "win_attn_ctx",
    )(*args)


def _log_sigmoid(x):
    return jnp.minimum(x, 0.0) - jnp.log(1.0 + jnp.exp(-jnp.abs(x)))


def _proj1_kernel(x_ref, mod_ref, ng_ref, win_ref, wgk_ref, bgk_ref, k_ref, v_ref, q_ref, gt_ref, gf_ref, gb_ref):
    hb, _ = _pre_mod(x_ref[0], mod_ref, ng_ref, 1)
    ql = _dot(hb, win_ref[:, P1_Q:P1_LOW + LANES])
    q_ref[0] = ql[:, :GLA_DK] * (GLA_DK_HEAD ** -0.5)
    low = ql[:, GLA_DK:].astype(BF16)
    k_ref[0] = _dot(hb, win_ref[:, P1_K:P1_K + GLA_DK])
    pre_f = _dot(low, wgk_ref[:, :GLA_DK]) + bgk_ref[:, :GLA_DK]
    gf_ref[0] = _log_sigmoid(pre_f) * (1.0 / GLA_GATE_NORM)
    v_ref[0] = _dot(hb, win_ref[:, P1_V:P1_V + GLA_DV]).astype(BF16)
    pre_b = _dot(low, wgk_ref[:, GLA_DK:]) + bgk_ref[:, GLA_DK:]
    gb_ref[0] = _log_sigmoid(pre_b) * (1.0 / GLA_GATE_NORM)
    gt_ref[0] = _silu(_dot(hb, win_ref[:, P1_G:P1_G + GLA_DV]))


def _proj1(x, mod, ng, win, wgk, bgk):
    bx, t, _ = x.shape
    tm = _row_tile(t)
    outs = ((GLA_DK, F32), (GLA_DV, BF16), (GLA_DK, F32), (GLA_DV, F32), (GLA_DK, F32), (GLA_DK, F32))
    return pl.pallas_call(
        _proj1_kernel,
        grid=(bx, t // tm),
        in_specs=[_tok_spec(tm, D_MODEL), _mod_spec(mod), _resident((3, D_MODEL)), _resident(win.shape),
                  _resident(wgk.shape), _resident(bgk.shape)],
        out_specs=[_tok_spec(tm, w) for w, _ in outs],
        out_shape=[jax.ShapeDtypeStruct((bx, t, w), d) for w, d in outs],
        compiler_params=_params("parallel", "parallel"),
        name="proj1",
    )(x, mod, ng, win, wgk, bgk)


def _gla_block(q_ref, k_ref, v_ref, g_ref, o_ref, st_ref, bi, cum, keep, backward):
    t = GLA_BLOCK
    g = g_ref[bi]
    g_hi = g.astype(BF16)
    g_lo = (g - g_hi.astype(F32)).astype(BF16)
    both = _dot(cum, g_hi) + _dot(cum, g_lo)
    bb, tot = both[:t], both[t:]
    k = k_ref[bi]
    q_dec = (q_ref[bi] * jnp.exp(bb)).astype(BF16)
    k_inv = (k * jnp.exp(-bb)).astype(BF16)
    k_end = (k * jnp.exp(tot - bb)).astype(BF16)
    decay = jnp.exp(tot)
    n_sub = t // GLA_CHUNK
    order = range(n_sub - 1, -1, -1) if backward else range(n_sub)
    for h in range(GLA_HEADS):
        ks = slice(h * GLA_DK_HEAD, (h + 1) * GLA_DK_HEAD)
        vs = slice(h * GLA_DV_HEAD, (h + 1) * GLA_DV_HEAD)
        vh = v_ref[bi, :, vs]
        a = jnp.where(keep, _dot_nt(q_dec[:, ks], k_inv[:, ks]), 0.0).astype(BF16)
        o_intra = _dot(a, vh)
        st = st_ref[bi, h]
        for c in order:
            rows = slice(c * GLA_CHUNK, (c + 1) * GLA_CHUNK)
            o_ref[bi, rows, vs] = o_intra[rows] + _dot_nt(q_dec[rows, ks], st.astype(BF16))
            st = st * decay[c * GLA_CHUNK:c * GLA_CHUNK + 1, ks] + _dot_tn(vh[rows], k_end[rows, ks])
        st_ref[bi, h] = st


def _gla_scan_kernel(qf_ref, kf_ref, vf_ref, gf_ref, qb_ref, kb_ref, vb_ref, gb_ref, s0f_ref, s0b_ref, cum_ref,
                     of_ref, ob_ref, sf_ref, sb_ref, stf_ref, stb_ref):
    i = pl.program_id(1)

    @pl.when(i == 0)
    def _():
        stf_ref[...] = s0f_ref[...]
        stb_ref[...] = s0b_ref[...]

    row = lax.broadcasted_iota(jnp.int32, (GLA_BLOCK, GLA_BLOCK), 0)
    col = lax.broadcasted_iota(jnp.int32, (GLA_BLOCK, GLA_BLOCK), 1)
    same = (row // GLA_CHUNK) == (col // GLA_CHUNK)
    for bi in range(qf_ref.shape[0]):
        _gla_block(qf_ref, kf_ref, vf_ref, gf_ref, of_ref, stf_ref, bi, cum_ref[0], same & (col <= row), False)
        _gla_block(qb_ref, kb_ref, vb_ref, gb_ref, ob_ref, stb_ref, bi, cum_ref[1], same & (col >= row), True)

    @pl.when(i == pl.num_programs(1) - 1)
    def _():
        sf_ref[...] = stf_ref[...]
        sb_ref[...] = stb_ref[...]


def _cum_matrices():
    r = np.arange(GLA_BLOCK)[:, None]
    c = np.arange(GLA_BLOCK)[None, :]
    same = (r // GLA_CHUNK) == (c // GLA_CHUNK)
    fwd = np.concatenate([same & (c <= r), same], axis=0)
    bwd = np.concatenate([same & (c >= r), same], axis=0)
    return jnp.asarray(np.stack([fwd, bwd]), BF16)


def _gla_scan(q, k, v, gf, gb, s0f, s0b):
    bsz, t, _ = q.shape
    tb = GLA_BLOCK
    assert t % tb == 0, t
    nblk = t // tb
    nbat = 2 if bsz % 2 == 0 else 1
    fwd = lambda w: pl.BlockSpec((nbat, tb, w), lambda b, i: (b, i, 0))
    bwd = lambda w: pl.BlockSpec((nbat, tb, w), lambda b, i: (b, nblk - 1 - i, 0))
    st_spec = pl.BlockSpec((nbat, GLA_HEADS, GLA_DV_HEAD, GLA_DK_HEAD), lambda b, i: (b, 0, 0, 0))
    st_shape = jax.ShapeDtypeStruct((bsz, GLA_HEADS, GLA_DV_HEAD, GLA_DK_HEAD), F32)
    cum = _cum_matrices()
    return pl.pallas_call(
        _gla_scan_kernel,
        grid=(bsz // nbat, nblk),
        in_specs=[fwd(GLA_DK), fwd(GLA_DK), fwd(GLA_DV), fwd(GLA_DK),
                  bwd(GLA_DK), bwd(GLA_DK), bwd(GLA_DV), bwd(GLA_DK), st_spec, st_spec, _resident(cum.shape)],
        out_specs=[fwd(GLA_DV), bwd(GLA_DV), st_spec, st_spec],
        out_shape=[jax.ShapeDtypeStruct((bsz, t, GLA_DV), F32)] * 2 + [st_shape] * 2,
        scratch_shapes=[pltpu.VMEM((nbat, GLA_HEADS, GLA_DV_HEAD, GLA_DK_HEAD), F32)] * 2,
        compiler_params=_params("parallel", "arbitrary"),
        name="gla_scan",
    )(q, k, v, gf, q, k, v, gb, s0f, s0b, cum)


def _deinterleave(n):
    return np.concatenate([np.arange(0, n, 2), np.arange(1, n, 2)])


def _gather_cols(w, idx):
    idx = np.asarray(idx)
    cols = jnp.take(w, jnp.asarray(np.maximum(idx, 0)), axis=-1)
    return jnp.where(jnp.asarray(idx >= 0), cols, 0.0)


def _mla_slot(nope, rope):
    pad = np.full((16,), -1, np.int64)
    return np.concatenate([nope[:32], rope[:16], pad, nope[32:], rope[16:], pad])


def _win_slot(a, b):
    return np.concatenate([a[:32], b[:32], a[32:], b[32:]])


def _take(vec, idx):
    return jnp.where(jnp.asarray(idx >= 0), vec[jnp.asarray(np.maximum(idx, 0))], 0.0)


def _layer0_layout(p):
    ckv0, kr0 = 0, MLA_KV_RANK
    wk0 = kr0 + MLA_ROPE_DIM
    wv0 = wk0 + WIN_KV_HEADS * WIN_HEAD_DIM
    cq0 = wv0 + WIN_KV_HEADS * WIN_HEAD_DIM
    wq0 = cq0 + MLA_Q_RANK
    de64, de32 = _deinterleave(WIN_HEAD_DIM), _deinterleave(MLA_ROPE_DIM)
    none64, none32 = np.full((64,), -1, np.int64), np.full((32,), -1, np.int64)

    idx = np.full((P0_COLS,), -1, np.int64)
    idx[P0_CKV:P0_CKV + MLA_KV_RANK] = ckv0 + np.arange(MLA_KV_RANK)
    idx[P0_WK:P0_WK + LANES] = _win_slot(wk0 + de64, wk0 + 64 + de64)
    idx[P0_CQ:P0_CQ + MLA_Q_RANK] = cq0 + np.arange(MLA_Q_RANK)
    for j in range(WIN_GROUP):
        idx[P0_WQ + j * LANES:P0_WQ + (j + 1) * LANES] = _win_slot(wq0 + j * 64 + de64,
                                                                   wq0 + (WIN_GROUP + j) * 64 + de64)
    idx[P0_KR:P0_KR + LANES] = _mla_slot(none64, kr0 + de32)
    w_in = _gather_cols(p["w_in"], idx).astype(BF16)
    vidx = np.full((WIN_KV_HEADS * LANES,), -1, np.int64)
    for n in range(WIN_KV_HEADS):
        vidx[n * LANES:n * LANES + 64] = wv0 + n * 64 + np.arange(64)
    w_wvt = _gather_cols(p["w_in"], vidx).T.astype(BF16)

    per = MLA_NOPE_DIM + MLA_V_DIM
    kidx, qidx, vidx = [], [], []
    for h in range(MLA_HEADS):
        kidx.append(_mla_slot(h * per + np.arange(64), none32))
        qidx.append(_mla_slot(h * MLA_QK_DIM + np.arange(64), h * MLA_QK_DIM + MLA_NOPE_DIM + de32))
        vidx.append(np.concatenate([h * per + MLA_NOPE_DIM + np.arange(64), none64]))
    w_uk = _gather_cols(p["mla_w_ukv"], np.concatenate(kidx)).astype(BF16)
    w_vt = _gather_cols(p["mla_w_ukv"], np.concatenate(vidx)).T.astype(BF16)
    w_uq = _gather_cols(p["mla_w_uq"], np.concatenate(qidx)).astype(BF16)

    lane = np.arange(LANES)
    nope, rot = (lane & 32) == 0, (lane & 48) == 32
    seg_a = (nope[:, None] & nope[None, :]) | (rot[:, None] & rot[None, :])
    seg_b = nope[:, None] == nope[None, :]
    two = lambda m: np.kron(np.eye(2), m.astype(np.float32))
    seg = jnp.asarray(np.stack([two(seg_a), two(seg_b)]), BF16)

    a64, a32 = np.arange(64), np.arange(32)
    twice = lambda v: jnp.concatenate([v, v])
    wide = lambda v: jnp.concatenate([v, jnp.zeros((LANES,), F32)])
    g_wq = _take(p["win_g_q"], _win_slot(de64, de64)) * (WIN_SCALE * LOG2E)
    g_wk = _take(p["win_g_k"], _win_slot(de64, de64))
    g_qslot = jnp.concatenate([p["mla_g_qn"], p["mla_g_qr"]])
    inv_n = jnp.concatenate([jnp.full((64,), 1.0 / MLA_NOPE_DIM, F32), jnp.full((32,), 1.0 / MLA_ROPE_DIM, F32)])
    gv = jnp.stack([
        wide(p["mla_g_kva"]),
        p["mla_g_qa"],
        twice(_take(p["mla_g_kn"], _mla_slot(a64, none32))),
        wide(_take(p["mla_g_kr"], _mla_slot(none64, de32))),
        twice(_take(g_qslot, _mla_slot(a64, 64 + de32))) * (MLA_SCALE * LOG2E),
        twice(g_wq),
        wide(g_wk),
        twice(_take(inv_n, _mla_slot(a64, 64 + a32))),
    ])

    out_rows = np.arange(D_MODEL)
    base = MLA_HEADS * MLA_V_DIM
    for j in range(WIN_GROUP):
        for n in range(WIN_KV_HEADS):
            lo = base + j * LANES + n * 64
            out_rows[lo:lo + 64] = base + (n * WIN_GROUP + j) * 64 + np.arange(64)
    w_out = p["w_out"][jnp.asarray(out_rows)].astype(BF16)
    return (w_in, w_uk, w_vt, w_uq, w_wvt, seg, gv), w_out


def _layer1_layout(p):
    k0, v0 = 0, GLA_DK
    lf0 = v0 + GLA_DV
    lb0 = lf0 + GLA_LOWRANK
    q0 = lb0 + GLA_LOWRANK
    g0 = q0 + GLA_DK
    idx = np.full((P1_COLS,), -1, np.int64)
    idx[P1_K:P1_K + GLA_DK] = k0 + np.arange(GLA_DK)
    idx[P1_V:P1_V + GLA_DV] = v0 + np.arange(GLA_DV)
    idx[P1_Q:P1_Q + GLA_DK] = q0 + np.arange(GLA_DK)
    idx[P1_G:P1_G + GLA_DV] = g0 + np.arange(GLA_DV)
    idx[P1_LOW:P1_LOW + GLA_LOWRANK] = lf0 + np.arange(GLA_LOWRANK)
    idx[P1_LOW + GLA_LOWRANK:P1_LOW + 2 * GLA_LOWRANK] = lb0 + np.arange(GLA_LOWRANK)
    w_in = _gather_cols(p["w_in"], idx).astype(BF16)
    wgk = jnp.zeros((LANES, 2 * GLA_DK), F32)
    wgk = wgk.at[:GLA_LOWRANK, :GLA_DK].set(p["w_gk_f"])
    wgk = wgk.at[GLA_LOWRANK:2 * GLA_LOWRANK, GLA_DK:].set(p["w_gk_b"])
    bgk = jnp.concatenate([p["b_gk_f"], p["b_gk_b"]]).reshape(1, 2 * GLA_DK)
    return w_in, wgk.astype(BF16), bgk


def _rope_tables(seq):
    t = np.arange(seq)
    row = (t // GRID_W).astype(np.float32)
    col = (t % GRID_W).astype(np.float32)

    def angles(rot_dim):
        n_freq = rot_dim // 4
        inv = (ROPE_BASE ** (-np.arange(n_freq, dtype=np.float32) / n_freq)).astype(np.float32)
        return np.concatenate([row[:, None] * inv, col[:, None] * inv], axis=-1)

    am, aw = angles(MLA_ROPE_DIM), angles(WIN_HEAD_DIM)
    cos_m, sin_m, cos_w, sin_w = np.cos(am), np.sin(am), np.cos(aw), np.sin(aw)
    one32, one16 = np.ones((seq, 32)), np.ones((seq, 16))
    cm = np.concatenate([one32, cos_m, one16] * 2, axis=-1)
    sm = np.concatenate([0 * one32, -sin_m, 0 * one16, 0 * one32, sin_m, 0 * one16], axis=-1)
    cw = np.concatenate([cos_w] * 4, axis=-1)
    sw = np.concatenate([-sin_w, -sin_w, sin_w, sin_w], axis=-1)
    return tuple(jnp.asarray(a, F32) for a in (cm, sm, cw, sw))


def _identity_tables(rows):
    one, zero = jnp.ones((rows, LANES), F32), jnp.zeros((rows, LANES), F32)
    return one, zero, one, zero


def kernel(x, c, ctx, c_ctx, l0_norm_g, l0_w_mod, l0_b_mod, l0_ffn1_w_gu, l0_ffn1_w_down, l0_ffn2_w_gu, l0_ffn2_w_down, l0_w_in, l0_mla_g_qa, l0_mla_g_kva, l0_mla_w_uq, l0_mla_w_ukv, l0_mla_g_qn, l0_mla_g_qr, l0_mla_g_kn, l0_mla_g_kr, l0_win_g_q, l0_win_g_k, l0_win_sink, l0_w_out, l1_norm_g, l1_w_mod, l1_b_mod, l1_ffn1_w_gu, l1_ffn1_w_down, l1_ffn2_w_gu, l1_ffn2_w_down, l1_w_in, l1_w_gk_f, l1_b_gk_f, l1_w_gk_b, l1_b_gk_b, l1_g_norm, l1_w_out):
    bsz, seq, _ = x.shape
    n_ctx = ctx.shape[1]
    out_dtype = x.dtype
    x = x.astype(F32)
    xc = ctx.astype(F32)

    cc = jnp.zeros((16, D_MODEL), F32).at[:bsz].set(c).at[bsz].set(c_ctx)

    def mods(w_mod, b_mod):
        tab = _mod_table(cc, w_mod, b_mod).reshape(16, N_MOD, D_MODEL)
        return tab[:bsz], tab[bsz:bsz + 1]

    p0 = dict(w_in=l0_w_in, mla_g_qa=l0_mla_g_qa, mla_g_kva=l0_mla_g_kva, mla_w_uq=l0_mla_w_uq,
              mla_w_ukv=l0_mla_w_ukv, mla_g_qn=l0_mla_g_qn, mla_g_qr=l0_mla_g_qr, mla_g_kn=l0_mla_g_kn,
              mla_g_kr=l0_mla_g_kr, win_g_q=l0_win_g_q, win_g_k=l0_win_g_k, w_out=l0_w_out)
    proj_w, w_out0 = _layer0_layout(p0)
    mod, mod_c = mods(l0_w_mod, l0_b_mod)
    wgu1, wd1 = l0_ffn1_w_gu.astype(BF16), l0_ffn1_w_down.astype(BF16)
    wgu2, wd2 = l0_ffn2_w_gu.astype(BF16), l0_ffn2_w_down.astype(BF16)
    sinkrow = jnp.repeat(l0_win_sink.astype(F32) * LOG2E, WIN_BLOCK, axis=1).reshape(WIN_KV_HEADS, 1, -1)

    x = _ffn(x, mod, l0_norm_g, wgu1, wd1, 0)
    xc = _ffn(xc, mod_c, l0_norm_g, wgu1, wd1, 0)
    qa, ka, vat, qw, kw, vwt = _proj0(x, mod, l0_norm_g, *proj_w, _rope_tables(seq))
    qa_c, ka_c, vat_c, qw_c, kw_c, vwt_c = _proj0(xc, mod_c, l0_norm_g, *proj_w, _identity_tables(n_ctx))
    vat_c = vat_c.reshape(bsz, MLA_HEADS * LANES, n_ctx)
    oa = _mla_attn(qa, ka_c, vat_c, ka, vat)
    ow = _win_attn(sinkrow, qw, kw_c, vwt_c, kw, vwt)
    oa_c = _mla_attn(qa_c, ka_c, vat_c)
    ow_c = _win_attn(sinkrow, qw_c, kw_c, vwt_c)
    x = _attn_out_ffn(x, oa, ow, mod, l0_norm_g, w_out0, wgu2, wd2)
    xc = _attn_out_ffn(xc, oa_c, ow_c, mod_c, l0_norm_g, w_out0, wgu2, wd2)

    w_in1, wgk, bgk = _layer1_layout(dict(w_in=l1_w_in, w_gk_f=l1_w_gk_f, b_gk_f=l1_b_gk_f,
                                          w_gk_b=l1_w_gk_b, b_gk_b=l1_b_gk_b))
    mod, mod_c = mods(l1_w_mod, l1_b_mod)
    wgu1, wd1 = l1_ffn1_w_gu.astype(BF16), l1_ffn1_w_down.astype(BF16)
    wgu2, wd2 = l1_ffn2_w_gu.astype(BF16), l1_ffn2_w_down.astype(BF16)

    x = _ffn(x, mod, l1_norm_g, wgu1, wd1, 0)
    xc = _ffn(xc, mod_c, l1_norm_g, wgu1, wd1, 0)
    k1, v1, q1, gt1, gf1, gb1 = _proj1(x, mod, l1_norm_g, w_in1, wgk, bgk)
    k1c, v1c, q1c, _, gf1c, gb1c = _proj1(xc, mod_c, l1_norm_g, w_in1, wgk, bgk)
    zero_state = jnp.zeros((bsz, GLA_HEADS, GLA_DV_HEAD, GLA_DK_HEAD), F32)
    _, _, s_f, s_b = _gla_scan(q1c, k1c, v1c, gf1c, gb1c, zero_state, zero_state)
    o_f, o_b, _, _ = _gla_scan(q1, k1, v1, gf1, gb1, s_f, s_b)
    x = _gla_out_ffn(x, o_f, o_b, gt1, l1_g_norm.reshape(1, GLA_DV_HEAD), mod, l1_norm_g,
                     l1_w_out.astype(BF16), wgu2, wd2)
    return x.astype(out_dtype)
```

```python
import functools

import numpy as np
import jax
import jax.numpy as jnp
from jax import lax
from jax.experimental import pallas as pl
from jax.experimental.pallas import tpu as pltpu

F32 = jnp.float32
BF16 = jnp.bfloat16

D_MODEL = 1024
GRID_W = 64
N_MOD = 9
EPS = 1e-6
ROPE_BASE = 10000.0
NEG_INF = -1e30
D_FF = 2816
LOG2E = 1.4426950408889634

MLA_HEADS = 8
MLA_Q_RANK = 256
MLA_KV_RANK = 128
MLA_NOPE_DIM = 64
MLA_ROPE_DIM = 32
MLA_V_DIM = 64
MLA_QK_DIM = MLA_NOPE_DIM + MLA_ROPE_DIM
MLA_SCALE = MLA_QK_DIM ** -0.5

WIN_HEADS = 8
WIN_KV_HEADS = 2
WIN_GROUP = WIN_HEADS // WIN_KV_HEADS
WIN_HEAD_DIM = 64
WINDOW = 128
WIN_BLOCK = 128
WIN_SCALE = WIN_HEAD_DIM ** -0.5

GLA_HEADS = 4
GLA_DK = D_MODEL // 2
GLA_DV = D_MODEL
GLA_DK_HEAD = GLA_DK // GLA_HEADS
GLA_DV_HEAD = GLA_DV // GLA_HEADS
GLA_LOWRANK = 16
GLA_GATE_NORM = 16.0
GLA_CHUNK = 64
GLA_BLOCK = 256

LANES = 128
FF_CHUNK = 256
FFN_TM = 512
MLA_TQ = 1024
MLA_TK = 512
WIN_TQ = 1024
VMEM_LIMIT = 58 * 1024 * 1024

P0_CKV, P0_KR, P0_CQ, P0_WQ, P0_WK, P0_COLS = 0, 128, 256, 512, 1024, 1152
P1_K, P1_V, P1_Q, P1_LOW, P1_G, P1_COLS = 0, 512, 1536, 2048, 2176, 3200


def _dot(a, b):
    return jnp.dot(a, b, preferred_element_type=F32)


def _dot_nt(a, b):
    return lax.dot_general(a, b, (((1,), (1,)), ((), ())), preferred_element_type=F32)


def _dot_tn(a, b):
    return lax.dot_general(a, b, (((0,), (0,)), ((), ())), preferred_element_type=F32)


def _params(*sem):
    return pltpu.CompilerParams(dimension_semantics=sem, vmem_limit_bytes=VMEM_LIMIT)


def _resident(shape):
    nd = len(shape)
    return pl.BlockSpec(shape, lambda *_: (0,) * nd, pipeline_mode=pl.Buffered(1))


def _silu(x):
    return x / (1.0 + jnp.exp(-x))


def _pre_mod(x, mod_ref, ng_ref, sub):
    shift = mod_ref[0, 3 * sub:3 * sub + 1, :]
    scale = mod_ref[0, 3 * sub + 1:3 * sub + 2, :]
    gate = mod_ref[0, 3 * sub + 2:3 * sub + 3, :]
    rinv = lax.rsqrt(jnp.mean(x * x, axis=-1, keepdims=True) + EPS)
    h = x * rinv * ng_ref[sub:sub + 1, :] * (1.0 + scale) + shift
    return h.astype(BF16), gate


def _swiglu(hb, wgu_ref, wd_ref, act_ref):
    for c in range(D_FF // FF_CHUNK):
        lo = c * FF_CHUNK
        g = _dot(hb, wgu_ref[:, lo:lo + FF_CHUNK])
        u = _dot(hb, wgu_ref[:, D_FF + lo:D_FF + lo + FF_CHUNK])
        act_ref[:, lo:lo + FF_CHUNK] = (_silu(g) * u).astype(BF16)
    return _dot(act_ref[...], wd_ref[...])


def _mod_kernel(c_ref, w_ref, b_ref, o_ref):
    a = _silu(c_ref[...]).astype(BF16)
    o_ref[...] = _dot(a, w_ref[...].astype(BF16)) + b_ref[...]


def _mod_table(cc, w_mod, b_mod):
    rows = cc.shape[0]
    n = w_mod.shape[1]
    bn = 1152
    return pl.pallas_call(
        _mod_kernel,
        grid=(n // bn,),
        in_specs=[pl.BlockSpec((rows, D_MODEL), lambda j: (0, 0)),
                  pl.BlockSpec((D_MODEL, bn), lambda j: (0, j)),
                  pl.BlockSpec((1, bn), lambda j: (0, j))],
        out_specs=pl.BlockSpec((rows, bn), lambda j: (0, j)),
        out_shape=jax.ShapeDtypeStruct((rows, n), F32),
        compiler_params=_params("arbitrary"),
        name="mod_table",
    )(cc, w_mod, b_mod.reshape(1, n))


def _ffn_kernel(x_ref, mod_ref, ng_ref, wgu_ref, wd_ref, o_ref, act_ref, *, sub):
    x = x_ref[0]
    hb, gate = _pre_mod(x, mod_ref, ng_ref, sub)
    o_ref[0] = x + 0.5 * gate * _swiglu(hb, wgu_ref, wd_ref, act_ref)


def _attn_out_ffn_kernel(x_ref, oa_ref, ob_ref, mod_ref, ng_ref, wo_ref, wgu_ref, wd_ref, o_ref, act_ref):
    half = oa_ref.shape[-1]
    y = _dot(oa_ref[0], wo_ref[:half, :]) + _dot(ob_ref[0], wo_ref[half:, :])
    x = x_ref[0] + mod_ref[0, 5:6, :] * y
    hb, gate = _pre_mod(x, mod_ref, ng_ref, 2)
    o_ref[0] = x + 0.5 * gate * _swiglu(hb, wgu_ref, wd_ref, act_ref)


def _gla_out_ffn_kernel(x_ref, of_ref, ob_ref, gt_ref, gn_ref, mod_ref, ng_ref, wo_ref, wgu_ref, wd_ref,
                        o_ref, act_ref, on_ref):
    for h in range(GLA_HEADS):
        hs = slice(h * GLA_DV_HEAD, (h + 1) * GLA_DV_HEAD)
        o = of_ref[0, :, hs] + ob_ref[0, :, hs]
        rinv = lax.rsqrt(jnp.mean(o * o, axis=-1, keepdims=True) + EPS)
        on_ref[:, hs] = (o * rinv * gn_ref[...] * gt_ref[0, :, hs]).astype(BF16)
    x = x_ref[0] + mod_ref[0, 5:6, :] * _dot(on_ref[...], wo_ref[...])
    hb, gate = _pre_mod(x, mod_ref, ng_ref, 2)
    o_ref[0] = x + 0.5 * gate * _swiglu(hb, wgu_ref, wd_ref, act_ref)


def _tok_spec(tm, width):
    return pl.BlockSpec((1, tm, width), lambda b, i: (b, i, 0))


def _mod_spec(mod):
    if mod.shape[0] == 1:
        return pl.BlockSpec((1, N_MOD, D_MODEL), lambda b, i: (0, 0, 0))
    return pl.BlockSpec((1, N_MOD, D_MODEL), lambda b, i: (b, 0, 0))


def _row_tile(t, pref=512):
    return pref if t % pref == 0 else t


def _ffn(x, mod, ng, wgu, wd, sub):
    bx, t, _ = x.shape
    tm = _row_tile(t, FFN_TM)
    return pl.pallas_call(
        functools.partial(_ffn_kernel, sub=sub),
        grid=(bx, t // tm),
        in_specs=[_tok_spec(tm, D_MODEL), _mod_spec(mod), _resident((3, D_MODEL)),
                  _resident(wgu.shape), _resident(wd.shape)],
        out_specs=_tok_spec(tm, D_MODEL),
        out_shape=jax.ShapeDtypeStruct(x.shape, F32),
        scratch_shapes=[pltpu.VMEM((tm, D_FF), BF16)],
        compiler_params=_params("parallel", "parallel"),
        name="ffn",
    )(x, mod, ng, wgu, wd)


def _attn_out_ffn(x, oa, ob, mod, ng, wo, wgu, wd):
    bx, t, _ = x.shape
    tm = _row_tile(t)
    return pl.pallas_call(
        _attn_out_ffn_kernel,
        grid=(bx, t // tm),
        in_specs=[_tok_spec(tm, D_MODEL), _tok_spec(tm, oa.shape[-1]), _tok_spec(tm, ob.shape[-1]),
                  _mod_spec(mod), _resident((3, D_MODEL)), _resident(wo.shape),
                  _resident(wgu.shape), _resident(wd.shape)],
        out_specs=_tok_spec(tm, D_MODEL),
        out_shape=jax.ShapeDtypeStruct(x.shape, F32),
        scratch_shapes=[pltpu.VMEM((tm, D_FF), BF16)],
        compiler_params=_params("parallel", "parallel"),
        name="attn_out_ffn",
    )(x, oa, ob, mod, ng, wo, wgu, wd)


def _gla_out_ffn(x, of, ob, gt, gn, mod, ng, wo, wgu, wd):
    bx, t, _ = x.shape
    tm = _row_tile(t)
    return pl.pallas_call(
        _gla_out_ffn_kernel,
        grid=(bx, t // tm),
        in_specs=[_tok_spec(tm, D_MODEL), _tok_spec(tm, GLA_DV), _tok_spec(tm, GLA_DV), _tok_spec(tm, GLA_DV),
                  _resident(gn.shape), _mod_spec(mod), _resident((3, D_MODEL)), _resident(wo.shape),
                  _resident(wgu.shape), _resident(wd.shape)],
        out_specs=_tok_spec(tm, D_MODEL),
        out_shape=jax.ShapeDtypeStruct(x.shape, F32),
        scratch_shapes=[pltpu.VMEM((tm, D_FF), BF16), pltpu.VMEM((tm, GLA_DV), BF16)],
        compiler_params=_params("parallel", "parallel"),
        name="gla_out_ffn",
    )(x, of, ob, gt, gn, mod, ng, wo, wgu, wd)


def _lane(shape):
    return lax.broadcasted_iota(jnp.int32, shape, 1)


def _rope(x, cos, sin):
    return x * cos + pltpu.roll(x, LANES // 2, 1) * sin


def _ones_row(vt):
    row = lax.broadcasted_iota(jnp.int32, vt.shape, 0)
    return jnp.where((row & (LANES - 1)) == MLA_V_DIM, 1.0, vt).astype(BF16)


def _proj0_kernel(x_ref, mod_ref, ng_ref, win_ref, wuk_ref, wvt_ref, wuq_ref, wwvt_ref, seg_ref, gv_ref,
                  cm_ref, sm_ref, cw_ref, sw_ref, qa_ref, ka_ref, va_ref, qw_ref, kw_ref, vw_ref):
    hb, _ = _pre_mod(x_ref[0], mod_ref, ng_ref, 1)
    tm = hb.shape[0]
    z = _dot(hb, win_ref[...])
    zk = z[:, P0_CKV:P0_CQ]
    cm, sm, cw, sw = cm_ref[...], sm_ref[...], cw_ref[...], sw_ref[...]
    g_kva, g_qa, g_kn2 = gv_ref[0:1, :LANES], gv_ref[1:2, :], gv_ref[2:3, :]
    g_kr, g_q2, g_wq2 = gv_ref[3:4, :LANES], gv_ref[4:5, :], gv_ref[5:6, :]
    g_wk, n_q2 = gv_ref[6:7, :LANES], gv_ref[7:8, :]

    def seg_rinv(v, seg, inv_n):
        return lax.rsqrt(_dot((v * v).astype(BF16), seg) * inv_n + EPS)

    ckv = zk[:, :MLA_KV_RANK]
    ckv = (ckv * lax.rsqrt(jnp.mean(ckv * ckv, axis=-1, keepdims=True) + EPS) * g_kva).astype(BF16)
    kv = _dot(ckv, wuk_ref[...])
    kr = zk[:, MLA_KV_RANK:]
    kr = kr * lax.rsqrt(jnp.sum(kr * kr, axis=-1, keepdims=True) * (1.0 / MLA_ROPE_DIM) + EPS) * g_kr
    kr = _rope(kr, cm, sm)
    kr2 = jnp.concatenate([kr, kr], axis=-1)
    for p in range(MLA_HEADS // 2):
        cols = slice(p * 2 * LANES, (p + 1) * 2 * LANES)
        k2 = kv[:, cols]
        ka_ref[0, :, cols] = (k2 * seg_rinv(k2, seg_ref[0], 1.0 / MLA_NOPE_DIM) * g_kn2 + kr2).astype(BF16)

    zw = z[:, P0_WQ:P0_WK]
    for p in range(WIN_GROUP // 2):
        w2 = zw[:, p * 2 * LANES:(p + 1) * 2 * LANES]
        w2 = w2 * seg_rinv(w2, seg_ref[1], 1.0 / WIN_HEAD_DIM) * g_wq2
        for h in range(2):
            lo = (2 * p + h) * LANES
            qw_ref[0, :, lo:lo + LANES] = _rope(w2[:, h * LANES:(h + 1) * LANES], cw, sw).astype(BF16)
    wk = z[:, P0_WK:P0_COLS]
    wk = wk * seg_rinv(wk, seg_ref[1, :LANES, :LANES], 1.0 / WIN_HEAD_DIM) * g_wk
    kw_ref[0] = _rope(wk, cw, sw).astype(BF16)

    cq = z[:, P0_CQ:P0_WQ]
    cq = (cq * lax.rsqrt(jnp.mean(cq * cq, axis=-1, keepdims=True) + EPS) * g_qa).astype(BF16)
    q = _dot(cq, wuq_ref[...])
    for p in range(MLA_HEADS // 2):
        q2 = q[:, p * 2 * LANES:(p + 1) * 2 * LANES]
        q2 = q2 * seg_rinv(q2, seg_ref[0], n_q2) * g_q2
        for h in range(2):
            lo = (2 * p + h) * LANES
            qa_ref[0, :, lo:lo + LANES] = _rope(q2[:, h * LANES:(h + 1) * LANES], cm, sm).astype(BF16)

    va_ref[0, 0] = _ones_row(_dot_nt(wvt_ref[...], ckv))
    vwt = _ones_row(_dot_nt(wwvt_ref[...], hb))
    for c in range(tm // WIN_BLOCK):
        vw_ref[0, c] = vwt[:, c * WIN_BLOCK:(c + 1) * WIN_BLOCK]


def _proj0(x, mod, ng, win, wuk, wvt, wuq, wwvt, seg, gv, tabs):
    bx, t, _ = x.shape
    tm = _row_tile(t, MLA_TK)
    ntab = tabs[0].shape[0] // tm
    tab_spec = pl.BlockSpec((tm, LANES), lambda b, i: (i % ntab, 0))
    nwb = tm // WIN_BLOCK
    out_specs = [_tok_spec(tm, MLA_HEADS * LANES), _tok_spec(tm, MLA_HEADS * LANES),
                 pl.BlockSpec((1, 1, MLA_HEADS * LANES, tm), lambda b, i: (b, i, 0, 0)),
                 _tok_spec(tm, WIN_HEADS * WIN_HEAD_DIM), _tok_spec(tm, LANES),
                 pl.BlockSpec((1, nwb, 2 * LANES, WIN_BLOCK), lambda b, i: (b, i, 0, 0))]
    out_shape = [jax.ShapeDtypeStruct((bx, t, MLA_HEADS * LANES), BF16),
                 jax.ShapeDtypeStruct((bx, t, MLA_HEADS * LANES), BF16),
                 jax.ShapeDtypeStruct((bx, t // tm, MLA_HEADS * LANES, tm), BF16),
                 jax.ShapeDtypeStruct((bx, t, WIN_HEADS * WIN_HEAD_DIM), BF16),
                 jax.ShapeDtypeStruct((bx, t, LANES), BF16),
                 jax.ShapeDtypeStruct((bx, t // WIN_BLOCK, 2 * LANES, WIN_BLOCK), BF16)]
    return pl.pallas_call(
        _proj0_kernel,
        grid=(bx, t // tm),
        in_specs=[_tok_spec(tm, D_MODEL), _mod_spec(mod), _resident((3, D_MODEL)), _resident(win.shape),
                  _resident(wuk.shape), _resident(wvt.shape), _resident(wuq.shape), _resident(wwvt.shape),
                  _resident(seg.shape), _resident(gv.shape)] + [tab_spec] * 4,
        out_specs=out_specs,
        out_shape=out_shape,
        compiler_params=_params("parallel", "parallel"),
        name="proj0",
    )(x, mod, ng, win, wuk, wvt, wuq, wwvt, seg, gv, *tabs)


def _mla_attn_kernel(*refs, has_latent):
    if has_latent:
        q_ref, kc_ref, vc_ref, k_ref, v_ref, o_ref, m_ref, acc_ref, qt_ref, s_ref, mc_ref = refs
    else:
        q_ref, kc_ref, vc_ref, o_ref, m_ref, acc_ref, qt_ref = refs

    def head(hh):
        return slice(hh * LANES, (hh + 1) * LANES)

    for hh in range(2):
        qt_ref[hh] = q_ref[0, :, head(hh)].astype(F32).T.astype(BF16)

    def scores(kblk, hh):
        return _dot(kblk, qt_ref[hh])

    def qk(j, slot, hh):
        st = scores(k_ref[0, pl.ds(pl.multiple_of(j * MLA_TK, MLA_TK), MLA_TK), head(hh)], hh)
        s_ref[slot, hh] = st
        mc_ref[slot, hh] = jnp.max(st, axis=0, keepdims=True)

    def process(j, slot, hh):
        m_prev = m_ref[hh]
        m_new = jnp.maximum(m_prev, mc_ref[slot, hh])
        p = jnp.exp2(s_ref[slot, hh] - m_new).astype(BF16)
        pv = _dot(v_ref[0, j, head(hh), :], p)
        acc_ref[hh] = acc_ref[hh] * jnp.exp2(m_prev - m_new) + pv
        m_ref[hh] = m_new

    def ctx_softmax(st, hh):
        m_new = jnp.max(st, axis=0, keepdims=True)
        p = jnp.exp2(st - m_new).astype(BF16)
        acc_ref[hh] = _dot(vc_ref[0, head(hh)], p)
        m_ref[hh] = m_new

    def pair(nxt, cur, slot):
        for hh in range(2):
            qk(nxt, 1 - slot, hh)
            process(cur, slot, hh)

    sts = [scores(kc_ref[0, :, head(hh)], hh) for hh in range(2)]
    if has_latent:
        n = v_ref.shape[1]
        for hh in range(2):
            qk(0, 0, hh)
            ctx_softmax(sts[hh], hh)

        def body(i, carry):
            pair(2 * i + 1, 2 * i, 0)
            pair(2 * i + 2, 2 * i + 1, 1)
            return carry
        lax.fori_loop(0, n // 2 - 1, body, 0)
        pair(n - 1, n - 2, 0)
        for hh in range(2):
            process(n - 1, 1, hh)
    else:
        for hh in range(2):
            ctx_softmax(sts[hh], hh)
    a0, a1 = acc_ref[0], acc_ref[1]
    ot = jnp.concatenate([a0[:MLA_V_DIM] / a0[MLA_V_DIM:MLA_V_DIM + 1],
                          a1[:MLA_V_DIM] / a1[MLA_V_DIM:MLA_V_DIM + 1]], axis=0)
    o_ref[0] = ot.T.astype(BF16)


def _mla_attn(q, kc, vtc, k=None, vt=None):
    bsz, t, _ = q.shape
    nc = kc.shape[1]
    has_latent = k is not None
    tq = _row_tile(t, MLA_TQ)
    in_specs = [pl.BlockSpec((1, tq, 2 * LANES), lambda b, h, i: (b, i, h)),
                pl.BlockSpec((1, nc, 2 * LANES), lambda b, h, i: (b, 0, h)),
                pl.BlockSpec((1, 2 * LANES, nc), lambda b, h, i: (b, h, 0))]
    args = [q, kc, vtc]
    scratch = [pltpu.VMEM((2, 1, tq), F32), pltpu.VMEM((2, LANES, tq), F32), pltpu.VMEM((2, LANES, tq), BF16)]
    if has_latent:
        n = k.shape[1]
        nt, _, tk = vt.shape[1:]
        assert tk == MLA_TK and nt * tk == n and nt % 2 == 0, (n, nt, tk)
        in_specs += [pl.BlockSpec((1, n, 2 * LANES), lambda b, h, i: (b, 0, h)),
                     pl.BlockSpec((1, nt, 2 * LANES, tk), lambda b, h, i: (b, 0, h, 0))]
        args += [k, vt]
        scratch += [pltpu.VMEM((2, 2, tk, tq), F32), pltpu.VMEM((2, 2, 1, tq), F32)]
    return pl.pallas_call(
        functools.partial(_mla_attn_kernel, has_latent=has_latent),
        grid=(bsz, MLA_HEADS // 2, t // tq),
        in_specs=in_specs,
        out_specs=pl.BlockSpec((1, tq, LANES), lambda b, h, i: (b, i, h)),
        out_shape=jax.ShapeDtypeStruct((bsz, t, MLA_HEADS * MLA_V_DIM), BF16),
        scratch_shapes=scratch,
        compiler_params=_params("parallel", "parallel", "arbitrary"),
        name="mla_attn" if has_latent else "mla_attn_ctx",
    )(*args)


def _win_attn_kernel(*refs, has_window, nb):
    if has_window:
        sink_ref, q_ref, kc_ref, vc_ref, k_ref, vt_ref, bias_ref, o_ref, s_ref, mc_ref = refs
    else:
        sink_ref, q_ref, kc_ref, vc_ref, o_ref, s_ref, mc_ref = refs
    qb = q_ref.shape[1] // WIN_BLOCK
    nc = kc_ref.shape[1]
    nct = vc_ref.shape[1]
    i = pl.program_id(1)
    units = [(blk, n) for blk in range(qb) for n in range(WIN_KV_HEADS)]
    cache = {}

    def block_operands(blk):
        if blk in cache:
            return cache[blk]
        vts = [vc_ref[0, t] for t in range(nct)]
        bias = None
        if has_window:
            g = i * qb + blk
            start = jnp.clip(g - 1, 0, nb - 3)
            kwin = k_ref[0, pl.ds(pl.multiple_of(start * WIN_BLOCK, WIN_BLOCK), 3 * WIN_BLOCK), :]
            kall = jnp.concatenate([kc_ref[0], kwin], axis=0)
            vts += [vt_ref[0, start + t] for t in range(3)]
            bias = jnp.concatenate([bias_ref[g - start]] * WIN_GROUP, axis=1)
        else:
            kall = kc_ref[0]
        rows = slice(blk * WIN_BLOCK, (blk + 1) * WIN_BLOCK)
        qs = jnp.concatenate([q_ref[0, rows, j * LANES:(j + 1) * LANES] for j in range(WIN_GROUP)], axis=0)
        cache[blk] = (kall, jnp.concatenate(vts, axis=1), qs, bias)
        return cache[blk]

    def qk(u, slot):
        blk, n = units[u]
        kall, _, qs, bias = block_operands(blk)
        head0 = (_lane(qs.shape) & 32) == 0
        zero = jnp.zeros_like(qs)
        st = _dot_nt(kall, jnp.where(head0, qs, zero) if n == 0 else jnp.where(head0, zero, qs))
        if bias is not None:
            st = jnp.concatenate([st[:nc], st[nc:] + bias], axis=0)
        s_ref[slot] = st
        mc_ref[slot] = jnp.max(st, axis=0, keepdims=True)

    outs = {}

    def process(u, slot):
        blk, n = units[u]
        vall = block_operands(blk)[1]
        sink = sink_ref[n]
        m = jnp.maximum(mc_ref[slot], sink)
        p = jnp.exp2(s_ref[slot] - m).astype(BF16)
        pv = _dot(vall[n * LANES:(n + 1) * LANES], p)
        denom = pv[WIN_HEAD_DIM:WIN_HEAD_DIM + 1] + jnp.exp2(sink - m)
        outs[(blk, n)] = pv[:WIN_HEAD_DIM] / denom
        if n == WIN_KV_HEADS - 1:
            rows = slice(blk * WIN_BLOCK, (blk + 1) * WIN_BLOCK)
            for j in range(WIN_GROUP):
                cols = slice(j * WIN_BLOCK, (j + 1) * WIN_BLOCK)
                both = jnp.concatenate([outs[(blk, 0)][:, cols], outs[(blk, 1)][:, cols]], axis=0)
                o_ref[0, rows, j * LANES:(j + 1) * LANES] = both.T.astype(BF16)

    qk(0, 0)
    for u in range(len(units)):
        if u + 1 < len(units):
            qk(u + 1, (u + 1) % 2)
        process(u, u % 2)


def _band_bias():
    krow = np.arange(3 * WIN_BLOCK)[:, None]
    qcol = np.arange(WIN_BLOCK)[None, :]
    keep = [np.abs(krow - (qcol + off * WIN_BLOCK)) <= WINDOW for off in range(3)]
    return jnp.asarray(np.where(np.stack(keep), 0.0, NEG_INF), F32)


def _win_attn(sinkrow, q, kc, vtc, k=None, vt=None):
    bsz, t, _ = q.shape
    nc = kc.shape[1]
    has_window = k is not None
    nb = t // WIN_BLOCK
    tq = _row_tile(t, WIN_TQ)
    kvw = WIN_KV_HEADS * WIN_HEAD_DIM
    nk = nc + (3 * WIN_BLOCK if has_window else 0)
    in_specs = [pl.BlockSpec((WIN_KV_HEADS, 1, WIN_GROUP * WIN_BLOCK), lambda b, i: (0, 0, 0)),
                pl.BlockSpec((1, tq, WIN_HEADS * WIN_HEAD_DIM), lambda b, i: (b, i, 0)),
                pl.BlockSpec((1, nc, kvw), lambda b, i: (b, 0, 0)),
                pl.BlockSpec((1, nc // WIN_BLOCK, 2 * LANES, WIN_BLOCK), lambda b, i: (b, 0, 0, 0))]
    args = [sinkrow, q, kc, vtc]
    if has_window:
        assert nb >= 3, nb
        in_specs += [pl.BlockSpec((1, t, kvw), lambda b, i: (b, 0, 0)),
                     pl.BlockSpec((1, nb, 2 * LANES, WIN_BLOCK), lambda b, i: (b, 0, 0, 0)),
                     pl.BlockSpec((3, 3 * WIN_BLOCK, WIN_BLOCK), lambda b, i: (0, 0, 0))]
        args += [k, vt, _band_bias()]
    return pl.pallas_call(
        functools.partial(_win_attn_kernel, has_window=has_window, nb=nb),
        grid=(bsz, t // tq),
        in_specs=in_specs,
        out_specs=pl.BlockSpec((1, tq, WIN_HEADS * WIN_HEAD_DIM), lambda b, i: (b, i, 0)),
        out_shape=jax.ShapeDtypeStruct((bsz, t, WIN_HEADS * WIN_HEAD_DIM), BF16),
        scratch_shapes=[pltpu.VMEM((2, nk, WIN_GROUP * WIN_BLOCK), F32),
                        pltpu.VMEM((2, 1, WIN_GROUP * WIN_BLOCK), F32)],
        compiler_params=_params("parallel", "parallel"),
        name="win_attn" if has_window else "win_attn_ctx",
    )(*args)


def _log_sigmoid(x):
    return jnp.minimum(x, 0.0) - jnp.log(1.0 + jnp.exp(-jnp.abs(x)))


def _proj1_kernel(x_ref, mod_ref, ng_ref, win_ref, wgk_ref, bgk_ref, k_ref, v_ref, q_ref, gt_ref, gf_ref, gb_ref):
    hb, _ = _pre_mod(x_ref[0], mod_ref, ng_ref, 1)
    ql = _dot(hb, win_ref[:, P1_Q:P1_LOW + LANES])
    q_ref[0] = ql[:, :GLA_DK] * (GLA_DK_HEAD ** -0.5)
    low = ql[:, GLA_DK:].astype(BF16)
    k_ref[0] = _dot(hb, win_ref[:, P1_K:P1_K + GLA_DK])
    pre_f = _dot(low, wgk_ref[:, :GLA_DK]) + bgk_ref[:, :GLA_DK]
    gf_ref[0] = _log_sigmoid(pre_f) * (1.0 / GLA_GATE_NORM)
    v_ref[0] = _dot(hb, win_ref[:, P1_V:P1_V + GLA_DV]).astype(BF16)
    pre_b = _dot(low, wgk_ref[:, GLA_DK:]) + bgk_ref[:, GLA_DK:]
    gb_ref[0] = _log_sigmoid(pre_b) * (1.0 / GLA_GATE_NORM)
    gt_ref[0] = _silu(_dot(hb, win_ref[:, P1_G:P1_G + GLA_DV]))


def _proj1(x, mod, ng, win, wgk, bgk):
    bx, t, _ = x.shape
    tm = _row_tile(t)
    outs = ((GLA_DK, F32), (GLA_DV, BF16), (GLA_DK, F32), (GLA_DV, F32), (GLA_DK, F32), (GLA_DK, F32))
    return pl.pallas_call(
        _proj1_kernel,
        grid=(bx, t // tm),
        in_specs=[_tok_spec(tm, D_MODEL), _mod_spec(mod), _resident((3, D_MODEL)), _resident(win.shape),
                  _resident(wgk.shape), _resident(bgk.shape)],
        out_specs=[_tok_spec(tm, w) for w, _ in outs],
        out_shape=[jax.ShapeDtypeStruct((bx, t, w), d) for w, d in outs],
        compiler_params=_params("parallel", "parallel"),
        name="proj1",
    )(x, mod, ng, win, wgk, bgk)


def _gla_block(q_ref, k_ref, v_ref, g_ref, o_ref, st_ref, bi, cum, keep, backward):
    t = GLA_BLOCK
    g = g_ref[bi]
    g_hi = g.astype(BF16)
    g_lo = (g - g_hi.astype(F32)).astype(BF16)
    both = _dot(cum, g_hi) + _dot(cum, g_lo)
    bb, tot = both[:t], both[t:]
    k = k_ref[bi]
    q_dec = (q_ref[bi] * jnp.exp(bb)).astype(BF16)
    k_inv = (k * jnp.exp(-bb)).astype(BF16)
    k_end = (k * jnp.exp(tot - bb)).astype(BF16)
    decay = jnp.exp(tot)
    n_sub = t // GLA_CHUNK
    order = range(n_sub - 1, -1, -1) if backward else range(n_sub)
    for h in range(GLA_HEADS):
        ks = slice(h * GLA_DK_HEAD, (h + 1) * GLA_DK_HEAD)
        vs = slice(h * GLA_DV_HEAD, (h + 1) * GLA_DV_HEAD)
        vh = v_ref[bi, :, vs]
        a = jnp.where(keep, _dot_nt(q_dec[:, ks], k_inv[:, ks]), 0.0).astype(BF16)
        o_intra = _dot(a, vh)
        st = st_ref[bi, h]
        for c in order:
            rows = slice(c * GLA_CHUNK, (c + 1) * GLA_CHUNK)
            o_ref[bi, rows, vs] = o_intra[rows] + _dot_nt(q_dec[rows, ks], st.astype(BF16))
            st = st * decay[c * GLA_CHUNK:c * GLA_CHUNK + 1, ks] + _dot_tn(vh[rows], k_end[rows, ks])
        st_ref[bi, h] = st


def _gla_scan_kernel(qf_ref, kf_ref, vf_ref, gf_ref, qb_ref, kb_ref, vb_ref, gb_ref, s0f_ref, s0b_ref, cum_ref,
                     of_ref, ob_ref, sf_ref, sb_ref, stf_ref, stb_ref):
    i = pl.program_id(1)

    @pl.when(i == 0)
    def _():
        stf_ref[...] = s0f_ref[...]
        stb_ref[...] = s0b_ref[...]

    row = lax.broadcasted_iota(jnp.int32, (GLA_BLOCK, GLA_BLOCK), 0)
    col = lax.broadcasted_iota(jnp.int32, (GLA_BLOCK, GLA_BLOCK), 1)
    same = (row // GLA_CHUNK) == (col // GLA_CHUNK)
    for bi in range(qf_ref.shape[0]):
        _gla_block(qf_ref, kf_ref, vf_ref, gf_ref, of_ref, stf_ref, bi, cum_ref[0], same & (col <= row), False)
        _gla_block(qb_ref, kb_ref, vb_ref, gb_ref, ob_ref, stb_ref, bi, cum_ref[1], same & (col >= row), True)

    @pl.when(i == pl.num_programs(1) - 1)
    def _():
        sf_ref[...] = stf_ref[...]
        sb_ref[...] = stb_ref[...]


def _cum_matrices():
    r = np.arange(GLA_BLOCK)[:, None]
    c = np.arange(GLA_BLOCK)[None, :]
    same = (r // GLA_CHUNK) == (c // GLA_CHUNK)
    fwd = np.concatenate([same & (c <= r), same], axis=0)
    bwd = np.concatenate([same & (c >= r), same], axis=0)
    return jnp.asarray(np.stack([fwd, bwd]), BF16)


def _gla_scan(q, k, v, gf, gb, s0f, s0b):
    bsz, t, _ = q.shape
    tb = GLA_BLOCK
    assert t % tb == 0, t
    nblk = t // tb
    nbat = 2 if bsz % 2 == 0 else 1
    fwd = lambda w: pl.BlockSpec((nbat, tb, w), lambda b, i: (b, i, 0))
    bwd = lambda w: pl.BlockSpec((nbat, tb, w), lambda b, i: (b, nblk - 1 - i, 0))
    st_spec = pl.BlockSpec((nbat, GLA_HEADS, GLA_DV_HEAD, GLA_DK_HEAD), lambda b, i: (b, 0, 0, 0))
    st_shape = jax.ShapeDtypeStruct((bsz, GLA_HEADS, GLA_DV_HEAD, GLA_DK_HEAD), F32)
    cum = _cum_matrices()
    return pl.pallas_call(
        _gla_scan_kernel,
        grid=(bsz // nbat, nblk),
        in_specs=[fwd(GLA_DK), fwd(GLA_DK), fwd(GLA_DV), fwd(GLA_DK),
                  bwd(GLA_DK), bwd(GLA_DK), bwd(GLA_DV), bwd(GLA_DK), st_spec, st_spec, _resident(cum.shape)],
        out_specs=[fwd(GLA_DV), bwd(GLA_DV), st_spec, st_spec],
        out_shape=[jax.ShapeDtypeStruct((bsz, t, GLA_DV), F32)] * 2 + [st_shape] * 2,
        scratch_shapes=[pltpu.VMEM((nbat, GLA_HEADS, GLA_DV_HEAD, GLA_DK_HEAD), F32)] * 2,
        compiler_params=_params("parallel", "arbitrary"),
        name="gla_scan",
    )(q, k, v, gf, q, k, v, gb, s0f, s0b, cum)


def _deinterleave(n):
    return np.concatenate([np.arange(0, n, 2), np.arange(1, n, 2)])


def _gather_cols(w, idx):
    idx = np.asarray(idx)
    cols = jnp.take(w, jnp.asarray(np.maximum(idx, 0)), axis=-1)
    return jnp.where(jnp.asarray(idx >= 0), cols, 0.0)


def _mla_slot(nope, rope):
    pad = np.full((16,), -1, np.int64)
    return np.concatenate([nope[:32], rope[:16], pad, nope[32:], rope[16:], pad])


def _win_slot(a, b):
    return np.concatenate([a[:32], b[:32], a[32:], b[32:]])


def _take(vec, idx):
    return jnp.where(jnp.asarray(idx >= 0), vec[jnp.asarray(np.maximum(idx, 0))], 0.0)


def _layer0_layout(p):
    ckv0, kr0 = 0, MLA_KV_RANK
    wk0 = kr0 + MLA_ROPE_DIM
    wv0 = wk0 + WIN_KV_HEADS * WIN_HEAD_DIM
    cq0 = wv0 + WIN_KV_HEADS * WIN_HEAD_DIM
    wq0 = cq0 + MLA_Q_RANK
    de64, de32 = _deinterleave(WIN_HEAD_DIM), _deinterleave(MLA_ROPE_DIM)
    none64, none32 = np.full((64,), -1, np.int64), np.full((32,), -1, np.int64)

    idx = np.full((P0_COLS,), -1, np.int64)
    idx[P0_CKV:P0_CKV + MLA_KV_RANK] = ckv0 + np.arange(MLA_KV_RANK)
    idx[P0_WK:P0_WK + LANES] = _win_slot(wk0 + de64, wk0 + 64 + de64)
    idx[P0_CQ:P0_CQ + MLA_Q_RANK] = cq0 + np.arange(MLA_Q_RANK)
    for j in range(WIN_GROUP):
        idx[P0_WQ + j * LANES:P0_WQ + (j + 1) * LANES] = _win_slot(wq0 + j * 64 + de64,
                                                                   wq0 + (WIN_GROUP + j) * 64 + de64)
    idx[P0_KR:P0_KR + LANES] = _mla_slot(none64, kr0 + de32)
    w_in = _gather_cols(p["w_in"], idx).astype(BF16)
    vidx = np.full((WIN_KV_HEADS * LANES,), -1, np.int64)
    for n in range(WIN_KV_HEADS):
        vidx[n * LANES:n * LANES + 64] = wv0 + n * 64 + np.arange(64)
    w_wvt = _gather_cols(p["w_in"], vidx).T.astype(BF16)

    per = MLA_NOPE_DIM + MLA_V_DIM
    kidx, qidx, vidx = [], [], []
    for h in range(MLA_HEADS):
        kidx.append(_mla_slot(h * per + np.arange(64), none32))
        qidx.append(_mla_slot(h * MLA_QK_DIM + np.arange(64), h * MLA_QK_DIM + MLA_NOPE_DIM + de32))
        vidx.append(np.concatenate([h * per + MLA_NOPE_DIM + np.arange(64), none64]))
    w_uk = _gather_cols(p["mla_w_ukv"], np.concatenate(kidx)).astype(BF16)
    w_vt = _gather_cols(p["mla_w_ukv"], np.concatenate(vidx)).T.astype(BF16)
    w_uq = _gather_cols(p["mla_w_uq"], np.concatenate(qidx)).astype(BF16)

    lane = np.arange(LANES)
    nope, rot = (lane & 32) == 0, (lane & 48) == 32
    seg_a = (nope[:, None] & nope[None, :]) | (rot[:, None] & rot[None, :])
    seg_b = nope[:, None] == nope[None, :]
    two = lambda m: np.kron(np.eye(2), m.astype(np.float32))
    seg = jnp.asarray(np.stack([two(seg_a), two(seg_b)]), BF16)

    a64, a32 = np.arange(64), np.arange(32)
    twice = lambda v: jnp.concatenate([v, v])
    wide = lambda v: jnp.concatenate([v, jnp.zeros((LANES,), F32)])
    g_wq = _take(p["win_g_q"], _win_slot(de64, de64)) * (WIN_SCALE * LOG2E)
    g_wk = _take(p["win_g_k"], _win_slot(de64, de64))
    g_qslot = jnp.concatenate([p["mla_g_qn"], p["mla_g_qr"]])
    inv_n = jnp.concatenate([jnp.full((64,), 1.0 / MLA_NOPE_DIM, F32), jnp.full((32,), 1.0 / MLA_ROPE_DIM, F32)])
    gv = jnp.stack([
        wide(p["mla_g_kva"]),
        p["mla_g_qa"],
        twice(_take(p["mla_g_kn"], _mla_slot(a64, none32))),
        wide(_take(p["mla_g_kr"], _mla_slot(none64, de32))),
        twice(_take(g_qslot, _mla_slot(a64, 64 + de32))) * (MLA_SCALE * LOG2E),
        twice(g_wq),
        wide(g_wk),
        twice(_take(inv_n, _mla_slot(a64, 64 + a32))),
    ])

    out_rows = np.arange(D_MODEL)
    base = MLA_HEADS * MLA_V_DIM
    for j in range(WIN_GROUP):
        for n in range(WIN_KV_HEADS):
            lo = base + j * LANES + n * 64
            out_rows[lo:lo + 64] = base + (n * WIN_GROUP + j) * 64 + np.arange(64)
    w_out = p["w_out"][jnp.asarray(out_rows)].astype(BF16)
    return (w_in, w_uk, w_vt, w_uq, w_wvt, seg, gv), w_out


def _layer1_layout(p):
    k0, v0 = 0, GLA_DK
    lf0 = v0 + GLA_DV
    lb0 = lf0 + GLA_LOWRANK
    q0 = lb0 + GLA_LOWRANK
    g0 = q0 + GLA_DK
    idx = np.full((P1_COLS,), -1, np.int64)
    idx[P1_K:P1_K + GLA_DK] = k0 + np.arange(GLA_DK)
    idx[P1_V:P1_V + GLA_DV] = v0 + np.arange(GLA_DV)
    idx[P1_Q:P1_Q + GLA_DK] = q0 + np.arange(GLA_DK)
    idx[P1_G:P1_G + GLA_DV] = g0 + np.arange(GLA_DV)
    idx[P1_LOW:P1_LOW + GLA_LOWRANK] = lf0 + np.arange(GLA_LOWRANK)
    idx[P1_LOW + GLA_LOWRANK:P1_LOW + 2 * GLA_LOWRANK] = lb0 + np.arange(GLA_LOWRANK)
    w_in = _gather_cols(p["w_in"], idx).astype(BF16)
    wgk = jnp.zeros((LANES, 2 * GLA_DK), F32)
    wgk = wgk.at[:GLA_LOWRANK, :GLA_DK].set(p["w_gk_f"])
    wgk = wgk.at[GLA_LOWRANK:2 * GLA_LOWRANK, GLA_DK:].set(p["w_gk_b"])
    bgk = jnp.concatenate([p["b_gk_f"], p["b_gk_b"]]).reshape(1, 2 * GLA_DK)
    return w_in, wgk.astype(BF16), bgk


def _rope_tables(seq):
    t = np.arange(seq)
    row = (t // GRID_W).astype(np.float32)
    col = (t % GRID_W).astype(np.float32)

    def angles(rot_dim):
        n_freq = rot_dim // 4
        inv = (ROPE_BASE ** (-np.arange(n_freq, dtype=np.float32) / n_freq)).astype(np.float32)
        return np.concatenate([row[:, None] * inv, col[:, None] * inv], axis=-1)

    am, aw = angles(MLA_ROPE_DIM), angles(WIN_HEAD_DIM)
    cos_m, sin_m, cos_w, sin_w = np.cos(am), np.sin(am), np.cos(aw), np.sin(aw)
    one32, one16 = np.ones((seq, 32)), np.ones((seq, 16))
    cm = np.concatenate([one32, cos_m, one16] * 2, axis=-1)
    sm = np.concatenate([0 * one32, -sin_m, 0 * one16, 0 * one32, sin_m, 0 * one16], axis=-1)
    cw = np.concatenate([cos_w] * 4, axis=-1)
    sw = np.concatenate([-sin_w, -sin_w, sin_w, sin_w], axis=-1)
    return tuple(jnp.asarray(a, F32) for a in (cm, sm, cw, sw))


def _identity_tables(rows):
    one, zero = jnp.ones((rows, LANES), F32), jnp.zeros((rows, LANES), F32)
    return one, zero, one, zero


def kernel(x, c, ctx, c_ctx, l0_norm_g, l0_w_mod, l0_b_mod, l0_ffn1_w_gu, l0_ffn1_w_down, l0_ffn2_w_gu, l0_ffn2_w_down, l0_w_in, l0_mla_g_qa, l0_mla_g_kva, l0_mla_w_uq, l0_mla_w_ukv, l0_mla_g_qn, l0_mla_g_qr, l0_mla_g_kn, l0_mla_g_kr, l0_win_g_q, l0_win_g_k, l0_win_sink, l0_w_out, l1_norm_g, l1_w_mod, l1_b_mod, l1_ffn1_w_gu, l1_ffn1_w_down, l1_ffn2_w_gu, l1_ffn2_w_down, l1_w_in, l1_w_gk_f, l1_b_gk_f, l1_w_gk_b, l1_b_gk_b, l1_g_norm, l1_w_out):
    bsz, seq, _ = x.shape
    n_ctx = ctx.shape[1]
    out_dtype = x.dtype
    x = x.astype(F32)
    xc = ctx.astype(F32)

    cc = jnp.zeros((16, D_MODEL), F32).at[:bsz].set(c).at[bsz].set(c_ctx)

    def mods(w_mod, b_mod):
        tab = _mod_table(cc, w_mod, b_mod).reshape(16, N_MOD, D_MODEL)
        return tab[:bsz], tab[bsz:bsz + 1]

    p0 = dict(w_in=l0_w_in, mla_g_qa=l0_mla_g_qa, mla_g_kva=l0_mla_g_kva, mla_w_uq=l0_mla_w_uq,
              mla_w_ukv=l0_mla_w_ukv, mla_g_qn=l0_mla_g_qn, mla_g_qr=l0_mla_g_qr, mla_g_kn=l0_mla_g_kn,
              mla_g_kr=l0_mla_g_kr, win_g_q=l0_win_g_q, win_g_k=l0_win_g_k, w_out=l0_w_out)
    proj_w, w_out0 = _layer0_layout(p0)
    mod, mod_c = mods(l0_w_mod, l0_b_mod)
    wgu1, wd1 = l0_ffn1_w_gu.astype(BF16), l0_ffn1_w_down.astype(BF16)
    wgu2, wd2 = l0_ffn2_w_gu.astype(BF16), l0_ffn2_w_down.astype(BF16)
    sinkrow = jnp.repeat(l0_win_sink.astype(F32) * LOG2E, WIN_BLOCK, axis=1).reshape(WIN_KV_HEADS, 1, -1)

    x = _ffn(x, mod, l0_norm_g, wgu1, wd1, 0)
    xc = _ffn(xc, mod_c, l0_norm_g, wgu1, wd1, 0)
    qa, ka, vat, qw, kw, vwt = _proj0(x, mod, l0_norm_g, *proj_w, _rope_tables(seq))
    qa_c, ka_c, vat_c, qw_c, kw_c, vwt_c = _proj0(xc, mod_c, l0_norm_g, *proj_w, _identity_tables(n_ctx))
    vat_c = vat_c.reshape(bsz, MLA_HEADS * LANES, n_ctx)
    oa = _mla_attn(qa, ka_c, vat_c, ka, vat)
    ow = _win_attn(sinkrow, qw, kw_c, vwt_c, kw, vwt)
    oa_c = _mla_attn(qa_c, ka_c, vat_c)
    ow_c = _win_attn(sinkrow, qw_c, kw_c, vwt_c)
    x = _attn_out_ffn(x, oa, ow, mod, l0_norm_g, w_out0, wgu2, wd2)
    xc = _attn_out_ffn(xc, oa_c, ow_c, mod_c, l0_norm_g, w_out0, wgu2, wd2)

    w_in1, wgk, bgk = _layer1_layout(dict(w_in=l1_w_in, w_gk_f=l1_w_gk_f, b_gk_f=l1_b_gk_f,
                                          w_gk_b=l1_w_gk_b, b_gk_b=l1_b_gk_b))
    mod, mod_c = mods(l1_w_mod, l1_b_mod)
    wgu1, wd1 = l1_ffn1_w_gu.astype(BF16), l1_ffn1_w_down.astype(BF16)
    wgu2, wd2 = l1_ffn2_w_gu.astype(BF16), l1_ffn2_w_down.astype(BF16)

    x = _ffn(x, mod, l1_norm_g, wgu1, wd1, 0)
    xc = _ffn(xc, mod_c, l1_norm_g, wgu1, wd1, 0)
    k1, v1, q1, gt1, gf1, gb1 = _proj1(x, mod, l1_norm_g, w_in1, wgk, bgk)
    k1c, v1c, q1c, _, gf1c, gb1c = _proj1(xc, mod_c, l1_norm_g, w_in1, wgk, bgk)
    zero_state = jnp.zeros((bsz, GLA_HEADS, GLA_DV_HEAD, GLA_DK_HEAD), F32)
    _, _, s_f, s_b = _gla_scan(q1c, k1c, v1c, gf1c, gb1c, zero_state, zero_state)
    o_f, o_b, _, _ = _gla_scan(q1, k1, v1, gf1, gb1, s_f, s_b)
    x = _gla_out_ffn(x, o_f, o_b, gt1, l1_g_norm.reshape(1, GLA_DV_HEAD), mod, l1_norm_g,
                     l1_w_out.astype(BF16), wgu2, wd2)
    return x.astype(out_dtype)
```

```python
import functools

import numpy as np
import jax
import jax.numpy as jnp
from jax import lax
from jax.experimental import pallas as pl
from jax.experimental.pallas import tpu as pltpu

F32 = jnp.float32
BF16 = jnp.bfloat16

D_MODEL = 1024
GRID_W = 64
N_MOD = 9
EPS = 1e-6
ROPE_BASE = 10000.0
NEG_INF = -1e30
D_FF = 2816
LOG2E = 1.4426950408889634

MLA_HEADS = 8
MLA_Q_RANK = 256
MLA_KV_RANK = 128
MLA_NOPE_DIM = 64
MLA_ROPE_DIM = 32
MLA_V_DIM = 64
MLA_QK_DIM = MLA_NOPE_DIM + MLA_ROPE_DIM
MLA_SCALE = MLA_QK_DIM ** -0.5

WIN_HEADS = 8
WIN_KV_HEADS = 2
WIN_GROUP = WIN_HEADS // WIN_KV_HEADS
WIN_HEAD_DIM = 64
WINDOW = 128
WIN_BLOCK = 128
WIN_SCALE = WIN_HEAD_DIM ** -0.5

GLA_HEADS = 4
GLA_DK = D_MODEL // 2
GLA_DV = D_MODEL
GLA_DK_HEAD = GLA_DK // GLA_HEADS
GLA_DV_HEAD = GLA_DV // GLA_HEADS
GLA_LOWRANK = 16
GLA_GATE_NORM = 16.0
GLA_CHUNK = 64
GLA_BLOCK = 256

LANES = 128
BF16_SUBLANES = 16
VT_ROWS = -(-(MLA_V_DIM + 1) // BF16_SUBLANES) * BF16_SUBLANES
FF_CHUNK = 256
FFN_TM = 512
MLA_TQ = 1024
MLA_TK = 512
WIN_TQ = 1024
VMEM_LIMIT = 58 * 1024 * 1024

P0_CKV, P0_KR, P0_CQ, P0_WQ, P0_WK, P0_COLS = 0, 128, 256, 512, 1024, 1152
P1_K, P1_V, P1_Q, P1_LOW, P1_G, P1_COLS = 0, 512, 1536, 2048, 2176, 3200


def _dot(a, b):
    return jnp.dot(a, b, preferred_element_type=F32)


def _dot_nt(a, b):
    return lax.dot_general(a, b, (((1,), (1,)), ((), ())), preferred_element_type=F32)


def _dot_tn(a, b):
    return lax.dot_general(a, b, (((0,), (0,)), ((), ())), preferred_element_type=F32)


def _params(*sem):
    return pltpu.CompilerParams(dimension_semantics=sem, vmem_limit_bytes=VMEM_LIMIT)


def _resident(shape):
    nd = len(shape)
    return pl.BlockSpec(shape, lambda *_: (0,) * nd, pipeline_mode=pl.Buffered(1))


def _silu(x):
    return x / (1.0 + jnp.exp(-x))


def _pre_mod(x, mod_ref, ng_ref, sub):
    shift = mod_ref[0, 3 * sub:3 * sub + 1, :]
    scale = mod_ref[0, 3 * sub + 1:3 * sub + 2, :]
    gate = mod_ref[0, 3 * sub + 2:3 * sub + 3, :]
    rinv = lax.rsqrt(jnp.mean(x * x, axis=-1, keepdims=True) + EPS)
    h = x * rinv * ng_ref[sub:sub + 1, :] * (1.0 + scale) + shift
    return h.astype(BF16), gate


def _swiglu(hb, wgu_ref, wd_ref, act_ref):
    for c in range(D_FF // FF_CHUNK):
        lo = c * FF_CHUNK
        g = _dot(hb, wgu_ref[:, lo:lo + FF_CHUNK])
        u = _dot(hb, wgu_ref[:, D_FF + lo:D_FF + lo + FF_CHUNK])
        act_ref[:, lo:lo + FF_CHUNK] = (_silu(g) * u).astype(BF16)
    return _dot(act_ref[...], wd_ref[...])


def _mod_kernel(c_ref, w_ref, b_ref, o_ref):
    a = _silu(c_ref[...]).astype(BF16)
    o_ref[...] = _dot(a, w_ref[...].astype(BF16)) + b_ref[...]


def _mod_table(cc, w_mod, b_mod):
    rows = cc.shape[0]
    n = w_mod.shape[1]
    bn = 1152
    return pl.pallas_call(
        _mod_kernel,
        grid=(n // bn,),
        in_specs=[pl.BlockSpec((rows, D_MODEL), lambda j: (0, 0)),
                  pl.BlockSpec((D_MODEL, bn), lambda j: (0, j)),
                  pl.BlockSpec((1, bn), lambda j: (0, j))],
        out_specs=pl.BlockSpec((rows, bn), lambda j: (0, j)),
        out_shape=jax.ShapeDtypeStruct((rows, n), F32),
        compiler_params=_params("arbitrary"),
        name="mod_table",
    )(cc, w_mod, b_mod.reshape(1, n))


def _ffn_kernel(x_ref, mod_ref, ng_ref, wgu_ref, wd_ref, o_ref, act_ref, *, sub):
    x = x_ref[0]
    hb, gate = _pre_mod(x, mod_ref, ng_ref, sub)
    o_ref[0] = x + 0.5 * gate * _swiglu(hb, wgu_ref, wd_ref, act_ref)


def _attn_out_ffn_kernel(x_ref, oa_ref, ob_ref, mod_ref, ng_ref, wo_ref, wgu_ref, wd_ref, o_ref, act_ref):
    half = oa_ref.shape[-1]
    y = _dot(oa_ref[0], wo_ref[:half, :]) + _dot(ob_ref[0], wo_ref[half:, :])
    x = x_ref[0] + mod_ref[0, 5:6, :] * y
    hb, gate = _pre_mod(x, mod_ref, ng_ref, 2)
    o_ref[0] = x + 0.5 * gate * _swiglu(hb, wgu_ref, wd_ref, act_ref)


def _gla_out_ffn_kernel(x_ref, of_ref, ob_ref, gt_ref, gn_ref, mod_ref, ng_ref, wo_ref, wgu_ref, wd_ref,
                        o_ref, act_ref, on_ref):
    for h in range(GLA_HEADS):
        hs = slice(h * GLA_DV_HEAD, (h + 1) * GLA_DV_HEAD)
        o = of_ref[0, :, hs] + ob_ref[0, :, hs]
        rinv = lax.rsqrt(jnp.mean(o * o, axis=-1, keepdims=True) + EPS)
        on_ref[:, hs] = (o * rinv * gn_ref[...] * gt_ref[0, :, hs]).astype(BF16)
    x = x_ref[0] + mod_ref[0, 5:6, :] * _dot(on_ref[...], wo_ref[...])
    hb, gate = _pre_mod(x, mod_ref, ng_ref, 2)
    o_ref[0] = x + 0.5 * gate * _swiglu(hb, wgu_ref, wd_ref, act_ref)


def _tok_spec(tm, width):
    return pl.BlockSpec((1, tm, width), lambda b, i: (b, i, 0))


def _mod_spec(mod):
    if mod.shape[0] == 1:
        return pl.BlockSpec((1, N_MOD, D_MODEL), lambda b, i: (0, 0, 0))
    return pl.BlockSpec((1, N_MOD, D_MODEL), lambda b, i: (b, 0, 0))


def _row_tile(t, pref=512):
    return pref if t % pref == 0 else t


def _ffn(x, mod, ng, wgu, wd, sub):
    bx, t, _ = x.shape
    tm = _row_tile(t, FFN_TM)
    return pl.pallas_call(
        functools.partial(_ffn_kernel, sub=sub),
        grid=(bx, t // tm),
        in_specs=[_tok_spec(tm, D_MODEL), _mod_spec(mod), _resident((3, D_MODEL)),
                  _resident(wgu.shape), _resident(wd.shape)],
        out_specs=_tok_spec(tm, D_MODEL),
        out_shape=jax.ShapeDtypeStruct(x.shape, F32),
        scratch_shapes=[pltpu.VMEM((tm, D_FF), BF16)],
        compiler_params=_params("parallel", "parallel"),
        name="ffn",
    )(x, mod, ng, wgu, wd)


def _attn_out_ffn(x, oa, ob, mod, ng, wo, wgu, wd):
    bx, t, _ = x.shape
    tm = _row_tile(t)
    return pl.pallas_call(
        _attn_out_ffn_kernel,
        grid=(bx, t // tm),
        in_specs=[_tok_spec(tm, D_MODEL), _tok_spec(tm, oa.shape[-1]), _tok_spec(tm, ob.shape[-1]),
                  _mod_spec(mod), _resident((3, D_MODEL)), _resident(wo.shape),
                  _resident(wgu.shape), _resident(wd.shape)],
        out_specs=_tok_spec(tm, D_MODEL),
        out_shape=jax.ShapeDtypeStruct(x.shape, F32),
        scratch_shapes=[pltpu.VMEM((tm, D_FF), BF16)],
        compiler_params=_params("parallel", "parallel"),
        name="attn_out_ffn",
    )(x, oa, ob, mod, ng, wo, wgu, wd)


def _gla_out_ffn(x, of, ob, gt, gn, mod, ng, wo, wgu, wd):
    bx, t, _ = x.shape
    tm = _row_tile(t)
    return pl.pallas_call(
        _gla_out_ffn_kernel,
        grid=(bx, t // tm),
        in_specs=[_tok_spec(tm, D_MODEL), _tok_spec(tm, GLA_DV), _tok_spec(tm, GLA_DV), _tok_spec(tm, GLA_DV),
                  _resident(gn.shape), _mod_spec(mod), _resident((3, D_MODEL)), _resident(wo.shape),
                  _resident(wgu.shape), _resident(wd.shape)],
        out_specs=_tok_spec(tm, D_MODEL),
        out_shape=jax.ShapeDtypeStruct(x.shape, F32),
        scratch_shapes=[pltpu.VMEM((tm, D_FF), BF16), pltpu.VMEM((tm, GLA_DV), BF16)],
        compiler_params=_params("parallel", "parallel"),
        name="gla_out_ffn",
    )(x, of, ob, gt, gn, mod, ng, wo, wgu, wd)


def _lane(shape):
    return lax.broadcasted_iota(jnp.int32, shape, 1)


def _rope(x, cos, sin):
    return x * cos + pltpu.roll(x, LANES // 2, 1) * sin


def _ones_row(vt, pat_ref):
    pat = pat_ref[:vt.shape[0], :]
    return (vt + jnp.concatenate([pat] * (vt.shape[1] // LANES), axis=1)).astype(BF16)


def _proj0_kernel(x_ref, mod_ref, ng_ref, win_ref, wuk_ref, wvt_ref, wuq_ref, wwvt_ref, seg_ref, gv_ref, pat_ref,
                  cm_ref, sm_ref, cw_ref, sw_ref, qa_ref, ka_ref, va_ref, qw_ref, kw_ref, vw_ref):
    hb, _ = _pre_mod(x_ref[0], mod_ref, ng_ref, 1)
    tm = hb.shape[0]
    z = _dot(hb, win_ref[...])
    zk = z[:, P0_CKV:P0_CQ]
    cm, sm, cw, sw = cm_ref[...], sm_ref[...], cw_ref[...], sw_ref[...]
    g_kva, g_qa, g_kn2 = gv_ref[0:1, :LANES], gv_ref[1:2, :], gv_ref[2:3, :]
    g_kr, g_q2, g_wq2 = gv_ref[3:4, :LANES], gv_ref[4:5, :], gv_ref[5:6, :]
    g_wk, n_q2 = gv_ref[6:7, :LANES], gv_ref[7:8, :]

    def seg_rinv(v, seg, inv_n):
        return lax.rsqrt(_dot((v * v).astype(BF16), seg) * inv_n + EPS)

    ckv = zk[:, :MLA_KV_RANK]
    ckv = (ckv * lax.rsqrt(jnp.mean(ckv * ckv, axis=-1, keepdims=True) + EPS) * g_kva).astype(BF16)
    kv = _dot(ckv, wuk_ref[...])
    kr = zk[:, MLA_KV_RANK:]
    kr = kr * lax.rsqrt(jnp.sum(kr * kr, axis=-1, keepdims=True) * (1.0 / MLA_ROPE_DIM) + EPS) * g_kr
    kr = _rope(kr, cm, sm)
    g_kn = g_kn2[:, :LANES]
    for h in range(MLA_HEADS):
        cols = slice(h * LANES, (h + 1) * LANES)
        kn = kv[:, cols]
        rinv = lax.rsqrt(jnp.sum(kn * kn, axis=-1, keepdims=True) * (1.0 / MLA_NOPE_DIM) + EPS)
        ka_ref[0, :, cols] = (kn * rinv * g_kn + kr).astype(BF16)

    head0 = (_lane((tm, LANES)) & 32) == 0

    def pair_rinv(v):
        sq = v * v
        s0 = jnp.sum(jnp.where(head0, sq, 0.0), axis=-1, keepdims=True) * (1.0 / WIN_HEAD_DIM)
        s1 = jnp.sum(jnp.where(head0, 0.0, sq), axis=-1, keepdims=True) * (1.0 / WIN_HEAD_DIM)
        return jnp.where(head0, lax.rsqrt(s0 + EPS), lax.rsqrt(s1 + EPS))

    zw = z[:, P0_WQ:P0_WK]
    for p in range(WIN_GROUP // 2):
        w2 = zw[:, p * 2 * LANES:(p + 1) * 2 * LANES]
        w2 = w2 * seg_rinv(w2, seg_ref[1], 1.0 / WIN_HEAD_DIM) * g_wq2
        for h in range(2):
            lo = (2 * p + h) * LANES
            qw_ref[0, :, lo:lo + LANES] = _rope(w2[:, h * LANES:(h + 1) * LANES], cw, sw).astype(BF16)
    wk = z[:, P0_WK:P0_COLS]
    kw_ref[0] = _rope(wk * pair_rinv(wk) * g_wk, cw, sw).astype(BF16)

    cq = z[:, P0_CQ:P0_WQ]
    cq = (cq * lax.rsqrt(jnp.mean(cq * cq, axis=-1, keepdims=True) + EPS) * g_qa).astype(BF16)
    q = _dot(cq, wuq_ref[...])
    for p in range(MLA_HEADS // 2):
        q2 = q[:, p * 2 * LANES:(p + 1) * 2 * LANES]
        q2 = q2 * seg_rinv(q2, seg_ref[0], n_q2) * g_q2
        for h in range(2):
            lo = (2 * p + h) * LANES
            qa_ref[0, :, lo:lo + LANES] = _rope(q2[:, h * LANES:(h + 1) * LANES], cm, sm).astype(BF16)

    va_ref[0, 0] = _ones_row(_dot_nt(wvt_ref[...], ckv), pat_ref)
    vwt = _ones_row(_dot_nt(wwvt_ref[...], hb), pat_ref)
    for c in range(tm // WIN_BLOCK):
        vw_ref[0, c] = vwt[:, c * WIN_BLOCK:(c + 1) * WIN_BLOCK]


def _proj0(x, mod, ng, win, wuk, wvt, wuq, wwvt, seg, gv, pat, tabs):
    bx, t, _ = x.shape
    tm = _row_tile(t, MLA_TK)
    ntab = tabs[0].shape[0] // tm
    tab_spec = pl.BlockSpec((tm, LANES), lambda b, i: (i % ntab, 0))
    nwb = tm // WIN_BLOCK
    out_specs = [_tok_spec(tm, MLA_HEADS * LANES), _tok_spec(tm, MLA_HEADS * LANES),
                 pl.BlockSpec((1, 1, MLA_HEADS * VT_ROWS, tm), lambda b, i: (b, i, 0, 0)),
                 _tok_spec(tm, WIN_HEADS * WIN_HEAD_DIM), _tok_spec(tm, LANES),
                 pl.BlockSpec((1, nwb, WIN_KV_HEADS * VT_ROWS, WIN_BLOCK), lambda b, i: (b, i, 0, 0))]
    out_shape = [jax.ShapeDtypeStruct((bx, t, MLA_HEADS * LANES), BF16),
                 jax.ShapeDtypeStruct((bx, t, MLA_HEADS * LANES), BF16),
                 jax.ShapeDtypeStruct((bx, t // tm, MLA_HEADS * VT_ROWS, tm), BF16),
                 jax.ShapeDtypeStruct((bx, t, WIN_HEADS * WIN_HEAD_DIM), BF16),
                 jax.ShapeDtypeStruct((bx, t, LANES), BF16),
                 jax.ShapeDtypeStruct((bx, t // WIN_BLOCK, WIN_KV_HEADS * VT_ROWS, WIN_BLOCK), BF16)]
    return pl.pallas_call(
        _proj0_kernel,
        grid=(bx, t // tm),
        in_specs=[_tok_spec(tm, D_MODEL), _mod_spec(mod), _resident((3, D_MODEL)), _resident(win.shape),
                  _resident(wuk.shape), _resident(wvt.shape), _resident(wuq.shape), _resident(wwvt.shape),
                  _resident(seg.shape), _resident(gv.shape), _resident(pat.shape)] + [tab_spec] * 4,
        out_specs=out_specs,
        out_shape=out_shape,
        compiler_params=_params("parallel", "parallel"),
        name="proj0",
    )(x, mod, ng, win, wuk, wvt, wuq, wwvt, seg, gv, pat, *tabs)


def _mla_attn_kernel(*refs, has_latent):
    if has_latent:
        q_ref, kc_ref, vc_ref, k_ref, v_ref, o_ref, m_ref, acc_ref, qt_ref, s_ref, mc_ref = refs
    else:
        q_ref, kc_ref, vc_ref, o_ref, m_ref, acc_ref, qt_ref = refs

    def head(hh):
        return slice(hh * LANES, (hh + 1) * LANES)

    def vrows(hh):
        return slice(hh * VT_ROWS, (hh + 1) * VT_ROWS)

    for hh in range(2):
        qt_ref[hh] = q_ref[0, :, head(hh)].astype(F32).T.astype(BF16)

    def scores(kblk, hh):
        return _dot(kblk, qt_ref[hh])

    def qk(j, slot, hh):
        st = scores(k_ref[0, pl.ds(pl.multiple_of(j * MLA_TK, MLA_TK), MLA_TK), head(hh)], hh)
        s_ref[slot, hh] = st
        mc_ref[slot, hh] = jnp.max(st, axis=0, keepdims=True)

    def process(j, slot, hh):
        m_prev = m_ref[hh]
        m_new = jnp.maximum(m_prev, mc_ref[slot, hh])
        p = jnp.exp2(s_ref[slot, hh] - m_new).astype(BF16)
        pv = _dot(v_ref[0, j, vrows(hh), :], p)
        acc_ref[hh] = acc_ref[hh] * jnp.exp2(m_prev - m_new) + pv
        m_ref[hh] = m_new

    def ctx_softmax(st, hh):
        m_new = jnp.max(st, axis=0, keepdims=True)
        p = jnp.exp2(st - m_new).astype(BF16)
        acc_ref[hh] = _dot(vc_ref[0, vrows(hh)], p)
        m_ref[hh] = m_new

    def pair(nxt, cur, slot):
        for hh in range(2):
            qk(nxt, 1 - slot, hh)
            process(cur, slot, hh)

    sts = [scores(kc_ref[0, :, head(hh)], hh) for hh in range(2)]
    if has_latent:
        n = v_ref.shape[1]
        for hh in range(2):
            qk(0, 0, hh)
            ctx_softmax(sts[hh], hh)

        def body(i, carry):
            pair(2 * i + 1, 2 * i, 0)
            pair(2 * i + 2, 2 * i + 1, 1)
            return carry
        lax.fori_loop(0, n // 2 - 1, body, 0)
        pair(n - 1, n - 2, 0)
        for hh in range(2):
            process(n - 1, 1, hh)
    else:
        for hh in range(2):
            ctx_softmax(sts[hh], hh)
    a0, a1 = acc_ref[0], acc_ref[1]
    ot = jnp.concatenate([a0[:MLA_V_DIM] / a0[MLA_V_DIM:MLA_V_DIM + 1],
                          a1[:MLA_V_DIM] / a1[MLA_V_DIM:MLA_V_DIM + 1]], axis=0)
    o_ref[0] = ot.T.astype(BF16)


def _mla_attn(q, kc, vtc, k=None, vt=None):
    bsz, t, _ = q.shape
    nc = kc.shape[1]
    has_latent = k is not None
    tq = _row_tile(t, MLA_TQ)
    in_specs = [pl.BlockSpec((1, tq, 2 * LANES), lambda b, h, i: (b, i, h)),
                pl.BlockSpec((1, nc, 2 * LANES), lambda b, h, i: (b, 0, h)),
                pl.BlockSpec((1, 2 * VT_ROWS, nc), lambda b, h, i: (b, h, 0))]
    args = [q, kc, vtc]
    scratch = [pltpu.VMEM((2, 1, tq), F32), pltpu.VMEM((2, VT_ROWS, tq), F32), pltpu.VMEM((2, LANES, tq), BF16)]
    if has_latent:
        n = k.shape[1]
        nt, _, tk = vt.shape[1:]
        assert tk == MLA_TK and nt * tk == n and nt % 2 == 0, (n, nt, tk)
        in_specs += [pl.BlockSpec((1, n, 2 * LANES), lambda b, h, i: (b, 0, h)),
                     pl.BlockSpec((1, nt, 2 * VT_ROWS, tk), lambda b, h, i: (b, 0, h, 0))]
        args += [k, vt]
        scratch += [pltpu.VMEM((2, 2, tk, tq), F32), pltpu.VMEM((2, 2, 1, tq), F32)]
    return pl.pallas_call(
        functools.partial(_mla_attn_kernel, has_latent=has_latent),
        grid=(bsz, MLA_HEADS // 2, t // tq),
        in_specs=in_specs,
        out_specs=pl.BlockSpec((1, tq, LANES), lambda b, h, i: (b, i, h)),
        out_shape=jax.ShapeDtypeStruct((bsz, t, MLA_HEADS * MLA_V_DIM), BF16),
        scratch_shapes=scratch,
        compiler_params=_params("parallel", "parallel", "arbitrary"),
        name="mla_attn" if has_latent else "mla_attn_ctx",
    )(*args)


def _win_attn_kernel(*refs, has_window, nb):
    if has_window:
        sink_ref, q_ref, kc_ref, vc_ref, k_ref, vt_ref, bias_ref, o_ref, s_ref, mc_ref = refs
    else:
        sink_ref, q_ref, kc_ref, vc_ref, o_ref, s_ref, mc_ref = refs
    qb = q_ref.shape[1] // WIN_BLOCK
    nc = kc_ref.shape[1]
    nct = vc_ref.shape[1]
    i = pl.program_id(1)
    units = [(blk, n) for blk in range(qb) for n in range(WIN_KV_HEADS)]
    cache = {}

    def block_operands(blk):
        if blk in cache:
            return cache[blk]
        vts = [vc_ref[0, t] for t in range(nct)]
        bias = None
        if has_window:
            g = i * qb + blk
            start = jnp.clip(g - 1, 0, nb - 3)
            kwin = k_ref[0, pl.ds(pl.multiple_of(start * WIN_BLOCK, WIN_BLOCK), 3 * WIN_BLOCK), :]
            kall = jnp.concatenate([kc_ref[0], kwin], axis=0)
            vts += [vt_ref[0, start + t] for t in range(3)]
            bias = jnp.concatenate([bias_ref[g - start]] * WIN_GROUP, axis=1)
        else:
            kall = kc_ref[0]
        rows = slice(blk * WIN_BLOCK, (blk + 1) * WIN_BLOCK)
        qs = jnp.concatenate([q_ref[0, rows, j * LANES:(j + 1) * LANES] for j in range(WIN_GROUP)], axis=0)
        cache[blk] = (kall, jnp.concatenate(vts, axis=1), qs, bias)
        return cache[blk]

    def qk(u, slot):
        blk, n = units[u]
        kall, _, qs, bias = block_operands(blk)
        head0 = (_lane(qs.shape) & 32) == 0
        zero = jnp.zeros_like(qs)
        st = _dot_nt(kall, jnp.where(head0, qs, zero) if n == 0 else jnp.where(head0, zero, qs))
        if bias is not None:
            st = jnp.concatenate([st[:nc], st[nc:] + bias], axis=0)
        s_ref[slot] = st
        mc_ref[slot] = jnp.max(st, axis=0, keepdims=True)

    outs = {}

    def process(u, slot):
        blk, n = units[u]
        vall = block_operands(blk)[1]
        sink = sink_ref[n]
        m = jnp.maximum(mc_ref[slot], sink)
        p = jnp.exp2(s_ref[slot] - m).astype(BF16)
        pv = _dot(vall[n * VT_ROWS:(n + 1) * VT_ROWS], p)
        denom = pv[WIN_HEAD_DIM:WIN_HEAD_DIM + 1] + jnp.exp2(sink - m)
        outs[(blk, n)] = pv[:WIN_HEAD_DIM] / denom
        if n == WIN_KV_HEADS - 1:
            rows = slice(blk * WIN_BLOCK, (blk + 1) * WIN_BLOCK)
            for j in range(WIN_GROUP):
                cols = slice(j * WIN_BLOCK, (j + 1) * WIN_BLOCK)
                both = jnp.concatenate([outs[(blk, 0)][:, cols], outs[(blk, 1)][:, cols]], axis=0)
                o_ref[0, rows, j * LANES:(j + 1) * LANES] = both.T.astype(BF16)

    qk(0, 0)
    for u in range(len(units)):
        if u + 1 < len(units):
            qk(u + 1, (u + 1) % 2)
        process(u, u % 2)


def _band_bias():
    krow = np.arange(3 * WIN_BLOCK)[:, None]
    qcol = np.arange(WIN_BLOCK)[None, :]
    keep = [np.abs(krow - (qcol + off * WIN_BLOCK)) <= WINDOW for off in range(3)]
    return jnp.asarray(np.where(np.stack(keep), 0.0, NEG_INF), F32)


def _win_attn(sinkrow, q, kc, vtc, k=None, vt=None):
    bsz, t, _ = q.shape
    nc = kc.shape[1]
    has_window = k is not None
    nb = t // WIN_BLOCK
    tq = _row_tile(t, WIN_TQ)
    kvw = WIN_KV_HEADS * WIN_HEAD_DIM
    nk = nc + (3 * WIN_BLOCK if has_window else 0)
    in_specs = [pl.BlockSpec((WIN_KV_HEADS, 1, WIN_GROUP * WIN_BLOCK), lambda b, i: (0, 0, 0)),
                pl.BlockSpec((1, tq, WIN_HEADS * WIN_HEAD_DIM), lambda b, i: (b, i, 0)),
                pl.BlockSpec((1, nc, kvw), lambda b, i: (b, 0, 0)),
                pl.BlockSpec((1, nc // WIN_BLOCK, WIN_KV_HEADS * VT_ROWS, WIN_BLOCK), lambda b, i: (b, 0, 0, 0))]
    args = [sinkrow, q, kc, vtc]
    if has_window:
        assert nb >= 3, nb
        in_specs += [pl.BlockSpec((1, t, kvw), lambda b, i: (b, 0, 0)),
                     pl.BlockSpec((1, nb, WIN_KV_HEADS * VT_ROWS, WIN_BLOCK), lambda b, i: (b, 0, 0, 0)),
                     pl.BlockSpec((3, 3 * WIN_BLOCK, WIN_BLOCK), lambda b, i: (0, 0, 0))]
        args += [k, vt, _band_bias()]
    return pl.pallas_call(
        functools.partial(_win_attn_kernel, has_window=has_window, nb=nb),
        grid=(bsz, t // tq),
        in_specs=in_specs,
        out_specs=pl.BlockSpec((1, tq, WIN_HEADS * WIN_HEAD_DIM), lambda b, i: (b, i, 0)),
        out_shape=jax.ShapeDtypeStruct((bsz, t, WIN_HEADS * WIN_HEAD_DIM), BF16),
        scratch_shapes=[pltpu.VMEM((2, nk, WIN_GROUP * WIN_BLOCK), F32),
                        pltpu.VMEM((2, 1, WIN_GROUP * WIN_BLOCK), F32)],
        compiler_params=_params("parallel", "parallel"),
        name="win_attn" if has_window else "win_attn_ctx",
    )(*args)


def _log_sigmoid(x):
    return jnp.minimum(x, 0.0) - jnp.log(1.0 + jnp.exp(-jnp.abs(x)))


def _proj1_kernel(x_ref, mod_ref, ng_ref, win_ref, wgk_ref, bgk_ref, k_ref, v_ref, q_ref, gt_ref, gf_ref, gb_ref):
    hb, _ = _pre_mod(x_ref[0], mod_ref, ng_ref, 1)
    ql = _dot(hb, win_ref[:, P1_Q:P1_LOW + LANES])
    q_ref[0] = ql[:, :GLA_DK] * (GLA_DK_HEAD ** -0.5)
    low = ql[:, GLA_DK:].astype(BF16)
    k_ref[0] = _dot(hb, win_ref[:, P1_K:P1_K + GLA_DK])
    pre_f = _dot(low, wgk_ref[:, :GLA_DK]) + bgk_ref[:, :GLA_DK]
    gf_ref[0] = _log_sigmoid(pre_f) * (1.0 / GLA_GATE_NORM)
    v_ref[0] = _dot(hb, win_ref[:, P1_V:P1_V + GLA_DV]).astype(BF16)
    pre_b = _dot(low, wgk_ref[:, GLA_DK:]) + bgk_ref[:, GLA_DK:]
    gb_ref[0] = _log_sigmoid(pre_b) * (1.0 / GLA_GATE_NORM)
    gt_ref[0] = _silu(_dot(hb, win_ref[:, P1_G:P1_G + GLA_DV]))


def _proj1(x, mod, ng, win, wgk, bgk):
    bx, t, _ = x.shape
    tm = _row_tile(t)
    outs = ((GLA_DK, F32), (GLA_DV, BF16), (GLA_DK, F32), (GLA_DV, F32), (GLA_DK, F32), (GLA_DK, F32))
    return pl.pallas_call(
        _proj1_kernel,
        grid=(bx, t // tm),
        in_specs=[_tok_spec(tm, D_MODEL), _mod_spec(mod), _resident((3, D_MODEL)), _resident(win.shape),
                  _resident(wgk.shape), _resident(bgk.shape)],
        out_specs=[_tok_spec(tm, w) for w, _ in outs],
        out_shape=[jax.ShapeDtypeStruct((bx, t, w), d) for w, d in outs],
        compiler_params=_params("parallel", "parallel"),
        name="proj1",
    )(x, mod, ng, win, wgk, bgk)


def _gla_block(q_ref, k_ref, v_ref, g_ref, o_ref, st_ref, bi, cum, keep, backward):
    t = GLA_BLOCK
    g = g_ref[bi]
    g_hi = g.astype(BF16)
    g_lo = (g - g_hi.astype(F32)).astype(BF16)
    both = _dot(cum, g_hi) + _dot(cum, g_lo)
    bb, tot = both[:t], both[t:]
    k = k_ref[bi]
    q_dec = (q_ref[bi] * jnp.exp(bb)).astype(BF16)
    k_inv = (k * jnp.exp(-bb)).astype(BF16)
    k_end = (k * jnp.exp(tot - bb)).astype(BF16)
    decay = jnp.exp(tot)
    n_sub = t // GLA_CHUNK
    order = range(n_sub - 1, -1, -1) if backward else range(n_sub)
    for h in range(GLA_HEADS):
        ks = slice(h * GLA_DK_HEAD, (h + 1) * GLA_DK_HEAD)
        vs = slice(h * GLA_DV_HEAD, (h + 1) * GLA_DV_HEAD)
        vh = v_ref[bi, :, vs]
        a = jnp.where(keep, _dot_nt(q_dec[:, ks], k_inv[:, ks]), 0.0).astype(BF16)
        o_intra = _dot(a, vh)
        st = st_ref[bi, h]
        for c in order:
            rows = slice(c * GLA_CHUNK, (c + 1) * GLA_CHUNK)
            o_ref[bi, rows, vs] = o_intra[rows] + _dot_nt(q_dec[rows, ks], st.astype(BF16))
            st = st * decay[c * GLA_CHUNK:c * GLA_CHUNK + 1, ks] + _dot_tn(vh[rows], k_end[rows, ks])
        st_ref[bi, h] = st


def _gla_scan_kernel(qf_ref, kf_ref, vf_ref, gf_ref, qb_ref, kb_ref, vb_ref, gb_ref, s0f_ref, s0b_ref, cum_ref,
                     of_ref, ob_ref, sf_ref, sb_ref, stf_ref, stb_ref):
    i = pl.program_id(1)

    @pl.when(i == 0)
    def _():
        stf_ref[...] = s0f_ref[...]
        stb_ref[...] = s0b_ref[...]

    row = lax.broadcasted_iota(jnp.int32, (GLA_BLOCK, GLA_BLOCK), 0)
    col = lax.broadcasted_iota(jnp.int32, (GLA_BLOCK, GLA_BLOCK), 1)
    same = (row // GLA_CHUNK) == (col // GLA_CHUNK)
    for bi in range(qf_ref.shape[0]):
        _gla_block(qf_ref, kf_ref, vf_ref, gf_ref, of_ref, stf_ref, bi, cum_ref[0], same & (col <= row), False)
        _gla_block(qb_ref, kb_ref, vb_ref, gb_ref, ob_ref, stb_ref, bi, cum_ref[1], same & (col >= row), True)

    @pl.when(i == pl.num_programs(1) - 1)
    def _():
        sf_ref[...] = stf_ref[...]
        sb_ref[...] = stb_ref[...]


def _cum_matrices():
    r = np.arange(GLA_BLOCK)[:, None]
    c = np.arange(GLA_BLOCK)[None, :]
    same = (r // GLA_CHUNK) == (c // GLA_CHUNK)
    fwd = np.concatenate([same & (c <= r), same], axis=0)
    bwd = np.concatenate([same & (c >= r), same], axis=0)
    return jnp.asarray(np.stack([fwd, bwd]), BF16)


def _gla_scan(q, k, v, gf, gb, s0f, s0b):
    bsz, t, _ = q.shape
    tb = GLA_BLOCK
    assert t % tb == 0, t
    nblk = t // tb
    nbat = 2 if bsz % 2 == 0 else 1
    fwd = lambda w: pl.BlockSpec((nbat, tb, w), lambda b, i: (b, i, 0))
    bwd = lambda w: pl.BlockSpec((nbat, tb, w), lambda b, i: (b, nblk - 1 - i, 0))
    st_spec = pl.BlockSpec((nbat, GLA_HEADS, GLA_DV_HEAD, GLA_DK_HEAD), lambda b, i: (b, 0, 0, 0))
    st_shape = jax.ShapeDtypeStruct((bsz, GLA_HEADS, GLA_DV_HEAD, GLA_DK_HEAD), F32)
    cum = _cum_matrices()
    return pl.pallas_call(
        _gla_scan_kernel,
        grid=(bsz // nbat, nblk),
        in_specs=[fwd(GLA_DK), fwd(GLA_DK), fwd(GLA_DV), fwd(GLA_DK),
                  bwd(GLA_DK), bwd(GLA_DK), bwd(GLA_DV), bwd(GLA_DK), st_spec, st_spec, _resident(cum.shape)],
        out_specs=[fwd(GLA_DV), bwd(GLA_DV), st_spec, st_spec],
        out_shape=[jax.ShapeDtypeStruct((bsz, t, GLA_DV), F32)] * 2 + [st_shape] * 2,
        scratch_shapes=[pltpu.VMEM((nbat, GLA_HEADS, GLA_DV_HEAD, GLA_DK_HEAD), F32)] * 2,
        compiler_params=_params("parallel", "arbitrary"),
        name="gla_scan",
    )(q, k, v, gf, q, k, v, gb, s0f, s0b, cum)


def _deinterleave(n):
    return np.concatenate([np.arange(0, n, 2), np.arange(1, n, 2)])


def _gather_cols(w, idx):
    idx = np.asarray(idx)
    cols = jnp.take(w, jnp.asarray(np.maximum(idx, 0)), axis=-1)
    return jnp.where(jnp.asarray(idx >= 0), cols, 0.0)


def _mla_slot(nope, rope):
    pad = np.full((16,), -1, np.int64)
    return np.concatenate([nope[:32], rope[:16], pad, nope[32:], rope[16:], pad])


def _win_slot(a, b):
    return np.concatenate([a[:32], b[:32], a[32:], b[32:]])


def _take(vec, idx):
    return jnp.where(jnp.asarray(idx >= 0), vec[jnp.asarray(np.maximum(idx, 0))], 0.0)


def _layer0_layout(p):
    ckv0, kr0 = 0, MLA_KV_RANK
    wk0 = kr0 + MLA_ROPE_DIM
    wv0 = wk0 + WIN_KV_HEADS * WIN_HEAD_DIM
    cq0 = wv0 + WIN_KV_HEADS * WIN_HEAD_DIM
    wq0 = cq0 + MLA_Q_RANK
    de64, de32 = _deinterleave(WIN_HEAD_DIM), _deinterleave(MLA_ROPE_DIM)
    none64, none32 = np.full((64,), -1, np.int64), np.full((32,), -1, np.int64)

    idx = np.full((P0_COLS,), -1, np.int64)
    idx[P0_CKV:P0_CKV + MLA_KV_RANK] = ckv0 + np.arange(MLA_KV_RANK)
    idx[P0_WK:P0_WK + LANES] = _win_slot(wk0 + de64, wk0 + 64 + de64)
    idx[P0_CQ:P0_CQ + MLA_Q_RANK] = cq0 + np.arange(MLA_Q_RANK)
    for j in range(WIN_GROUP):
        idx[P0_WQ + j * LANES:P0_WQ + (j + 1) * LANES] = _win_slot(wq0 + j * 64 + de64,
                                                                   wq0 + (WIN_GROUP + j) * 64 + de64)
    idx[P0_KR:P0_KR + LANES] = _mla_slot(none64, kr0 + de32)
    w_in = _gather_cols(p["w_in"], idx).astype(BF16)
    vpad = np.full((VT_ROWS - MLA_V_DIM,), -1, np.int64)
    vidx = np.concatenate([np.concatenate([wv0 + n * 64 + np.arange(64), vpad]) for n in range(WIN_KV_HEADS)])
    w_wvt = _gather_cols(p["w_in"], vidx).T.astype(BF16)

    per = MLA_NOPE_DIM + MLA_V_DIM
    kidx, qidx, vidx = [], [], []
    for h in range(MLA_HEADS):
        kidx.append(_mla_slot(h * per + np.arange(64), none32))
        qidx.append(_mla_slot(h * MLA_QK_DIM + np.arange(64), h * MLA_QK_DIM + MLA_NOPE_DIM + de32))
        vidx.append(np.concatenate([h * per + MLA_NOPE_DIM + np.arange(64), vpad]))
    w_uk = _gather_cols(p["mla_w_ukv"], np.concatenate(kidx)).astype(BF16)
    w_vt = _gather_cols(p["mla_w_ukv"], np.concatenate(vidx)).T.astype(BF16)
    pat = np.zeros((MLA_HEADS * VT_ROWS, LANES), np.float32)
    pat[MLA_V_DIM::VT_ROWS] = 1.0
    w_uq = _gather_cols(p["mla_w_uq"], np.concatenate(qidx)).astype(BF16)

    lane = np.arange(LANES)
    nope, rot = (lane & 32) == 0, (lane & 48) == 32
    seg_a = (nope[:, None] & nope[None, :]) | (rot[:, None] & rot[None, :])
    seg_b = nope[:, None] == nope[None, :]
    two = lambda m: np.kron(np.eye(2), m.astype(np.float32))
    seg = jnp.asarray(np.stack([two(seg_a), two(seg_b)]), BF16)

    a64, a32 = np.arange(64), np.arange(32)
    twice = lambda v: jnp.concatenate([v, v])
    wide = lambda v: jnp.concatenate([v, jnp.zeros((LANES,), F32)])
    g_wq = _take(p["win_g_q"], _win_slot(de64, de64)) * (WIN_SCALE * LOG2E)
    g_wk = _take(p["win_g_k"], _win_slot(de64, de64))
    g_qslot = jnp.concatenate([p["mla_g_qn"], p["mla_g_qr"]])
    inv_n = jnp.concatenate([jnp.full((64,), 1.0 / MLA_NOPE_DIM, F32), jnp.full((32,), 1.0 / MLA_ROPE_DIM, F32)])
    gv = jnp.stack([
        wide(p["mla_g_kva"]),
        p["mla_g_qa"],
        twice(_take(p["mla_g_kn"], _mla_slot(a64, none32))),
        wide(_take(p["mla_g_kr"], _mla_slot(none64, de32))),
        twice(_take(g_qslot, _mla_slot(a64, 64 + de32))) * (MLA_SCALE * LOG2E),
        twice(g_wq),
        wide(g_wk),
        twice(_take(inv_n, _mla_slot(a64, 64 + a32))),
    ])

    out_rows = np.arange(D_MODEL)
    base = MLA_HEADS * MLA_V_DIM
    for j in range(WIN_GROUP):
        for n in range(WIN_KV_HEADS):
            lo = base + j * LANES + n * 64
            out_rows[lo:lo + 64] = base + (n * WIN_GROUP + j) * 64 + np.arange(64)
    w_out = p["w_out"][jnp.asarray(out_rows)].astype(BF16)
    return (w_in, w_uk, w_vt, w_uq, w_wvt, seg, gv, jnp.asarray(pat)), w_out


def _layer1_layout(p):
    k0, v0 = 0, GLA_DK
    lf0 = v0 + GLA_DV
    lb0 = lf0 + GLA_LOWRANK
    q0 = lb0 + GLA_LOWRANK
    g0 = q0 + GLA_DK
    idx = np.full((P1_COLS,), -1, np.int64)
    idx[P1_K:P1_K + GLA_DK] = k0 + np.arange(GLA_DK)
    idx[P1_V:P1_V + GLA_DV] = v0 + np.arange(GLA_DV)
    idx[P1_Q:P1_Q + GLA_DK] = q0 + np.arange(GLA_DK)
    idx[P1_G:P1_G + GLA_DV] = g0 + np.arange(GLA_DV)
    idx[P1_LOW:P1_LOW + GLA_LOWRANK] = lf0 + np.arange(GLA_LOWRANK)
    idx[P1_LOW + GLA_LOWRANK:P1_LOW + 2 * GLA_LOWRANK] = lb0 + np.arange(GLA_LOWRANK)
    w_in = _gather_cols(p["w_in"], idx).astype(BF16)
    wgk = jnp.zeros((LANES, 2 * GLA_DK), F32)
    wgk = wgk.at[:GLA_LOWRANK, :GLA_DK].set(p["w_gk_f"])
    wgk = wgk.at[GLA_LOWRANK:2 * GLA_LOWRANK, GLA_DK:].set(p["w_gk_b"])
    bgk = jnp.concatenate([p["b_gk_f"], p["b_gk_b"]]).reshape(1, 2 * GLA_DK)
    return w_in, wgk.astype(BF16), bgk


def _rope_tables(seq):
    t = np.arange(seq)
    row = (t // GRID_W).astype(np.float32)
    col = (t % GRID_W).astype(np.float32)

    def angles(rot_dim):
        n_freq = rot_dim // 4
        inv = (ROPE_BASE ** (-np.arange(n_freq, dtype=np.float32) / n_freq)).astype(np.float32)
        return np.concatenate([row[:, None] * inv, col[:, None] * inv], axis=-1)

    am, aw = angles(MLA_ROPE_DIM), angles(WIN_HEAD_DIM)
    cos_m, sin_m, cos_w, sin_w = np.cos(am), np.sin(am), np.cos(aw), np.sin(aw)
    one32, one16 = np.ones((seq, 32)), np.ones((seq, 16))
    cm = np.concatenate([one32, cos_m, one16] * 2, axis=-1)
    sm = np.concatenate([0 * one32, -sin_m, 0 * one16, 0 * one32, sin_m, 0 * one16], axis=-1)
    cw = np.concatenate([cos_w] * 4, axis=-1)
    sw = np.concatenate([-sin_w, -sin_w, sin_w, sin_w], axis=-1)
    return tuple(jnp.asarray(a, F32) for a in (cm, sm, cw, sw))


def _identity_tables(rows):
    one, zero = jnp.ones((rows, LANES), F32), jnp.zeros((rows, LANES), F32)
    return one, zero, one, zero


def kernel(x, c, ctx, c_ctx, l0_norm_g, l0_w_mod, l0_b_mod, l0_ffn1_w_gu, l0_ffn1_w_down, l0_ffn2_w_gu, l0_ffn2_w_down, l0_w_in, l0_mla_g_qa, l0_mla_g_kva, l0_mla_w_uq, l0_mla_w_ukv, l0_mla_g_qn, l0_mla_g_qr, l0_mla_g_kn, l0_mla_g_kr, l0_win_g_q, l0_win_g_k, l0_win_sink, l0_w_out, l1_norm_g, l1_w_mod, l1_b_mod, l1_ffn1_w_gu, l1_ffn1_w_down, l1_ffn2_w_gu, l1_ffn2_w_down, l1_w_in, l1_w_gk_f, l1_b_gk_f, l1_w_gk_b, l1_b_gk_b, l1_g_norm, l1_w_out):
    bsz, seq, _ = x.shape
    n_ctx = ctx.shape[1]
    out_dtype = x.dtype
    x = x.astype(F32)
    xc = ctx.astype(F32)

    cc = jnp.zeros((16, D_MODEL), F32).at[:bsz].set(c).at[bsz].set(c_ctx)

    def mods(w_mod, b_mod):
        tab = _mod_table(cc, w_mod, b_mod).reshape(16, N_MOD, D_MODEL)
        return tab[:bsz], tab[bsz:bsz + 1]

    p0 = dict(w_in=l0_w_in, mla_g_qa=l0_mla_g_qa, mla_g_kva=l0_mla_g_kva, mla_w_uq=l0_mla_w_uq,
              mla_w_ukv=l0_mla_w_ukv, mla_g_qn=l0_mla_g_qn, mla_g_qr=l0_mla_g_qr, mla_g_kn=l0_mla_g_kn,
              mla_g_kr=l0_mla_g_kr, win_g_q=l0_win_g_q, win_g_k=l0_win_g_k, w_out=l0_w_out)
    proj_w, w_out0 = _layer0_layout(p0)
    mod, mod_c = mods(l0_w_mod, l0_b_mod)
    wgu1, wd1 = l0_ffn1_w_gu.astype(BF16), l0_ffn1_w_down.astype(BF16)
    wgu2, wd2 = l0_ffn2_w_gu.astype(BF16), l0_ffn2_w_down.astype(BF16)
    sinkrow = jnp.repeat(l0_win_sink.astype(F32) * LOG2E, WIN_BLOCK, axis=1).reshape(WIN_KV_HEADS, 1, -1)

    x = _ffn(x, mod, l0_norm_g, wgu1, wd1, 0)
    xc = _ffn(xc, mod_c, l0_norm_g, wgu1, wd1, 0)
    qa, ka, vat, qw, kw, vwt = _proj0(x, mod, l0_norm_g, *proj_w, _rope_tables(seq))
    qa_c, ka_c, vat_c, qw_c, kw_c, vwt_c = _proj0(xc, mod_c, l0_norm_g, *proj_w, _identity_tables(n_ctx))
    vat_c = vat_c.reshape(bsz, MLA_HEADS * VT_ROWS, n_ctx)
    oa = _mla_attn(qa, ka_c, vat_c, ka, vat)
    ow = _win_attn(sinkrow, qw, kw_c, vwt_c, kw, vwt)
    oa_c = _mla_attn(qa_c, ka_c, vat_c)
    ow_c = _win_attn(sinkrow, qw_c, kw_c, vwt_c)
    x = _attn_out_ffn(x, oa, ow, mod, l0_norm_g, w_out0, wgu2, wd2)
    xc = _attn_out_ffn(xc, oa_c, ow_c, mod_c, l0_norm_g, w_out0, wgu2, wd2)

    w_in1, wgk, bgk = _layer1_layout(dict(w_in=l1_w_in, w_gk_f=l1_w_gk_f, b_gk_f=l1_b_gk_f,
                                          w_gk_b=l1_w_gk_b, b_gk_b=l1_b_gk_b))
    mod, mod_c = mods(l1_w_mod, l1_b_mod)
    wgu1, wd1 = l1_ffn1_w_gu.astype(BF16), l1_ffn1_w_down.astype(BF16)
    wgu2, wd2 = l1_ffn2_w_gu.astype(BF16), l1_ffn2_w_down.astype(BF16)

    x = _ffn(x, mod, l1_norm_g, wgu1, wd1, 0)
    xc = _ffn(xc, mod_c, l1_norm_g, wgu1, wd1, 0)
    k1, v1, q1, gt1, gf1, gb1 = _proj1(x, mod, l1_norm_g, w_in1, wgk, bgk)
    k1c, v1c, q1c, _, gf1c, gb1c = _proj1(xc, mod_c, l1_norm_g, w_in1, wgk, bgk)
    zero_state = jnp.zeros((bsz, GLA_HEADS, GLA_DV_HEAD, GLA_DK_HEAD), F32)
    _, _, s_f, s_b = _gla_scan(q1c, k1c, v1c, gf1c, gb1c, zero_state, zero_state)
    o_f, o_b, _, _ = _gla_scan(q1, k1, v1, gf1, gb1, s_f, s_b)
    x = _gla_out_ffn(x, o_f, o_b, gt1, l1_g_norm.reshape(1, GLA_DV_HEAD), mod, l1_norm_g,
                     l1_w_out.astype(BF16), wgu2, wd2)
    return x.astype(out_dtype)
```

```python
import functools

import numpy as np
import jax
import jax.numpy as jnp
from jax import lax
from jax.experimental import pallas as pl
from jax.experimental.pallas import tpu as pltpu

F32 = jnp.float32
BF16 = jnp.bfloat16

D_MODEL = 1024
GRID_W = 64
N_MOD = 9
EPS = 1e-6
ROPE_BASE = 10000.0
NEG_INF = -1e30
D_FF = 2816
LOG2E = 1.4426950408889634

MLA_HEADS = 8
MLA_Q_RANK = 256
MLA_KV_RANK = 128
MLA_NOPE_DIM = 64
MLA_ROPE_DIM = 32
MLA_V_DIM = 64
MLA_QK_DIM = MLA_NOPE_DIM + MLA_ROPE_DIM
MLA_SCALE = MLA_QK_DIM ** -0.5

WIN_HEADS = 8
WIN_KV_HEADS = 2
WIN_GROUP = WIN_HEADS // WIN_KV_HEADS
WIN_HEAD_DIM = 64
WINDOW = 128
WIN_BLOCK = 128
WIN_SCALE = WIN_HEAD_DIM ** -0.5

GLA_HEADS = 4
GLA_DK = D_MODEL // 2
GLA_DV = D_MODEL
GLA_DK_HEAD = GLA_DK // GLA_HEADS
GLA_DV_HEAD = GLA_DV // GLA_HEADS
GLA_LOWRANK = 16
GLA_GATE_NORM = 16.0
GLA_CHUNK = 64
GLA_BLOCK = 256

LANES = 128
VT_ROWS = 128
FF_CHUNK = 256
FFN_TM = 512
MLA_TQ = 1024
MLA_TK = 512
WIN_TQ = 1024
VMEM_LIMIT = 58 * 1024 * 1024

P0_CKV, P0_KR, P0_CQ, P0_WQ, P0_WK, P0_COLS = 0, 128, 256, 512, 1024, 1152
P1_K, P1_V, P1_Q, P1_LOW, P1_G, P1_COLS = 0, 512, 1536, 2048, 2176, 3200


def _dot(a, b):
    return jnp.dot(a, b, preferred_element_type=F32)


def _dot_nt(a, b):
    return lax.dot_general(a, b, (((1,), (1,)), ((), ())), preferred_element_type=F32)


def _dot_tn(a, b):
    return lax.dot_general(a, b, (((0,), (0,)), ((), ())), preferred_element_type=F32)


def _params(*sem):
    return pltpu.CompilerParams(dimension_semantics=sem, vmem_limit_bytes=VMEM_LIMIT)


def _resident(shape):
    nd = len(shape)
    return pl.BlockSpec(shape, lambda *_: (0,) * nd, pipeline_mode=pl.Buffered(1))


def _silu(x):
    return x / (1.0 + jnp.exp(-x))


def _pre_mod(x, mod_ref, ng_ref, sub):
    shift = mod_ref[0, 3 * sub:3 * sub + 1, :]
    scale = mod_ref[0, 3 * sub + 1:3 * sub + 2, :]
    gate = mod_ref[0, 3 * sub + 2:3 * sub + 3, :]
    rinv = lax.rsqrt(jnp.mean(x * x, axis=-1, keepdims=True) + EPS)
    h = x * rinv * ng_ref[sub:sub + 1, :] * (1.0 + scale) + shift
    return h.astype(BF16), gate


def _swiglu(hb, wgu_ref, wd_ref, act_ref):
    for c in range(D_FF // FF_CHUNK):
        lo = c * FF_CHUNK
        g = _dot(hb, wgu_ref[:, lo:lo + FF_CHUNK])
        u = _dot(hb, wgu_ref[:, D_FF + lo:D_FF + lo + FF_CHUNK])
        act_ref[:, lo:lo + FF_CHUNK] = (_silu(g) * u).astype(BF16)
    return _dot(act_ref[...], wd_ref[...])


def _mod_kernel(c_ref, w_ref, b_ref, o_ref):
    a = _silu(c_ref[...]).astype(BF16)
    o_ref[...] = _dot(a, w_ref[...].astype(BF16)) + b_ref[...]


def _mod_table(cc, w_mod, b_mod):
    rows = cc.shape[0]
    n = w_mod.shape[1]
    bn = 1152
    return pl.pallas_call(
        _mod_kernel,
        grid=(n // bn,),
        in_specs=[pl.BlockSpec((rows, D_MODEL), lambda j: (0, 0)),
                  pl.BlockSpec((D_MODEL, bn), lambda j: (0, j)),
                  pl.BlockSpec((1, bn), lambda j: (0, j))],
        out_specs=pl.BlockSpec((rows, bn), lambda j: (0, j)),
        out_shape=jax.ShapeDtypeStruct((rows, n), F32),
        compiler_params=_params("arbitrary"),
        name="mod_table",
    )(cc, w_mod, b_mod.reshape(1, n))


def _ffn_kernel(x_ref, mod_ref, ng_ref, wgu_ref, wd_ref, o_ref, act_ref, *, sub):
    x = x_ref[0]
    hb, gate = _pre_mod(x, mod_ref, ng_ref, sub)
    o_ref[0] = x + 0.5 * gate * _swiglu(hb, wgu_ref, wd_ref, act_ref)


def _attn_out_ffn_kernel(x_ref, oa_ref, ob_ref, mod_ref, ng_ref, wo_ref, wgu_ref, wd_ref, o_ref, act_ref):
    half = oa_ref.shape[-1]
    y = _dot(oa_ref[0], wo_ref[:half, :]) + _dot(ob_ref[0], wo_ref[half:, :])
    x = x_ref[0] + mod_ref[0, 5:6, :] * y
    hb, gate = _pre_mod(x, mod_ref, ng_ref, 2)
    o_ref[0] = x + 0.5 * gate * _swiglu(hb, wgu_ref, wd_ref, act_ref)


def _gla_out_ffn_kernel(x_ref, of_ref, ob_ref, gt_ref, gn_ref, mod_ref, ng_ref, wo_ref, wgu_ref, wd_ref,
                        o_ref, act_ref, on_ref):
    for h in range(GLA_HEADS):
        hs = slice(h * GLA_DV_HEAD, (h + 1) * GLA_DV_HEAD)
        o = of_ref[0, :, hs] + ob_ref[0, :, hs]
        rinv = lax.rsqrt(jnp.mean(o * o, axis=-1, keepdims=True) + EPS)
        on_ref[:, hs] = (o * rinv * gn_ref[...] * gt_ref[0, :, hs]).astype(BF16)
    x = x_ref[0] + mod_ref[0, 5:6, :] * _dot(on_ref[...], wo_ref[...])
    hb, gate = _pre_mod(x, mod_ref, ng_ref, 2)
    o_ref[0] = x + 0.5 * gate * _swiglu(hb, wgu_ref, wd_ref, act_ref)


def _tok_spec(tm, width):
    return pl.BlockSpec((1, tm, width), lambda b, i: (b, i, 0))


def _mod_spec(mod):
    if mod.shape[0] == 1:
        return pl.BlockSpec((1, N_MOD, D_MODEL), lambda b, i: (0, 0, 0))
    return pl.BlockSpec((1, N_MOD, D_MODEL), lambda b, i: (b, 0, 0))


def _row_tile(t, pref=512):
    return pref if t % pref == 0 else t


def _ffn(x, mod, ng, wgu, wd, sub):
    bx, t, _ = x.shape
    tm = _row_tile(t, FFN_TM)
    return pl.pallas_call(
        functools.partial(_ffn_kernel, sub=sub),
        grid=(bx, t // tm),
        in_specs=[_tok_spec(tm, D_MODEL), _mod_spec(mod), _resident((3, D_MODEL)),
                  _resident(wgu.shape), _resident(wd.shape)],
        out_specs=_tok_spec(tm, D_MODEL),
        out_shape=jax.ShapeDtypeStruct(x.shape, F32),
        scratch_shapes=[pltpu.VMEM((tm, D_FF), BF16)],
        compiler_params=_params("parallel", "parallel"),
        name="ffn",
    )(x, mod, ng, wgu, wd)


def _attn_out_ffn(x, oa, ob, mod, ng, wo, wgu, wd):
    bx, t, _ = x.shape
    tm = _row_tile(t)
    return pl.pallas_call(
        _attn_out_ffn_kernel,
        grid=(bx, t // tm),
        in_specs=[_tok_spec(tm, D_MODEL), _tok_spec(tm, oa.shape[-1]), _tok_spec(tm, ob.shape[-1]),
                  _mod_spec(mod), _resident((3, D_MODEL)), _resident(wo.shape),
                  _resident(wgu.shape), _resident(wd.shape)],
        out_specs=_tok_spec(tm, D_MODEL),
        out_shape=jax.ShapeDtypeStruct(x.shape, F32),
        scratch_shapes=[pltpu.VMEM((tm, D_FF), BF16)],
        compiler_params=_params("parallel", "parallel"),
        name="attn_out_ffn",
    )(x, oa, ob, mod, ng, wo, wgu, wd)


def _gla_out_ffn(x, of, ob, gt, gn, mod, ng, wo, wgu, wd):
    bx, t, _ = x.shape
    tm = _row_tile(t)
    return pl.pallas_call(
        _gla_out_ffn_kernel,
        grid=(bx, t // tm),
        in_specs=[_tok_spec(tm, D_MODEL), _tok_spec(tm, GLA_DV), _tok_spec(tm, GLA_DV), _tok_spec(tm, GLA_DV),
                  _resident(gn.shape), _mod_spec(mod), _resident((3, D_MODEL)), _resident(wo.shape),
                  _resident(wgu.shape), _resident(wd.shape)],
        out_specs=_tok_spec(tm, D_MODEL),
        out_shape=jax.ShapeDtypeStruct(x.shape, F32),
        scratch_shapes=[pltpu.VMEM((tm, D_FF), BF16), pltpu.VMEM((tm, GLA_DV), BF16)],
        compiler_params=_params("parallel", "parallel"),
        name="gla_out_ffn",
    )(x, of, ob, gt, gn, mod, ng, wo, wgu, wd)


def _lane(shape):
    return lax.broadcasted_iota(jnp.int32, shape, 1)


def _rope(x, cos, sin):
    return x * cos + pltpu.roll(x, LANES // 2, 1) * sin


def _ones_row(vt, pat_ref):
    pat = pat_ref[:vt.shape[0], :]
    return (vt + jnp.concatenate([pat] * (vt.shape[1] // LANES), axis=1)).astype(BF16)


def _proj0_kernel(x_ref, mod_ref, ng_ref, win_ref, wuk_ref, wvt_ref, wuq_ref, wwvt_ref, seg_ref, gv_ref, pat_ref,
                  cm_ref, sm_ref, cw_ref, sw_ref, qa_ref, ka_ref, va_ref, qw_ref, kw_ref, vw_ref):
    hb, _ = _pre_mod(x_ref[0], mod_ref, ng_ref, 1)
    tm = hb.shape[0]
    z = _dot(hb, win_ref[...])
    zk = z[:, P0_CKV:P0_CQ]
    cm, sm, cw, sw = cm_ref[...], sm_ref[...], cw_ref[...], sw_ref[...]
    g_kva, g_qa, g_kn2 = gv_ref[0:1, :LANES], gv_ref[1:2, :], gv_ref[2:3, :]
    g_kr, g_q2, g_wq2 = gv_ref[3:4, :LANES], gv_ref[4:5, :], gv_ref[5:6, :]
    g_wk, n_q2 = gv_ref[6:7, :LANES], gv_ref[7:8, :]

    def seg_rinv(v, seg, inv_n):
        return lax.rsqrt(_dot((v * v).astype(BF16), seg) * inv_n + EPS)

    ckv = zk[:, :MLA_KV_RANK]
    ckv = (ckv * lax.rsqrt(jnp.mean(ckv * ckv, axis=-1, keepdims=True) + EPS) * g_kva).astype(BF16)
    kv = _dot(ckv, wuk_ref[...])
    kr = zk[:, MLA_KV_RANK:]
    kr = kr * lax.rsqrt(jnp.sum(kr * kr, axis=-1, keepdims=True) * (1.0 / MLA_ROPE_DIM) + EPS) * g_kr
    kr = _rope(kr, cm, sm)
    g_kn = g_kn2[:, :LANES]
    for h in range(MLA_HEADS):
        cols = slice(h * LANES, (h + 1) * LANES)
        kn = kv[:, cols]
        rinv = lax.rsqrt(jnp.sum(kn * kn, axis=-1, keepdims=True) * (1.0 / MLA_NOPE_DIM) + EPS)
        ka_ref[0, :, cols] = (kn * rinv * g_kn + kr).astype(BF16)

    head0 = (_lane((tm, LANES)) & 32) == 0

    def pair_rinv(v):
        sq = v * v
        s0 = jnp.sum(jnp.where(head0, sq, 0.0), axis=-1, keepdims=True) * (1.0 / WIN_HEAD_DIM)
        s1 = jnp.sum(jnp.where(head0, 0.0, sq), axis=-1, keepdims=True) * (1.0 / WIN_HEAD_DIM)
        return jnp.where(head0, lax.rsqrt(s0 + EPS), lax.rsqrt(s1 + EPS))

    zw = z[:, P0_WQ:P0_WK]
    for p in range(WIN_GROUP // 2):
        w2 = zw[:, p * 2 * LANES:(p + 1) * 2 * LANES]
        w2 = w2 * seg_rinv(w2, seg_ref[1], 1.0 / WIN_HEAD_DIM) * g_wq2
        for h in range(2):
            lo = (2 * p + h) * LANES
            qw_ref[0, :, lo:lo + LANES] = _rope(w2[:, h * LANES:(h + 1) * LANES], cw, sw).astype(BF16)
    wk = z[:, P0_WK:P0_COLS]
    kw_ref[0] = _rope(wk * pair_rinv(wk) * g_wk, cw, sw).astype(BF16)

    cq = z[:, P0_CQ:P0_WQ]
    cq = (cq * lax.rsqrt(jnp.mean(cq * cq, axis=-1, keepdims=True) + EPS) * g_qa).astype(BF16)
    q = _dot(cq, wuq_ref[...])
    for p in range(MLA_HEADS // 2):
        q2 = q[:, p * 2 * LANES:(p + 1) * 2 * LANES]
        q2 = q2 * seg_rinv(q2, seg_ref[0], n_q2) * g_q2
        for h in range(2):
            lo = (2 * p + h) * LANES
            qa_ref[0, :, lo:lo + LANES] = _rope(q2[:, h * LANES:(h + 1) * LANES], cm, sm).astype(BF16)

    va_ref[0, 0] = _ones_row(_dot_nt(wvt_ref[...], ckv), pat_ref)
    vwt = _ones_row(_dot_nt(wwvt_ref[...], hb), pat_ref)
    for c in range(tm // WIN_BLOCK):
        vw_ref[0, c] = vwt[:, c * WIN_BLOCK:(c + 1) * WIN_BLOCK]


def _proj0(x, mod, ng, win, wuk, wvt, wuq, wwvt, seg, gv, pat, tabs):
    bx, t, _ = x.shape
    tm = _row_tile(t, MLA_TK)
    ntab = tabs[0].shape[0] // tm
    tab_spec = pl.BlockSpec((tm, LANES), lambda b, i: (i % ntab, 0))
    nwb = tm // WIN_BLOCK
    out_specs = [_tok_spec(tm, MLA_HEADS * LANES), _tok_spec(tm, MLA_HEADS * LANES),
                 pl.BlockSpec((1, 1, MLA_HEADS * VT_ROWS, tm), lambda b, i: (b, i, 0, 0)),
                 _tok_spec(tm, WIN_HEADS * WIN_HEAD_DIM), _tok_spec(tm, LANES),
                 pl.BlockSpec((1, nwb, WIN_KV_HEADS * VT_ROWS, WIN_BLOCK), lambda b, i: (b, i, 0, 0))]
    out_shape = [jax.ShapeDtypeStruct((bx, t, MLA_HEADS * LANES), BF16),
                 jax.ShapeDtypeStruct((bx, t, MLA_HEADS * LANES), BF16),
                 jax.ShapeDtypeStruct((bx, t // tm, MLA_HEADS * VT_ROWS, tm), BF16),
                 jax.ShapeDtypeStruct((bx, t, WIN_HEADS * WIN_HEAD_DIM), BF16),
                 jax.ShapeDtypeStruct((bx, t, LANES), BF16),
                 jax.ShapeDtypeStruct((bx, t // WIN_BLOCK, WIN_KV_HEADS * VT_ROWS, WIN_BLOCK), BF16)]
    return pl.pallas_call(
        _proj0_kernel,
        grid=(bx, t // tm),
        in_specs=[_tok_spec(tm, D_MODEL), _mod_spec(mod), _resident((3, D_MODEL)), _resident(win.shape),
                  _resident(wuk.shape), _resident(wvt.shape), _resident(wuq.shape), _resident(wwvt.shape),
                  _resident(seg.shape), _resident(gv.shape), _resident(pat.shape)] + [tab_spec] * 4,
        out_specs=out_specs,
        out_shape=out_shape,
        compiler_params=_params("parallel", "parallel"),
        name="proj0",
    )(x, mod, ng, win, wuk, wvt, wuq, wwvt, seg, gv, pat, *tabs)


def _mla_attn_kernel(*refs, has_latent):
    if has_latent:
        q_ref, kc_ref, vc_ref, k_ref, v_ref, o_ref, m_ref, acc_ref, qt_ref, s_ref, mc_ref = refs
    else:
        q_ref, kc_ref, vc_ref, o_ref, m_ref, acc_ref, qt_ref = refs

    def head(hh):
        return slice(hh * LANES, (hh + 1) * LANES)

    def vrows(hh):
        return slice(hh * VT_ROWS, (hh + 1) * VT_ROWS)

    for hh in range(2):
        qt_ref[hh] = q_ref[0, :, head(hh)].astype(F32).T.astype(BF16)

    def scores(kblk, hh):
        return _dot(kblk, qt_ref[hh])

    def qk(j, slot, hh):
        st = scores(k_ref[0, pl.ds(pl.multiple_of(j * MLA_TK, MLA_TK), MLA_TK), head(hh)], hh)
        s_ref[slot, hh] = st
        mc_ref[slot, hh] = jnp.max(st, axis=0, keepdims=True)

    def process(j, slot, hh):
        m_prev = m_ref[hh]
        m_new = jnp.maximum(m_prev, mc_ref[slot, hh])
        p = jnp.exp2(s_ref[slot, hh] - m_new).astype(BF16)
        pv = _dot(v_ref[0, j, vrows(hh), :], p)
        acc_ref[hh] = acc_ref[hh] * jnp.exp2(m_prev - m_new) + pv
        m_ref[hh] = m_new

    def ctx_softmax(st, hh):
        m_new = jnp.max(st, axis=0, keepdims=True)
        p = jnp.exp2(st - m_new).astype(BF16)
        acc_ref[hh] = _dot(vc_ref[0, vrows(hh)], p)
        m_ref[hh] = m_new

    def pair(nxt, cur, slot):
        for hh in range(2):
            qk(nxt, 1 - slot, hh)
            process(cur, slot, hh)

    sts = [scores(kc_ref[0, :, head(hh)], hh) for hh in range(2)]
    if has_latent:
        n = v_ref.shape[1]
        for hh in range(2):
            qk(0, 0, hh)
            ctx_softmax(sts[hh], hh)

        def body(i, carry):
            pair(2 * i + 1, 2 * i, 0)
            pair(2 * i + 2, 2 * i + 1, 1)
            return carry
        lax.fori_loop(0, n // 2 - 1, body, 0)
        pair(n - 1, n - 2, 0)
        for hh in range(2):
            process(n - 1, 1, hh)
    else:
        for hh in range(2):
            ctx_softmax(sts[hh], hh)
    a0, a1 = acc_ref[0], acc_ref[1]
    ot = jnp.concatenate([a0[:MLA_V_DIM] / a0[MLA_V_DIM:MLA_V_DIM + 1],
                          a1[:MLA_V_DIM] / a1[MLA_V_DIM:MLA_V_DIM + 1]], axis=0)
    o_ref[0] = ot.T.astype(BF16)


def _mla_attn(q, kc, vtc, k=None, vt=None):
    bsz, t, _ = q.shape
    nc = kc.shape[1]
    has_latent = k is not None
    tq = _row_tile(t, MLA_TQ)
    in_specs = [pl.BlockSpec((1, tq, 2 * LANES), lambda b, h, i: (b, i, h)),
                pl.BlockSpec((1, nc, 2 * LANES), lambda b, h, i: (b, 0, h)),
                pl.BlockSpec((1, 2 * VT_ROWS, nc), lambda b, h, i: (b, h, 0))]
    args = [q, kc, vtc]
    scratch = [pltpu.VMEM((2, 1, tq), F32), pltpu.VMEM((2, VT_ROWS, tq), F32), pltpu.VMEM((2, LANES, tq), BF16)]
    if has_latent:
        n = k.shape[1]
        nt, _, tk = vt.shape[1:]
        assert tk == MLA_TK and nt * tk == n and nt % 2 == 0, (n, nt, tk)
        in_specs += [pl.BlockSpec((1, n, 2 * LANES), lambda b, h, i: (b, 0, h)),
                     pl.BlockSpec((1, nt, 2 * VT_ROWS, tk), lambda b, h, i: (b, 0, h, 0))]
        args += [k, vt]
        scratch += [pltpu.VMEM((2, 2, tk, tq), F32), pltpu.VMEM((2, 2, 1, tq), F32)]
    return pl.pallas_call(
        functools.partial(_mla_attn_kernel, has_latent=has_latent),
        grid=(bsz, MLA_HEADS // 2, t // tq),
        in_specs=in_specs,
        out_specs=pl.BlockSpec((1, tq, LANES), lambda b, h, i: (b, i, h)),
        out_shape=jax.ShapeDtypeStruct((bsz, t, MLA_HEADS * MLA_V_DIM), BF16),
        scratch_shapes=scratch,
        compiler_params=_params("parallel", "parallel", "arbitrary"),
        name="mla_attn" if has_latent else "mla_attn_ctx",
    )(*args)


def _win_attn_kernel(*refs, has_window, nb):
    if has_window:
        sink_ref, q_ref, kc_ref, vc_ref, k_ref, vt_ref, bias_ref, o_ref, s_ref, mc_ref = refs
    else:
        sink_ref, q_ref, kc_ref, vc_ref, o_ref, s_ref, mc_ref = refs
    qb = q_ref.shape[1] // WIN_BLOCK
    nc = kc_ref.shape[1]
    nct = vc_ref.shape[1]
    i = pl.program_id(1)
    units = [(blk, n) for blk in range(qb) for n in range(WIN_KV_HEADS)]
    cache = {}

    def block_operands(blk):
        if blk in cache:
            return cache[blk]
        vts = [vc_ref[0, t] for t in range(nct)]
        bias = None
        if has_window:
            g = i * qb + blk
            start = jnp.clip(g - 1, 0, nb - 3)
            kwin = k_ref[0, pl.ds(pl.multiple_of(start * WIN_BLOCK, WIN_BLOCK), 3 * WIN_BLOCK), :]
            kall = jnp.concatenate([kc_ref[0], kwin], axis=0)
            vts += [vt_ref[0, start + t] for t in range(3)]
            bias = jnp.concatenate([bias_ref[g - start]] * WIN_GROUP, axis=1)
        else:
            kall = kc_ref[0]
        rows = slice(blk * WIN_BLOCK, (blk + 1) * WIN_BLOCK)
        qs = jnp.concatenate([q_ref[0, rows, j * LANES:(j + 1) * LANES] for j in range(WIN_GROUP)], axis=0)
        cache[blk] = (kall, jnp.concatenate(vts, axis=1), qs, bias)
        return cache[blk]

    def qk(u, slot):
        blk, n = units[u]
        kall, _, qs, bias = block_operands(blk)
        head0 = (_lane(qs.shape) & 32) == 0
        zero = jnp.zeros_like(qs)
        st = _dot_nt(kall, jnp.where(head0, qs, zero) if n == 0 else jnp.where(head0, zero, qs))
        if bias is not None:
            st = jnp.concatenate([st[:nc], st[nc:] + bias], axis=0)
        s_ref[slot] = st
        mc_ref[slot] = jnp.max(st, axis=0, keepdims=True)

    outs = {}

    def process(u, slot):
        blk, n = units[u]
        vall = block_operands(blk)[1]
        sink = sink_ref[n]
        m = jnp.maximum(mc_ref[slot], sink)
        p = jnp.exp2(s_ref[slot] - m).astype(BF16)
        pv = _dot(vall[n * VT_ROWS:(n + 1) * VT_ROWS], p)
        denom = pv[WIN_HEAD_DIM:WIN_HEAD_DIM + 1] + jnp.exp2(sink - m)
        outs[(blk, n)] = pv[:WIN_HEAD_DIM] / denom
        if n == WIN_KV_HEADS - 1:
            rows = slice(blk * WIN_BLOCK, (blk + 1) * WIN_BLOCK)
            for j in range(WIN_GROUP):
                cols = slice(j * WIN_BLOCK, (j + 1) * WIN_BLOCK)
                both = jnp.concatenate([outs[(blk, 0)][:, cols], outs[(blk, 1)][:, cols]], axis=0)
                o_ref[0, rows, j * LANES:(j + 1) * LANES] = both.T.astype(BF16)

    qk(0, 0)
    for u in range(len(units)):
        if u + 1 < len(units):
            qk(u + 1, (u + 1) % 2)
        process(u, u % 2)


def _band_bias():
    krow = np.arange(3 * WIN_BLOCK)[:, None]
    qcol = np.arange(WIN_BLOCK)[None, :]
    keep = [np.abs(krow - (qcol + off * WIN_BLOCK)) <= WINDOW for off in range(3)]
    return jnp.asarray(np.where(np.stack(keep), 0.0, NEG_INF), F32)


def _win_attn(sinkrow, q, kc, vtc, k=None, vt=None):
    bsz, t, _ = q.shape
    nc = kc.shape[1]
    has_window = k is not None
    nb = t // WIN_BLOCK
    tq = _row_tile(t, WIN_TQ)
    kvw = WIN_KV_HEADS * WIN_HEAD_DIM
    nk = nc + (3 * WIN_BLOCK if has_window else 0)
    in_specs = [pl.BlockSpec((WIN_KV_HEADS, 1, WIN_GROUP * WIN_BLOCK), lambda b, i: (0, 0, 0)),
                pl.BlockSpec((1, tq, WIN_HEADS * WIN_HEAD_DIM), lambda b, i: (b, i, 0)),
                pl.BlockSpec((1, nc, kvw), lambda b, i: (b, 0, 0)),
                pl.BlockSpec((1, nc // WIN_BLOCK, WIN_KV_HEADS * VT_ROWS, WIN_BLOCK), lambda b, i: (b, 0, 0, 0))]
    args = [sinkrow, q, kc, vtc]
    if has_window:
        assert nb >= 3, nb
        in_specs += [pl.BlockSpec((1, t, kvw), lambda b, i: (b, 0, 0)),
                     pl.BlockSpec((1, nb, WIN_KV_HEADS * VT_ROWS, WIN_BLOCK), lambda b, i: (b, 0, 0, 0)),
                     pl.BlockSpec((3, 3 * WIN_BLOCK, WIN_BLOCK), lambda b, i: (0, 0, 0))]
        args += [k, vt, _band_bias()]
    return pl.pallas_call(
        functools.partial(_win_attn_kernel, has_window=has_window, nb=nb),
        grid=(bsz, t // tq),
        in_specs=in_specs,
        out_specs=pl.BlockSpec((1, tq, WIN_HEADS * WIN_HEAD_DIM), lambda b, i: (b, i, 0)),
        out_shape=jax.ShapeDtypeStruct((bsz, t, WIN_HEADS * WIN_HEAD_DIM), BF16),
        scratch_shapes=[pltpu.VMEM((2, nk, WIN_GROUP * WIN_BLOCK), F32),
                        pltpu.VMEM((2, 1, WIN_GROUP * WIN_BLOCK), F32)],
        compiler_params=_params("parallel", "parallel"),
        name="win_attn" if has_window else "win_attn_ctx",
    )(*args)


def _log_sigmoid(x):
    return jnp.minimum(x, 0.0) - jnp.log(1.0 + jnp.exp(-jnp.abs(x)))


def _proj1_kernel(x_ref, mod_ref, ng_ref, win_ref, wgk_ref, bgk_ref, k_ref, v_ref, q_ref, gt_ref, gf_ref, gb_ref):
    hb, _ = _pre_mod(x_ref[0], mod_ref, ng_ref, 1)
    ql = _dot(hb, win_ref[:, P1_Q:P1_LOW + LANES])
    q_ref[0] = ql[:, :GLA_DK] * (GLA_DK_HEAD ** -0.5)
    low = ql[:, GLA_DK:].astype(BF16)
    k_ref[0] = _dot(hb, win_ref[:, P1_K:P1_K + GLA_DK])
    pre_f = _dot(low, wgk_ref[:, :GLA_DK]) + bgk_ref[:, :GLA_DK]
    gf_ref[0] = _log_sigmoid(pre_f) * (1.0 / GLA_GATE_NORM)
    v_ref[0] = _dot(hb, win_ref[:, P1_V:P1_V + GLA_DV]).astype(BF16)
    pre_b = _dot(low, wgk_ref[:, GLA_DK:]) + bgk_ref[:, GLA_DK:]
    gb_ref[0] = _log_sigmoid(pre_b) * (1.0 / GLA_GATE_NORM)
    gt_ref[0] = _silu(_dot(hb, win_ref[:, P1_G:P1_G + GLA_DV]))


def _proj1(x, mod, ng, win, wgk, bgk):
    bx, t, _ = x.shape
    tm = _row_tile(t)
    outs = ((GLA_DK, F32), (GLA_DV, BF16), (GLA_DK, F32), (GLA_DV, F32), (GLA_DK, F32), (GLA_DK, F32))
    return pl.pallas_call(
        _proj1_kernel,
        grid=(bx, t // tm),
        in_specs=[_tok_spec(tm, D_MODEL), _mod_spec(mod), _resident((3, D_MODEL)), _resident(win.shape),
                  _resident(wgk.shape), _resident(bgk.shape)],
        out_specs=[_tok_spec(tm, w) for w, _ in outs],
        out_shape=[jax.ShapeDtypeStruct((bx, t, w), d) for w, d in outs],
        compiler_params=_params("parallel", "parallel"),
        name="proj1",
    )(x, mod, ng, win, wgk, bgk)


def _gla_block(q_ref, k_ref, v_ref, g_ref, o_ref, st_ref, bi, cum, keep, backward):
    t = GLA_BLOCK
    g = g_ref[bi]
    g_hi = g.astype(BF16)
    g_lo = (g - g_hi.astype(F32)).astype(BF16)
    both = _dot(cum, g_hi) + _dot(cum, g_lo)
    bb, tot = both[:t], both[t:]
    k = k_ref[bi]
    q_dec = (q_ref[bi] * jnp.exp(bb)).astype(BF16)
    k_inv = (k * jnp.exp(-bb)).astype(BF16)
    k_end = (k * jnp.exp(tot - bb)).astype(BF16)
    decay = jnp.exp(tot)
    n_sub = t // GLA_CHUNK
    order = range(n_sub - 1, -1, -1) if backward else range(n_sub)
    for h in range(GLA_HEADS):
        ks = slice(h * GLA_DK_HEAD, (h + 1) * GLA_DK_HEAD)
        vs = slice(h * GLA_DV_HEAD, (h + 1) * GLA_DV_HEAD)
        vh = v_ref[bi, :, vs]
        a = jnp.where(keep, _dot_nt(q_dec[:, ks], k_inv[:, ks]), 0.0).astype(BF16)
        o_intra = _dot(a, vh)
        st = st_ref[bi, h]
        for c in order:
            rows = slice(c * GLA_CHUNK, (c + 1) * GLA_CHUNK)
            o_ref[bi, rows, vs] = o_intra[rows] + _dot_nt(q_dec[rows, ks], st.astype(BF16))
            st = st * decay[c * GLA_CHUNK:c * GLA_CHUNK + 1, ks] + _dot_tn(vh[rows], k_end[rows, ks])
        st_ref[bi, h] = st


def _gla_scan_kernel(qf_ref, kf_ref, vf_ref, gf_ref, qb_ref, kb_ref, vb_ref, gb_ref, s0f_ref, s0b_ref, cum_ref,
                     of_ref, ob_ref, sf_ref, sb_ref, stf_ref, stb_ref):
    i = pl.program_id(1)

    @pl.when(i == 0)
    def _():
        stf_ref[...] = s0f_ref[...]
        stb_ref[...] = s0b_ref[...]

    row = lax.broadcasted_iota(jnp.int32, (GLA_BLOCK, GLA_BLOCK), 0)
    col = lax.broadcasted_iota(jnp.int32, (GLA_BLOCK, GLA_BLOCK), 1)
    same = (row // GLA_CHUNK) == (col // GLA_CHUNK)
    for bi in range(qf_ref.shape[0]):
        _gla_block(qf_ref, kf_ref, vf_ref, gf_ref, of_ref, stf_ref, bi, cum_ref[0], same & (col <= row), False)
        _gla_block(qb_ref, kb_ref, vb_ref, gb_ref, ob_ref, stb_ref, bi, cum_ref[1], same & (col >= row), True)

    @pl.when(i == pl.num_programs(1) - 1)
    def _():
        sf_ref[...] = stf_ref[...]
        sb_ref[...] = stb_ref[...]


def _cum_matrices():
    r = np.arange(GLA_BLOCK)[:, None]
    c = np.arange(GLA_BLOCK)[None, :]
    same = (r // GLA_CHUNK) == (c // GLA_CHUNK)
    fwd = np.concatenate([same & (c <= r), same], axis=0)
    bwd = np.concatenate([same & (c >= r), same], axis=0)
    return jnp.asarray(np.stack([fwd, bwd]), BF16)


def _gla_scan(q, k, v, gf, gb, s0f, s0b):
    bsz, t, _ = q.shape
    tb = GLA_BLOCK
    assert t % tb == 0, t
    nblk = t // tb
    nbat = 2 if bsz % 2 == 0 else 1
    fwd = lambda w: pl.BlockSpec((nbat, tb, w), lambda b, i: (b, i, 0))
    bwd = lambda w: pl.BlockSpec((nbat, tb, w), lambda b, i: (b, nblk - 1 - i, 0))
    st_spec = pl.BlockSpec((nbat, GLA_HEADS, GLA_DV_HEAD, GLA_DK_HEAD), lambda b, i: (b, 0, 0, 0))
    st_shape = jax.ShapeDtypeStruct((bsz, GLA_HEADS, GLA_DV_HEAD, GLA_DK_HEAD), F32)
    cum = _cum_matrices()
    return pl.pallas_call(
        _gla_scan_kernel,
        grid=(bsz // nbat, nblk),
        in_specs=[fwd(GLA_DK), fwd(GLA_DK), fwd(GLA_DV), fwd(GLA_DK),
                  bwd(GLA_DK), bwd(GLA_DK), bwd(GLA_DV), bwd(GLA_DK), st_spec, st_spec, _resident(cum.shape)],
        out_specs=[fwd(GLA_DV), bwd(GLA_DV), st_spec, st_spec],
        out_shape=[jax.ShapeDtypeStruct((bsz, t, GLA_DV), F32)] * 2 + [st_shape] * 2,
        scratch_shapes=[pltpu.VMEM((nbat, GLA_HEADS, GLA_DV_HEAD, GLA_DK_HEAD), F32)] * 2,
        compiler_params=_params("parallel", "arbitrary"),
        name="gla_scan",
    )(q, k, v, gf, q, k, v, gb, s0f, s0b, cum)


def _deinterleave(n):
    return np.concatenate([np.arange(0, n, 2), np.arange(1, n, 2)])


def _gather_cols(w, idx):
    idx = np.asarray(idx)
    cols = jnp.take(w, jnp.asarray(np.maximum(idx, 0)), axis=-1)
    return jnp.where(jnp.asarray(idx >= 0), cols, 0.0)


def _mla_slot(nope, rope):
    pad = np.full((16,), -1, np.int64)
    return np.concatenate([nope[:32], rope[:16], pad, nope[32:], rope[16:], pad])


def _win_slot(a, b):
    return np.concatenate([a[:32], b[:32], a[32:], b[32:]])


def _take(vec, idx):
    return jnp.where(jnp.asarray(idx >= 0), vec[jnp.asarray(np.maximum(idx, 0))], 0.0)


def _layer0_layout(p):
    ckv0, kr0 = 0, MLA_KV_RANK
    wk0 = kr0 + MLA_ROPE_DIM
    wv0 = wk0 + WIN_KV_HEADS * WIN_HEAD_DIM
    cq0 = wv0 + WIN_KV_HEADS * WIN_HEAD_DIM
    wq0 = cq0 + MLA_Q_RANK
    de64, de32 = _deinterleave(WIN_HEAD_DIM), _deinterleave(MLA_ROPE_DIM)
    none64, none32 = np.full((64,), -1, np.int64), np.full((32,), -1, np.int64)

    idx = np.full((P0_COLS,), -1, np.int64)
    idx[P0_CKV:P0_CKV + MLA_KV_RANK] = ckv0 + np.arange(MLA_KV_RANK)
    idx[P0_WK:P0_WK + LANES] = _win_slot(wk0 + de64, wk0 + 64 + de64)
    idx[P0_CQ:P0_CQ + MLA_Q_RANK] = cq0 + np.arange(MLA_Q_RANK)
    for j in range(WIN_GROUP):
        idx[P0_WQ + j * LANES:P0_WQ + (j + 1) * LANES] = _win_slot(wq0 + j * 64 + de64,
                                                                   wq0 + (WIN_GROUP + j) * 64 + de64)
    idx[P0_KR:P0_KR + LANES] = _mla_slot(none64, kr0 + de32)
    w_in = _gather_cols(p["w_in"], idx).astype(BF16)
    vpad = np.full((VT_ROWS - MLA_V_DIM,), -1, np.int64)
    vidx = np.concatenate([np.concatenate([wv0 + n * 64 + np.arange(64), vpad]) for n in range(WIN_KV_HEADS)])
    w_wvt = _gather_cols(p["w_in"], vidx).T.astype(BF16)

    per = MLA_NOPE_DIM + MLA_V_DIM
    kidx, qidx, vidx = [], [], []
    for h in range(MLA_HEADS):
        kidx.append(_mla_slot(h * per + np.arange(64), none32))
        qidx.append(_mla_slot(h * MLA_QK_DIM + np.arange(64), h * MLA_QK_DIM + MLA_NOPE_DIM + de32))
        vidx.append(np.concatenate([h * per + MLA_NOPE_DIM + np.arange(64), vpad]))
    w_uk = _gather_cols(p["mla_w_ukv"], np.concatenate(kidx)).astype(BF16)
    w_vt = _gather_cols(p["mla_w_ukv"], np.concatenate(vidx)).T.astype(BF16)
    pat = np.zeros((MLA_HEADS * VT_ROWS, LANES), np.float32)
    pat[MLA_V_DIM::VT_ROWS] = 1.0
    w_uq = _gather_cols(p["mla_w_uq"], np.concatenate(qidx)).astype(BF16)

    lane = np.arange(LANES)
    nope, rot = (lane & 32) == 0, (lane & 48) == 32
    seg_a = (nope[:, None] & nope[None, :]) | (rot[:, None] & rot[None, :])
    seg_b = nope[:, None] == nope[None, :]
    two = lambda m: np.kron(np.eye(2), m.astype(np.float32))
    seg = jnp.asarray(np.stack([two(seg_a), two(seg_b)]), BF16)

    a64, a32 = np.arange(64), np.arange(32)
    twice = lambda v: jnp.concatenate([v, v])
    wide = lambda v: jnp.concatenate([v, jnp.zeros((LANES,), F32)])
    g_wq = _take(p["win_g_q"], _win_slot(de64, de64)) * (WIN_SCALE * LOG2E)
    g_wk = _take(p["win_g_k"], _win_slot(de64, de64))
    g_qslot = jnp.concatenate([p["mla_g_qn"], p["mla_g_qr"]])
    inv_n = jnp.concatenate([jnp.full((64,), 1.0 / MLA_NOPE_DIM, F32), jnp.full((32,), 1.0 / MLA_ROPE_DIM, F32)])
    gv = jnp.stack([
        wide(p["mla_g_kva"]),
        p["mla_g_qa"],
        twice(_take(p["mla_g_kn"], _mla_slot(a64, none32))),
        wide(_take(p["mla_g_kr"], _mla_slot(none64, de32))),
        twice(_take(g_qslot, _mla_slot(a64, 64 + de32))) * (MLA_SCALE * LOG2E),
        twice(g_wq),
        wide(g_wk),
        twice(_take(inv_n, _mla_slot(a64, 64 + a32))),
    ])

    out_rows = np.arange(D_MODEL)
    base = MLA_HEADS * MLA_V_DIM
    for j in range(WIN_GROUP):
        for n in range(WIN_KV_HEADS):
            lo = base + j * LANES + n * 64
            out_rows[lo:lo + 64] = base + (n * WIN_GROUP + j) * 64 + np.arange(64)
    w_out = p["w_out"][jnp.asarray(out_rows)].astype(BF16)
    return (w_in, w_uk, w_vt, w_uq, w_wvt, seg, gv, jnp.asarray(pat)), w_out


def _layer1_layout(p):
    k0, v0 = 0, GLA_DK
    lf0 = v0 + GLA_DV
    lb0 = lf0 + GLA_LOWRANK
    q0 = lb0 + GLA_LOWRANK
    g0 = q0 + GLA_DK
    idx = np.full((P1_COLS,), -1, np.int64)
    idx[P1_K:P1_K + GLA_DK] = k0 + np.arange(GLA_DK)
    idx[P1_V:P1_V + GLA_DV] = v0 + np.arange(GLA_DV)
    idx[P1_Q:P1_Q + GLA_DK] = q0 + np.arange(GLA_DK)
    idx[P1_G:P1_G + GLA_DV] = g0 + np.arange(GLA_DV)
    idx[P1_LOW:P1_LOW + GLA_LOWRANK] = lf0 + np.arange(GLA_LOWRANK)
    idx[P1_LOW + GLA_LOWRANK:P1_LOW + 2 * GLA_LOWRANK] = lb0 + np.arange(GLA_LOWRANK)
    w_in = _gather_cols(p["w_in"], idx).astype(BF16)
    wgk = jnp.zeros((LANES, 2 * GLA_DK), F32)
    wgk = wgk.at[:GLA_LOWRANK, :GLA_DK].set(p["w_gk_f"])
    wgk = wgk.at[GLA_LOWRANK:2 * GLA_LOWRANK, GLA_DK:].set(p["w_gk_b"])
    bgk = jnp.concatenate([p["b_gk_f"], p["b_gk_b"]]).reshape(1, 2 * GLA_DK)
    return w_in, wgk.astype(BF16), bgk


def _rope_tables(seq):
    t = np.arange(seq)
    row = (t // GRID_W).astype(np.float32)
    col = (t % GRID_W).astype(np.float32)

    def angles(rot_dim):
        n_freq = rot_dim // 4
        inv = (ROPE_BASE ** (-np.arange(n_freq, dtype=np.float32) / n_freq)).astype(np.float32)
        return np.concatenate([row[:, None] * inv, col[:, None] * inv], axis=-1)

    am, aw = angles(MLA_ROPE_DIM), angles(WIN_HEAD_DIM)
    cos_m, sin_m, cos_w, sin_w = np.cos(am), np.sin(am), np.cos(aw), np.sin(aw)
    one32, one16 = np.ones((seq, 32)), np.ones((seq, 16))
    cm = np.concatenate([one32, cos_m, one16] * 2, axis=-1)
    sm = np.concatenate([0 * one32, -sin_m, 0 * one16, 0 * one32, sin_m, 0 * one16], axis=-1)
    cw = np.concatenate([cos_w] * 4, axis=-1)
    sw = np.concatenate([-sin_w, -sin_w, sin_w, sin_w], axis=-1)
    return tuple(jnp.asarray(a, F32) for a in (cm, sm, cw, sw))


def _identity_tables(rows):
    one, zero = jnp.ones((rows, LANES), F32), jnp.zeros((rows, LANES), F32)
    return one, zero, one, zero


def kernel(x, c, ctx, c_ctx, l0_norm_g, l0_w_mod, l0_b_mod, l0_ffn1_w_gu, l0_ffn1_w_down, l0_ffn2_w_gu, l0_ffn2_w_down, l0_w_in, l0_mla_g_qa, l0_mla_g_kva, l0_mla_w_uq, l0_mla_w_ukv, l0_mla_g_qn, l0_mla_g_qr, l0_mla_g_kn, l0_mla_g_kr, l0_win_g_q, l0_win_g_k, l0_win_sink, l0_w_out, l1_norm_g, l1_w_mod, l1_b_mod, l1_ffn1_w_gu, l1_ffn1_w_down, l1_ffn2_w_gu, l1_ffn2_w_down, l1_w_in, l1_w_gk_f, l1_b_gk_f, l1_w_gk_b, l1_b_gk_b, l1_g_norm, l1_w_out):
    bsz, seq, _ = x.shape
    n_ctx = ctx.shape[1]
    out_dtype = x.dtype
    x = x.astype(F32)
    xc = ctx.astype(F32)

    cc = jnp.zeros((16, D_MODEL), F32).at[:bsz].set(c).at[bsz].set(c_ctx)

    def mods(w_mod, b_mod):
        tab = _mod_table(cc, w_mod, b_mod).reshape(16, N_MOD, D_MODEL)
        return tab[:bsz], tab[bsz:bsz + 1]

    p0 = dict(w_in=l0_w_in, mla_g_qa=l0_mla_g_qa, mla_g_kva=l0_mla_g_kva, mla_w_uq=l0_mla_w_uq,
              mla_w_ukv=l0_mla_w_ukv, mla_g_qn=l0_mla_g_qn, mla_g_qr=l0_mla_g_qr, mla_g_kn=l0_mla_g_kn,
              mla_g_kr=l0_mla_g_kr, win_g_q=l0_win_g_q, win_g_k=l0_win_g_k, w_out=l0_w_out)
    proj_w, w_out0 = _layer0_layout(p0)
    mod, mod_c = mods(l0_w_mod, l0_b_mod)
    wgu1, wd1 = l0_ffn1_w_gu.astype(BF16), l0_ffn1_w_down.astype(BF16)
    wgu2, wd2 = l0_ffn2_w_gu.astype(BF16), l0_ffn2_w_down.astype(BF16)
    sinkrow = jnp.repeat(l0_win_sink.astype(F32) * LOG2E, WIN_BLOCK, axis=1).reshape(WIN_KV_HEADS, 1, -1)

    x = _ffn(x, mod, l0_norm_g, wgu1, wd1, 0)
    xc = _ffn(xc, mod_c, l0_norm_g, wgu1, wd1, 0)
    qa, ka, vat, qw, kw, vwt = _proj0(x, mod, l0_norm_g, *proj_w, _rope_tables(seq))
    qa_c, ka_c, vat_c, qw_c, kw_c, vwt_c = _proj0(xc, mod_c, l0_norm_g, *proj_w, _identity_tables(n_ctx))
    vat_c = vat_c.reshape(bsz, MLA_HEADS * VT_ROWS, n_ctx)
    oa = _mla_attn(qa, ka_c, vat_c, ka, vat)
    ow = _win_attn(sinkrow, qw, kw_c, vwt_c, kw, vwt)
    oa_c = _mla_attn(qa_c, ka_c, vat_c)
    ow_c = _win_attn(sinkrow, qw_c, kw_c, vwt_c)
    x = _attn_out_ffn(x, oa, ow, mod, l0_norm_g, w_out0, wgu2, wd2)
    xc = _attn_out_ffn(xc, oa_c, ow_c, mod_c, l0_norm_g, w_out0, wgu2, wd2)

    w_in1, wgk, bgk = _layer1_layout(dict(w_in=l1_w_in, w_gk_f=l1_w_gk_f, b_gk_f=l1_b_gk_f,
                                          w_gk_b=l1_w_gk_b, b_gk_b=l1_b_gk_b))
    mod, mod_c = mods(l1_w_mod, l1_b_mod)
    wgu1, wd1 = l1_ffn1_w_gu.astype(BF16), l1_ffn1_w_down.astype(BF16)
    wgu2, wd2 = l1_ffn2_w_gu.astype(BF16), l1_ffn2_w_down.astype(BF16)

    x = _ffn(x, mod, l1_norm_g, wgu1, wd1, 0)
    xc = _ffn(xc, mod_c, l1_norm_g, wgu1, wd1, 0)
    k1, v1, q1, gt1, gf1, gb1 = _proj1(x, mod, l1_norm_g, w_in1, wgk, bgk)
    k1c, v1c, q1c, _, gf1c, gb1c = _proj1(xc, mod_c, l1_norm_g, w_in1, wgk, bgk)
    zero_state = jnp.zeros((bsz, GLA_HEADS, GLA_DV_HEAD, GLA_DK_HEAD), F32)
    _, _, s_f, s_b = _gla_scan(q1c, k1c, v1c, gf1c, gb1c, zero_state, zero_state)
    o_f, o_b, _, _ = _gla_scan(q1, k1, v1, gf1, gb1, s_f, s_b)
    x = _gla_out_ffn(x, o_f, o_b, gt1, l1_g_norm.reshape(1, GLA_DV_HEAD), mod, l1_norm_g,
                     l1_w_out.astype(BF16), wgu2, wd2)
    return x.astype(out_dtype)
```

```python
import functools

import numpy as np
import jax
import jax.numpy as jnp
from jax import lax
from jax.experimental import pallas as pl
from jax.experimental.pallas import tpu as pltpu

F32 = jnp.float32
BF16 = jnp.bfloat16

D_MODEL = 1024
GRID_W = 64
N_MOD = 9
EPS = 1e-6
ROPE_BASE = 10000.0
NEG_INF = -1e30
D_FF = 2816
LOG2E = 1.4426950408889634

MLA_HEADS = 8
MLA_Q_RANK = 256
MLA_KV_RANK = 128
MLA_NOPE_DIM = 64
MLA_ROPE_DIM = 32
MLA_V_DIM = 64
MLA_QK_DIM = MLA_NOPE_DIM + MLA_ROPE_DIM
MLA_SCALE = MLA_QK_DIM ** -0.5

WIN_HEADS = 8
WIN_KV_HEADS = 2
WIN_GROUP = WIN_HEADS // WIN_KV_HEADS
WIN_HEAD_DIM = 64
WINDOW = 128
WIN_BLOCK = 128
WIN_SCALE = WIN_HEAD_DIM ** -0.5

GLA_HEADS = 4
GLA_DK = D_MODEL // 2
GLA_DV = D_MODEL
GLA_DK_HEAD = GLA_DK // GLA_HEADS
GLA_DV_HEAD = GLA_DV // GLA_HEADS
GLA_LOWRANK = 16
GLA_GATE_NORM = 16.0
GLA_CHUNK = 64
GLA_BLOCK = 256

LANES = 128
VT_ROWS = 128
FF_CHUNK = 256
FFN_TM = 512
MLA_TQ = 1024
MLA_TK = 512
WIN_TQ = 1024
VMEM_LIMIT = 58 * 1024 * 1024

P0_CKV, P0_KR, P0_CQ, P0_WQ, P0_WK, P0_COLS = 0, 128, 256, 512, 1024, 1152
P1_K, P1_V, P1_Q, P1_LOW, P1_G, P1_COLS = 0, 512, 1536, 2048, 2176, 3200


def _dot(a, b):
    return jnp.dot(a, b, preferred_element_type=F32)


def _dot_nt(a, b):
    return lax.dot_general(a, b, (((1,), (1,)), ((), ())), preferred_element_type=F32)


def _dot_tn(a, b):
    return lax.dot_general(a, b, (((0,), (0,)), ((), ())), preferred_element_type=F32)


def _params(*sem):
    return pltpu.CompilerParams(dimension_semantics=sem, vmem_limit_bytes=VMEM_LIMIT)


def _resident(shape):
    nd = len(shape)
    return pl.BlockSpec(shape, lambda *_: (0,) * nd, pipeline_mode=pl.Buffered(1))


def _silu(x):
    return x / (1.0 + jnp.exp(-x))


def _pre_mod(x, mod_ref, ng_ref, sub):
    shift = mod_ref[0, 3 * sub:3 * sub + 1, :]
    scale = mod_ref[0, 3 * sub + 1:3 * sub + 2, :]
    gate = mod_ref[0, 3 * sub + 2:3 * sub + 3, :]
    rinv = lax.rsqrt(jnp.mean(x * x, axis=-1, keepdims=True) + EPS)
    h = x * rinv * ng_ref[sub:sub + 1, :] * (1.0 + scale) + shift
    return h.astype(BF16), gate


def _swiglu(hb, wgu_ref, wd_ref, act_ref):
    for c in range(D_FF // FF_CHUNK):
        lo = c * FF_CHUNK
        g = _dot(hb, wgu_ref[:, lo:lo + FF_CHUNK])
        u = _dot(hb, wgu_ref[:, D_FF + lo:D_FF + lo + FF_CHUNK])
        act_ref[:, lo:lo + FF_CHUNK] = (_silu(g) * u).astype(BF16)
    return _dot(act_ref[...], wd_ref[...])


def _mod_kernel(c_ref, w_ref, b_ref, o_ref):
    a = _silu(c_ref[...]).astype(BF16)
    o_ref[...] = _dot(a, w_ref[...].astype(BF16)) + b_ref[...]


def _mod_table(cc, w_mod, b_mod):
    rows = cc.shape[0]
    n = w_mod.shape[1]
    bn = 1152
    return pl.pallas_call(
        _mod_kernel,
        grid=(n // bn,),
        in_specs=[pl.BlockSpec((rows, D_MODEL), lambda j: (0, 0)),
                  pl.BlockSpec((D_MODEL, bn), lambda j: (0, j)),
                  pl.BlockSpec((1, bn), lambda j: (0, j))],
        out_specs=pl.BlockSpec((rows, bn), lambda j: (0, j)),
        out_shape=jax.ShapeDtypeStruct((rows, n), F32),
        compiler_params=_params("arbitrary"),
        name="mod_table",
    )(cc, w_mod, b_mod.reshape(1, n))


def _ffn_kernel(x_ref, mod_ref, ng_ref, wgu_ref, wd_ref, o_ref, act_ref, *, sub):
    x = x_ref[0]
    hb, gate = _pre_mod(x, mod_ref, ng_ref, sub)
    o_ref[0] = x + 0.5 * gate * _swiglu(hb, wgu_ref, wd_ref, act_ref)


def _attn_out_ffn_kernel(x_ref, oa_ref, ob_ref, mod_ref, ng_ref, wo_ref, wgu_ref, wd_ref, o_ref, act_ref):
    half = oa_ref.shape[-1]
    y = _dot(oa_ref[0], wo_ref[:half, :]) + _dot(ob_ref[0], wo_ref[half:, :])
    x = x_ref[0] + mod_ref[0, 5:6, :] * y
    hb, gate = _pre_mod(x, mod_ref, ng_ref, 2)
    o_ref[0] = x + 0.5 * gate * _swiglu(hb, wgu_ref, wd_ref, act_ref)


def _gla_out_ffn_kernel(x_ref, of_ref, ob_ref, gt_ref, gn_ref, mod_ref, ng_ref, wo_ref, wgu_ref, wd_ref,
                        o_ref, act_ref, on_ref):
    for h in range(GLA_HEADS):
        hs = slice(h * GLA_DV_HEAD, (h + 1) * GLA_DV_HEAD)
        o = of_ref[0, :, hs] + ob_ref[0, :, hs]
        rinv = lax.rsqrt(jnp.mean(o * o, axis=-1, keepdims=True) + EPS)
        on_ref[:, hs] = (o * rinv * gn_ref[...] * gt_ref[0, :, hs]).astype(BF16)
    x = x_ref[0] + mod_ref[0, 5:6, :] * _dot(on_ref[...], wo_ref[...])
    hb, gate = _pre_mod(x, mod_ref, ng_ref, 2)
    o_ref[0] = x + 0.5 * gate * _swiglu(hb, wgu_ref, wd_ref, act_ref)


def _tok_spec(tm, width):
    return pl.BlockSpec((1, tm, width), lambda b, i: (b, i, 0))


def _mod_spec(mod):
    if mod.shape[0] == 1:
        return pl.BlockSpec((1, N_MOD, D_MODEL), lambda b, i: (0, 0, 0))
    return pl.BlockSpec((1, N_MOD, D_MODEL), lambda b, i: (b, 0, 0))


def _row_tile(t, pref=512):
    return pref if t % pref == 0 else t


def _ffn(x, mod, ng, wgu, wd, sub):
    bx, t, _ = x.shape
    tm = _row_tile(t, FFN_TM)
    return pl.pallas_call(
        functools.partial(_ffn_kernel, sub=sub),
        grid=(bx, t // tm),
        in_specs=[_tok_spec(tm, D_MODEL), _mod_spec(mod), _resident((3, D_MODEL)),
                  _resident(wgu.shape), _resident(wd.shape)],
        out_specs=_tok_spec(tm, D_MODEL),
        out_shape=jax.ShapeDtypeStruct(x.shape, F32),
        scratch_shapes=[pltpu.VMEM((tm, D_FF), BF16)],
        compiler_params=_params("parallel", "parallel"),
        name="ffn",
    )(x, mod, ng, wgu, wd)


def _attn_out_ffn(x, oa, ob, mod, ng, wo, wgu, wd):
    bx, t, _ = x.shape
    tm = _row_tile(t)
    return pl.pallas_call(
        _attn_out_ffn_kernel,
        grid=(bx, t // tm),
        in_specs=[_tok_spec(tm, D_MODEL), _tok_spec(tm, oa.shape[-1]), _tok_spec(tm, ob.shape[-1]),
                  _mod_spec(mod), _resident((3, D_MODEL)), _resident(wo.shape),
                  _resident(wgu.shape), _resident(wd.shape)],
        out_specs=_tok_spec(tm, D_MODEL),
        out_shape=jax.ShapeDtypeStruct(x.shape, F32),
        scratch_shapes=[pltpu.VMEM((tm, D_FF), BF16)],
        compiler_params=_params("parallel", "parallel"),
        name="attn_out_ffn",
    )(x, oa, ob, mod, ng, wo, wgu, wd)


def _gla_out_ffn(x, of, ob, gt, gn, mod, ng, wo, wgu, wd):
    bx, t, _ = x.shape
    tm = _row_tile(t)
    return pl.pallas_call(
        _gla_out_ffn_kernel,
        grid=(bx, t // tm),
        in_specs=[_tok_spec(tm, D_MODEL), _tok_spec(tm, GLA_DV), _tok_spec(tm, GLA_DV), _tok_spec(tm, GLA_DV),
                  _resident(gn.shape), _mod_spec(mod), _resident((3, D_MODEL)), _resident(wo.shape),
                  _resident(wgu.shape), _resident(wd.shape)],
        out_specs=_tok_spec(tm, D_MODEL),
        out_shape=jax.ShapeDtypeStruct(x.shape, F32),
        scratch_shapes=[pltpu.VMEM((tm, D_FF), BF16), pltpu.VMEM((tm, GLA_DV), BF16)],
        compiler_params=_params("parallel", "parallel"),
        name="gla_out_ffn",
    )(x, of, ob, gt, gn, mod, ng, wo, wgu, wd)


def _lane(shape):
    return lax.broadcasted_iota(jnp.int32, shape, 1)


def _rope(x, cos, sin):
    return x * cos + pltpu.roll(x, LANES // 2, 1) * sin


def _ones_row(vt, pat_ref):
    pat = pat_ref[:vt.shape[0], :]
    return (vt + jnp.concatenate([pat] * (vt.shape[1] // LANES), axis=1)).astype(BF16)


def _proj0_kernel(x_ref, mod_ref, ng_ref, win_ref, wuk_ref, wvt_ref, wuq_ref, wwvt_ref, seg_ref, gv_ref, pat_ref,
                  cm_ref, sm_ref, cw_ref, sw_ref, qa_ref, ka_ref, va_ref, qw_ref, kw_ref, vw_ref):
    hb, _ = _pre_mod(x_ref[0], mod_ref, ng_ref, 1)
    tm = hb.shape[0]
    z = _dot(hb, win_ref[...])
    zk = z[:, P0_CKV:P0_CQ]
    cm, sm, cw, sw = cm_ref[...], sm_ref[...], cw_ref[...], sw_ref[...]
    g_kva, g_qa, g_kn2 = gv_ref[0:1, :LANES], gv_ref[1:2, :], gv_ref[2:3, :]
    g_kr, g_q2, g_wq2 = gv_ref[3:4, :LANES], gv_ref[4:5, :], gv_ref[5:6, :]
    g_wk, n_q2 = gv_ref[6:7, :LANES], gv_ref[7:8, :]

    def seg_rinv(v, seg, inv_n):
        return lax.rsqrt(_dot((v * v).astype(BF16), seg) * inv_n + EPS)

    ckv = zk[:, :MLA_KV_RANK]
    ckv = (ckv * lax.rsqrt(jnp.mean(ckv * ckv, axis=-1, keepdims=True) + EPS) * g_kva).astype(BF16)
    kv = _dot(ckv, wuk_ref[...])
    kr = zk[:, MLA_KV_RANK:]
    kr = kr * lax.rsqrt(jnp.sum(kr * kr, axis=-1, keepdims=True) * (1.0 / MLA_ROPE_DIM) + EPS) * g_kr
    kr = _rope(kr, cm, sm)
    g_kn = g_kn2[:, :LANES]
    for h in range(MLA_HEADS):
        cols = slice(h * LANES, (h + 1) * LANES)
        kn = kv[:, cols]
        rinv = lax.rsqrt(jnp.sum(kn * kn, axis=-1, keepdims=True) * (1.0 / MLA_NOPE_DIM) + EPS)
        ka_ref[0, :, cols] = (kn * rinv * g_kn + kr).astype(BF16)

    head0 = (_lane((tm, LANES)) & 32) == 0

    def pair_rinv(v):
        sq = v * v
        s0 = jnp.sum(jnp.where(head0, sq, 0.0), axis=-1, keepdims=True) * (1.0 / WIN_HEAD_DIM)
        s1 = jnp.sum(jnp.where(head0, 0.0, sq), axis=-1, keepdims=True) * (1.0 / WIN_HEAD_DIM)
        return jnp.where(head0, lax.rsqrt(s0 + EPS), lax.rsqrt(s1 + EPS))

    zw = z[:, P0_WQ:P0_WK]
    for p in range(WIN_GROUP // 2):
        w2 = zw[:, p * 2 * LANES:(p + 1) * 2 * LANES]
        w2 = w2 * seg_rinv(w2, seg_ref[1], 1.0 / WIN_HEAD_DIM) * g_wq2
        for h in range(2):
            lo = (2 * p + h) * LANES
            qw_ref[0, :, lo:lo + LANES] = _rope(w2[:, h * LANES:(h + 1) * LANES], cw, sw).astype(BF16)
    wk = z[:, P0_WK:P0_COLS]
    kw_ref[0] = _rope(wk * pair_rinv(wk) * g_wk, cw, sw).astype(BF16)

    cq = z[:, P0_CQ:P0_WQ]
    cq = (cq * lax.rsqrt(jnp.mean(cq * cq, axis=-1, keepdims=True) + EPS) * g_qa).astype(BF16)
    q = _dot(cq, wuq_ref[...])
    for p in range(MLA_HEADS // 2):
        q2 = q[:, p * 2 * LANES:(p + 1) * 2 * LANES]
        q2 = q2 * seg_rinv(q2, seg_ref[0], n_q2) * g_q2
        for h in range(2):
            lo = (2 * p + h) * LANES
            qa_ref[0, :, lo:lo + LANES] = _rope(q2[:, h * LANES:(h + 1) * LANES], cm, sm).astype(BF16)

    va_ref[0, 0] = _ones_row(_dot_nt(wvt_ref[...], ckv), pat_ref)
    vwt = _ones_row(_dot_nt(wwvt_ref[...], hb), pat_ref)
    for c in range(tm // WIN_BLOCK):
        vw_ref[0, c] = vwt[:, c * WIN_BLOCK:(c + 1) * WIN_BLOCK]


def _proj0(x, mod, ng, win, wuk, wvt, wuq, wwvt, seg, gv, pat, tabs):
    bx, t, _ = x.shape
    tm = _row_tile(t, MLA_TK)
    ntab = tabs[0].shape[0] // tm
    tab_spec = pl.BlockSpec((tm, LANES), lambda b, i: (i % ntab, 0))
    nwb = tm // WIN_BLOCK
    out_specs = [_tok_spec(tm, MLA_HEADS * LANES), _tok_spec(tm, MLA_HEADS * LANES),
                 pl.BlockSpec((1, 1, MLA_HEADS * VT_ROWS, tm), lambda b, i: (b, i, 0, 0)),
                 _tok_spec(tm, WIN_HEADS * WIN_HEAD_DIM), _tok_spec(tm, LANES),
                 pl.BlockSpec((1, nwb, WIN_KV_HEADS * VT_ROWS, WIN_BLOCK), lambda b, i: (b, i, 0, 0))]
    out_shape = [jax.ShapeDtypeStruct((bx, t, MLA_HEADS * LANES), BF16),
                 jax.ShapeDtypeStruct((bx, t, MLA_HEADS * LANES), BF16),
                 jax.ShapeDtypeStruct((bx, t // tm, MLA_HEADS * VT_ROWS, tm), BF16),
                 jax.ShapeDtypeStruct((bx, t, WIN_HEADS * WIN_HEAD_DIM), BF16),
                 jax.ShapeDtypeStruct((bx, t, LANES), BF16),
                 jax.ShapeDtypeStruct((bx, t // WIN_BLOCK, WIN_KV_HEADS * VT_ROWS, WIN_BLOCK), BF16)]
    return pl.pallas_call(
        _proj0_kernel,
        grid=(bx, t // tm),
        in_specs=[_tok_spec(tm, D_MODEL), _mod_spec(mod), _resident((3, D_MODEL)), _resident(win.shape),
                  _resident(wuk.shape), _resident(wvt.shape), _resident(wuq.shape), _resident(wwvt.shape),
                  _resident(seg.shape), _resident(gv.shape), _resident(pat.shape)] + [tab_spec] * 4,
        out_specs=out_specs,
        out_shape=out_shape,
        compiler_params=_params("parallel", "parallel"),
        name="proj0",
    )(x, mod, ng, win, wuk, wvt, wuq, wwvt, seg, gv, pat, *tabs)


def _mla_attn_kernel(*refs, has_latent):
    if has_latent:
        q_ref, kc_ref, vc_ref, k_ref, v_ref, o_ref, m_ref, acc_ref, qt_ref, s_ref, mc_ref = refs
    else:
        q_ref, kc_ref, vc_ref, o_ref, m_ref, acc_ref, qt_ref = refs

    def head(hh):
        return slice(hh * LANES, (hh + 1) * LANES)

    def vrows(hh):
        return slice(hh * VT_ROWS, (hh + 1) * VT_ROWS)

    for hh in range(2):
        qt_ref[hh] = q_ref[0, :, head(hh)].astype(F32).T.astype(BF16)

    def scores(kblk, hh):
        return _dot(kblk, qt_ref[hh])

    def slot_cols(slot):
        return slice(slot * LANES, slot * LANES + q_ref.shape[1])

    def qk(j, slot, hh):
        st = scores(k_ref[0, pl.ds(pl.multiple_of(j * MLA_TK, MLA_TK), MLA_TK), head(hh)], hh)
        s_ref[slot, hh, :, slot_cols(slot)] = st
        mc_ref[slot, hh] = jnp.max(st, axis=0, keepdims=True)

    def process(j, slot, hh):
        m_prev = m_ref[hh]
        m_new = jnp.maximum(m_prev, mc_ref[slot, hh])
        p = jnp.exp2(s_ref[slot, hh, :, slot_cols(slot)] - m_new).astype(BF16)
        pv = _dot(v_ref[0, j, vrows(hh), :], p)
        acc_ref[hh] = acc_ref[hh] * jnp.exp2(m_prev - m_new) + pv
        m_ref[hh] = m_new

    def ctx_softmax(st, hh):
        m_new = jnp.max(st, axis=0, keepdims=True)
        p = jnp.exp2(st - m_new).astype(BF16)
        acc_ref[hh] = _dot(vc_ref[0, vrows(hh)], p)
        m_ref[hh] = m_new

    def pair(nxt, cur, slot):
        for hh in range(2):
            qk(nxt, 1 - slot, hh)
            process(cur, slot, hh)

    sts = [scores(kc_ref[0, :, head(hh)], hh) for hh in range(2)]
    if has_latent:
        n = v_ref.shape[1]
        for hh in range(2):
            qk(0, 0, hh)
            ctx_softmax(sts[hh], hh)

        def body(i, carry):
            pair(2 * i + 1, 2 * i, 0)
            pair(2 * i + 2, 2 * i + 1, 1)
            return carry
        lax.fori_loop(0, n // 2 - 1, body, 0)
        pair(n - 1, n - 2, 0)
        for hh in range(2):
            process(n - 1, 1, hh)
    else:
        for hh in range(2):
            ctx_softmax(sts[hh], hh)
    a0, a1 = acc_ref[0], acc_ref[1]
    ot = jnp.concatenate([a0[:MLA_V_DIM] / a0[MLA_V_DIM:MLA_V_DIM + 1],
                          a1[:MLA_V_DIM] / a1[MLA_V_DIM:MLA_V_DIM + 1]], axis=0)
    o_ref[0] = ot.T.astype(BF16)


def _mla_attn(q, kc, vtc, k=None, vt=None):
    bsz, t, _ = q.shape
    nc = kc.shape[1]
    has_latent = k is not None
    tq = _row_tile(t, MLA_TQ)
    in_specs = [pl.BlockSpec((1, tq, 2 * LANES), lambda b, h, i: (b, i, h)),
                pl.BlockSpec((1, nc, 2 * LANES), lambda b, h, i: (b, 0, h)),
                pl.BlockSpec((1, 2 * VT_ROWS, nc), lambda b, h, i: (b, h, 0))]
    args = [q, kc, vtc]
    scratch = [pltpu.VMEM((2, 1, tq), F32), pltpu.VMEM((2, VT_ROWS, tq), F32), pltpu.VMEM((2, LANES, tq), BF16)]
    if has_latent:
        n = k.shape[1]
        nt, _, tk = vt.shape[1:]
        assert tk == MLA_TK and nt * tk == n and nt % 2 == 0, (n, nt, tk)
        in_specs += [pl.BlockSpec((1, n, 2 * LANES), lambda b, h, i: (b, 0, h)),
                     pl.BlockSpec((1, nt, 2 * VT_ROWS, tk), lambda b, h, i: (b, 0, h, 0))]
        args += [k, vt]
        scratch += [pltpu.VMEM((2, 2, tk, tq + LANES), F32), pltpu.VMEM((2, 2, 1, tq), F32)]
    return pl.pallas_call(
        functools.partial(_mla_attn_kernel, has_latent=has_latent),
        grid=(bsz, MLA_HEADS // 2, t // tq),
        in_specs=in_specs,
        out_specs=pl.BlockSpec((1, tq, LANES), lambda b, h, i: (b, i, h)),
        out_shape=jax.ShapeDtypeStruct((bsz, t, MLA_HEADS * MLA_V_DIM), BF16),
        scratch_shapes=scratch,
        compiler_params=_params("parallel", "parallel", "arbitrary"),
        name="mla_attn" if has_latent else "mla_attn_ctx",
    )(*args)


def _win_attn_kernel(*refs, has_window, nb):
    if has_window:
        sink_ref, q_ref, kc_ref, vc_ref, k_ref, vt_ref, bias_ref, o_ref, s_ref, mc_ref = refs
    else:
        sink_ref, q_ref, kc_ref, vc_ref, o_ref, s_ref, mc_ref = refs
    qb = q_ref.shape[1] // WIN_BLOCK
    nc = kc_ref.shape[1]
    nct = vc_ref.shape[1]
    i = pl.program_id(1)
    units = [(blk, n) for blk in range(qb) for n in range(WIN_KV_HEADS)]
    cache = {}

    def block_operands(blk):
        if blk in cache:
            return cache[blk]
        vts = [vc_ref[0, t] for t in range(nct)]
        bias = None
        if has_window:
            g = i * qb + blk
            start = jnp.clip(g - 1, 0, nb - 3)
            kwin = k_ref[0, pl.ds(pl.multiple_of(start * WIN_BLOCK, WIN_BLOCK), 3 * WIN_BLOCK), :]
            kall = jnp.concatenate([kc_ref[0], kwin], axis=0)
            vts += [vt_ref[0, start + t] for t in range(3)]
            bias = jnp.concatenate([bias_ref[g - start]] * WIN_GROUP, axis=1)
        else:
            kall = kc_ref[0]
        rows = slice(blk * WIN_BLOCK, (blk + 1) * WIN_BLOCK)
        qs = jnp.concatenate([q_ref[0, rows, j * LANES:(j + 1) * LANES] for j in range(WIN_GROUP)], axis=0)
        cache[blk] = (kall, jnp.concatenate(vts, axis=1), qs, bias)
        return cache[blk]

    def qk(u, slot):
        blk, n = units[u]
        kall, _, qs, bias = block_operands(blk)
        head0 = (_lane(qs.shape) & 32) == 0
        zero = jnp.zeros_like(qs)
        st = _dot_nt(kall, jnp.where(head0, qs, zero) if n == 0 else jnp.where(head0, zero, qs))
        if bias is not None:
            st = jnp.concatenate([st[:nc], st[nc:] + bias], axis=0)
        s_ref[slot] = st
        mc_ref[slot] = jnp.max(st, axis=0, keepdims=True)

    outs = {}

    def process(u, slot):
        blk, n = units[u]
        vall = block_operands(blk)[1]
        sink = sink_ref[n]
        m = jnp.maximum(mc_ref[slot], sink)
        p = jnp.exp2(s_ref[slot] - m).astype(BF16)
        pv = _dot(vall[n * VT_ROWS:(n + 1) * VT_ROWS], p)
        denom = pv[WIN_HEAD_DIM:WIN_HEAD_DIM + 1] + jnp.exp2(sink - m)
        outs[(blk, n)] = pv[:WIN_HEAD_DIM] / denom
        if n == WIN_KV_HEADS - 1:
            rows = slice(blk * WIN_BLOCK, (blk + 1) * WIN_BLOCK)
            for j in range(WIN_GROUP):
                cols = slice(j * WIN_BLOCK, (j + 1) * WIN_BLOCK)
                both = jnp.concatenate([outs[(blk, 0)][:, cols], outs[(blk, 1)][:, cols]], axis=0)
                o_ref[0, rows, j * LANES:(j + 1) * LANES] = both.T.astype(BF16)

    qk(0, 0)
    for u in range(len(units)):
        if u + 1 < len(units):
            qk(u + 1, (u + 1) % 2)
        process(u, u % 2)


def _band_bias():
    krow = np.arange(3 * WIN_BLOCK)[:, None]
    qcol = np.arange(WIN_BLOCK)[None, :]
    keep = [np.abs(krow - (qcol + off * WIN_BLOCK)) <= WINDOW for off in range(3)]
    return jnp.asarray(np.where(np.stack(keep), 0.0, NEG_INF), F32)


def _win_attn(sinkrow, q, kc, vtc, k=None, vt=None):
    bsz, t, _ = q.shape
    nc = kc.shape[1]
    has_window = k is not None
    nb = t // WIN_BLOCK
    tq = _row_tile(t, WIN_TQ)
    kvw = WIN_KV_HEADS * WIN_HEAD_DIM
    nk = nc + (3 * WIN_BLOCK if has_window else 0)
    in_specs = [pl.BlockSpec((WIN_KV_HEADS, 1, WIN_GROUP * WIN_BLOCK), lambda b, i: (0, 0, 0)),
                pl.BlockSpec((1, tq, WIN_HEADS * WIN_HEAD_DIM), lambda b, i: (b, i, 0)),
                pl.BlockSpec((1, nc, kvw), lambda b, i: (b, 0, 0)),
                pl.BlockSpec((1, nc // WIN_BLOCK, WIN_KV_HEADS * VT_ROWS, WIN_BLOCK), lambda b, i: (b, 0, 0, 0))]
    args = [sinkrow, q, kc, vtc]
    if has_window:
        assert nb >= 3, nb
        in_specs += [pl.BlockSpec((1, t, kvw), lambda b, i: (b, 0, 0)),
                     pl.BlockSpec((1, nb, WIN_KV_HEADS * VT_ROWS, WIN_BLOCK), lambda b, i: (b, 0, 0, 0)),
                     pl.BlockSpec((3, 3 * WIN_BLOCK, WIN_BLOCK), lambda b, i: (0, 0, 0))]
        args += [k, vt, _band_bias()]
    return pl.pallas_call(
        functools.partial(_win_attn_kernel, has_window=has_window, nb=nb),
        grid=(bsz, t // tq),
        in_specs=in_specs,
        out_specs=pl.BlockSpec((1, tq, WIN_HEADS * WIN_HEAD_DIM), lambda b, i: (b, i, 0)),
        out_shape=jax.ShapeDtypeStruct((bsz, t, WIN_HEADS * WIN_HEAD_DIM), BF16),
        scratch_shapes=[pltpu.VMEM((2, nk, WIN_GROUP * WIN_BLOCK), F32),
                        pltpu.VMEM((2, 1, WIN_GROUP * WIN_BLOCK), F32)],
        compiler_params=_params("parallel", "parallel"),
        name="win_attn" if has_window else "win_attn_ctx",
    )(*args)


def _log_sigmoid(x):
    return jnp.minimum(x, 0.0) - jnp.log(1.0 + jnp.exp(-jnp.abs(x)))


def _proj1_kernel(x_ref, mod_ref, ng_ref, win_ref, wgk_ref, bgk_ref, k_ref, v_ref, q_ref, gt_ref, gf_ref, gb_ref):
    hb, _ = _pre_mod(x_ref[0], mod_ref, ng_ref, 1)
    ql = _dot(hb, win_ref[:, P1_Q:P1_LOW + LANES])
    q_ref[0] = ql[:, :GLA_DK] * (GLA_DK_HEAD ** -0.5)
    low = ql[:, GLA_DK:].astype(BF16)
    k_ref[0] = _dot(hb, win_ref[:, P1_K:P1_K + GLA_DK])
    pre_f = _dot(low, wgk_ref[:, :GLA_DK]) + bgk_ref[:, :GLA_DK]
    gf_ref[0] = _log_sigmoid(pre_f) * (1.0 / GLA_GATE_NORM)
    v_ref[0] = _dot(hb, win_ref[:, P1_V:P1_V + GLA_DV]).astype(BF16)
    pre_b = _dot(low, wgk_ref[:, GLA_DK:]) + bgk_ref[:, GLA_DK:]
    gb_ref[0] = _log_sigmoid(pre_b) * (1.0 / GLA_GATE_NORM)
    gt_ref[0] = _silu(_dot(hb, win_ref[:, P1_G:P1_G + GLA_DV]))


def _proj1(x, mod, ng, win, wgk, bgk):
    bx, t, _ = x.shape
    tm = _row_tile(t)
    outs = ((GLA_DK, F32), (GLA_DV, BF16), (GLA_DK, F32), (GLA_DV, F32), (GLA_DK, F32), (GLA_DK, F32))
    return pl.pallas_call(
        _proj1_kernel,
        grid=(bx, t // tm),
        in_specs=[_tok_spec(tm, D_MODEL), _mod_spec(mod), _resident((3, D_MODEL)), _resident(win.shape),
                  _resident(wgk.shape), _resident(bgk.shape)],
        out_specs=[_tok_spec(tm, w) for w, _ in outs],
        out_shape=[jax.ShapeDtypeStruct((bx, t, w), d) for w, d in outs],
        compiler_params=_params("parallel", "parallel"),
        name="proj1",
    )(x, mod, ng, win, wgk, bgk)


def _gla_block(q_ref, k_ref, v_ref, g_ref, o_ref, st_ref, bi, cum, keep, backward):
    t = GLA_BLOCK
    g = g_ref[bi]
    g_hi = g.astype(BF16)
    g_lo = (g - g_hi.astype(F32)).astype(BF16)
    both = _dot(cum, g_hi) + _dot(cum, g_lo)
    bb, tot = both[:t], both[t:]
    k = k_ref[bi]
    q_dec = (q_ref[bi] * jnp.exp(bb)).astype(BF16)
    k_inv = (k * jnp.exp(-bb)).astype(BF16)
    k_end = (k * jnp.exp(tot - bb)).astype(BF16)
    decay = jnp.exp(tot)
    n_sub = t // GLA_CHUNK
    order = range(n_sub - 1, -1, -1) if backward else range(n_sub)
    for h in range(GLA_HEADS):
        ks = slice(h * GLA_DK_HEAD, (h + 1) * GLA_DK_HEAD)
        vs = slice(h * GLA_DV_HEAD, (h + 1) * GLA_DV_HEAD)
        vh = v_ref[bi, :, vs]
        a = jnp.where(keep, _dot_nt(q_dec[:, ks], k_inv[:, ks]), 0.0).astype(BF16)
        o_intra = _dot(a, vh)
        st = st_ref[bi, h]
        for c in order:
            rows = slice(c * GLA_CHUNK, (c + 1) * GLA_CHUNK)
            o_ref[bi, rows, vs] = o_intra[rows] + _dot_nt(q_dec[rows, ks], st.astype(BF16))
            st = st * decay[c * GLA_CHUNK:c * GLA_CHUNK + 1, ks] + _dot_tn(vh[rows], k_end[rows, ks])
        st_ref[bi, h] = st


def _gla_scan_kernel(qf_ref, kf_ref, vf_ref, gf_ref, qb_ref, kb_ref, vb_ref, gb_ref, s0f_ref, s0b_ref, cum_ref,
                     of_ref, ob_ref, sf_ref, sb_ref, stf_ref, stb_ref):
    i = pl.program_id(1)

    @pl.when(i == 0)
    def _():
        stf_ref[...] = s0f_ref[...]
        stb_ref[...] = s0b_ref[...]

    row = lax.broadcasted_iota(jnp.int32, (GLA_BLOCK, GLA_BLOCK), 0)
    col = lax.broadcasted_iota(jnp.int32, (GLA_BLOCK, GLA_BLOCK), 1)
    same = (row // GLA_CHUNK) == (col // GLA_CHUNK)
    for bi in range(qf_ref.shape[0]):
        _gla_block(qf_ref, kf_ref, vf_ref, gf_ref, of_ref, stf_ref, bi, cum_ref[0], same & (col <= row), False)
        _gla_block(qb_ref, kb_ref, vb_ref, gb_ref, ob_ref, stb_ref, bi, cum_ref[1], same & (col >= row), True)

    @pl.when(i == pl.num_programs(1) - 1)
    def _():
        sf_ref[...] = stf_ref[...]
        sb_ref[...] = stb_ref[...]


def _cum_matrices():
    r = np.arange(GLA_BLOCK)[:, None]
    c = np.arange(GLA_BLOCK)[None, :]
    same = (r // GLA_CHUNK) == (c // GLA_CHUNK)
    fwd = np.concatenate([same & (c <= r), same], axis=0)
    bwd = np.concatenate([same & (c >= r), same], axis=0)
    return jnp.asarray(np.stack([fwd, bwd]), BF16)


def _gla_scan(q, k, v, gf, gb, s0f, s0b):
    bsz, t, _ = q.shape
    tb = GLA_BLOCK
    assert t % tb == 0, t
    nblk = t // tb
    nbat = 2 if bsz % 2 == 0 else 1
    fwd = lambda w: pl.BlockSpec((nbat, tb, w), lambda b, i: (b, i, 0))
    bwd = lambda w: pl.BlockSpec((nbat, tb, w), lambda b, i: (b, nblk - 1 - i, 0))
    st_spec = pl.BlockSpec((nbat, GLA_HEADS, GLA_DV_HEAD, GLA_DK_HEAD), lambda b, i: (b, 0, 0, 0))
    st_shape = jax.ShapeDtypeStruct((bsz, GLA_HEADS, GLA_DV_HEAD, GLA_DK_HEAD), F32)
    cum = _cum_matrices()
    return pl.pallas_call(
        _gla_scan_kernel,
        grid=(bsz // nbat, nblk),
        in_specs=[fwd(GLA_DK), fwd(GLA_DK), fwd(GLA_DV), fwd(GLA_DK),
                  bwd(GLA_DK), bwd(GLA_DK), bwd(GLA_DV), bwd(GLA_DK), st_spec, st_spec, _resident(cum.shape)],
        out_specs=[fwd(GLA_DV), bwd(GLA_DV), st_spec, st_spec],
        out_shape=[jax.ShapeDtypeStruct((bsz, t, GLA_DV), F32)] * 2 + [st_shape] * 2,
        scratch_shapes=[pltpu.VMEM((nbat, GLA_HEADS, GLA_DV_HEAD, GLA_DK_HEAD), F32)] * 2,
        compiler_params=_params("parallel", "arbitrary"),
        name="gla_scan",
    )(q, k, v, gf, q, k, v, gb, s0f, s0b, cum)


def _deinterleave(n):
    return np.concatenate([np.arange(0, n, 2), np.arange(1, n, 2)])


def _gather_cols(w, idx):
    idx = np.asarray(idx)
    cols = jnp.take(w, jnp.asarray(np.maximum(idx, 0)), axis=-1)
    return jnp.where(jnp.asarray(idx >= 0), cols, 0.0)


def _mla_slot(nope, rope):
    pad = np.full((16,), -1, np.int64)
    return np.concatenate([nope[:32], rope[:16], pad, nope[32:], rope[16:], pad])


def _win_slot(a, b):
    return np.concatenate([a[:32], b[:32], a[32:], b[32:]])


def _take(vec, idx):
    return jnp.where(jnp.asarray(idx >= 0), vec[jnp.asarray(np.maximum(idx, 0))], 0.0)


def _layer0_layout(p):
    ckv0, kr0 = 0, MLA_KV_RANK
    wk0 = kr0 + MLA_ROPE_DIM
    wv0 = wk0 + WIN_KV_HEADS * WIN_HEAD_DIM
    cq0 = wv0 + WIN_KV_HEADS * WIN_HEAD_DIM
    wq0 = cq0 + MLA_Q_RANK
    de64, de32 = _deinterleave(WIN_HEAD_DIM), _deinterleave(MLA_ROPE_DIM)
    none64, none32 = np.full((64,), -1, np.int64), np.full((32,), -1, np.int64)

    idx = np.full((P0_COLS,), -1, np.int64)
    idx[P0_CKV:P0_CKV + MLA_KV_RANK] = ckv0 + np.arange(MLA_KV_RANK)
    idx[P0_WK:P0_WK + LANES] = _win_slot(wk0 + de64, wk0 + 64 + de64)
    idx[P0_CQ:P0_CQ + MLA_Q_RANK] = cq0 + np.arange(MLA_Q_RANK)
    for j in range(WIN_GROUP):
        idx[P0_WQ + j * LANES:P0_WQ + (j + 1) * LANES] = _win_slot(wq0 + j * 64 + de64,
                                                                   wq0 + (WIN_GROUP + j) * 64 + de64)
    idx[P0_KR:P0_KR + LANES] = _mla_slot(none64, kr0 + de32)
    w_in = _gather_cols(p["w_in"], idx).astype(BF16)
    vpad = np.full((VT_ROWS - MLA_V_DIM,), -1, np.int64)
    vidx = np.concatenate([np.concatenate([wv0 + n * 64 + np.arange(64), vpad]) for n in range(WIN_KV_HEADS)])
    w_wvt = _gather_cols(p["w_in"], vidx).T.astype(BF16)

    per = MLA_NOPE_DIM + MLA_V_DIM
    kidx, qidx, vidx = [], [], []
    for h in range(MLA_HEADS):
        kidx.append(_mla_slot(h * per + np.arange(64), none32))
        qidx.append(_mla_slot(h * MLA_QK_DIM + np.arange(64), h * MLA_QK_DIM + MLA_NOPE_DIM + de32))
        vidx.append(np.concatenate([h * per + MLA_NOPE_DIM + np.arange(64), vpad]))
    w_uk = _gather_cols(p["mla_w_ukv"], np.concatenate(kidx)).astype(BF16)
    w_vt = _gather_cols(p["mla_w_ukv"], np.concatenate(vidx)).T.astype(BF16)
    pat = np.zeros((MLA_HEADS * VT_ROWS, LANES), np.float32)
    pat[MLA_V_DIM::VT_ROWS] = 1.0
    w_uq = _gather_cols(p["mla_w_uq"], np.concatenate(qidx)).astype(BF16)

    lane = np.arange(LANES)
    nope, rot = (lane & 32) == 0, (lane & 48) == 32
    seg_a = (nope[:, None] & nope[None, :]) | (rot[:, None] & rot[None, :])
    seg_b = nope[:, None] == nope[None, :]
    two = lambda m: np.kron(np.eye(2), m.astype(np.float32))
    seg = jnp.asarray(np.stack([two(seg_a), two(seg_b)]), BF16)

    a64, a32 = np.arange(64), np.arange(32)
    twice = lambda v: jnp.concatenate([v, v])
    wide = lambda v: jnp.concatenate([v, jnp.zeros((LANES,), F32)])
    g_wq = _take(p["win_g_q"], _win_slot(de64, de64)) * (WIN_SCALE * LOG2E)
    g_wk = _take(p["win_g_k"], _win_slot(de64, de64))
    g_qslot = jnp.concatenate([p["mla_g_qn"], p["mla_g_qr"]])
    inv_n = jnp.concatenate([jnp.full((64,), 1.0 / MLA_NOPE_DIM, F32), jnp.full((32,), 1.0 / MLA_ROPE_DIM, F32)])
    gv = jnp.stack([
        wide(p["mla_g_kva"]),
        p["mla_g_qa"],
        twice(_take(p["mla_g_kn"], _mla_slot(a64, none32))),
        wide(_take(p["mla_g_kr"], _mla_slot(none64, de32))),
        twice(_take(g_qslot, _mla_slot(a64, 64 + de32))) * (MLA_SCALE * LOG2E),
        twice(g_wq),
        wide(g_wk),
        twice(_take(inv_n, _mla_slot(a64, 64 + a32))),
    ])

    out_rows = np.arange(D_MODEL)
    base = MLA_HEADS * MLA_V_DIM
    for j in range(WIN_GROUP):
        for n in range(WIN_KV_HEADS):
            lo = base + j * LANES + n * 64
            out_rows[lo:lo + 64] = base + (n * WIN_GROUP + j) * 64 + np.arange(64)
    w_out = p["w_out"][jnp.asarray(out_rows)].astype(BF16)
    return (w_in, w_uk, w_vt, w_uq, w_wvt, seg, gv, jnp.asarray(pat)), w_out


def _layer1_layout(p):
    k0, v0 = 0, GLA_DK
    lf0 = v0 + GLA_DV
    lb0 = lf0 + GLA_LOWRANK
    q0 = lb0 + GLA_LOWRANK
    g0 = q0 + GLA_DK
    idx = np.full((P1_COLS,), -1, np.int64)
    idx[P1_K:P1_K + GLA_DK] = k0 + np.arange(GLA_DK)
    idx[P1_V:P1_V + GLA_DV] = v0 + np.arange(GLA_DV)
    idx[P1_Q:P1_Q + GLA_DK] = q0 + np.arange(GLA_DK)
    idx[P1_G:P1_G + GLA_DV] = g0 + np.arange(GLA_DV)
    idx[P1_LOW:P1_LOW + GLA_LOWRANK] = lf0 + np.arange(GLA_LOWRANK)
    idx[P1_LOW + GLA_LOWRANK:P1_LOW + 2 * GLA_LOWRANK] = lb0 + np.arange(GLA_LOWRANK)
    w_in = _gather_cols(p["w_in"], idx).astype(BF16)
    wgk = jnp.zeros((LANES, 2 * GLA_DK), F32)
    wgk = wgk.at[:GLA_LOWRANK, :GLA_DK].set(p["w_gk_f"])
    wgk = wgk.at[GLA_LOWRANK:2 * GLA_LOWRANK, GLA_DK:].set(p["w_gk_b"])
    bgk = jnp.concatenate([p["b_gk_f"], p["b_gk_b"]]).reshape(1, 2 * GLA_DK)
    return w_in, wgk.astype(BF16), bgk


def _rope_tables(seq):
    t = np.arange(seq)
    row = (t // GRID_W).astype(np.float32)
    col = (t % GRID_W).astype(np.float32)

    def angles(rot_dim):
        n_freq = rot_dim // 4
        inv = (ROPE_BASE ** (-np.arange(n_freq, dtype=np.float32) / n_freq)).astype(np.float32)
        return np.concatenate([row[:, None] * inv, col[:, None] * inv], axis=-1)

    am, aw = angles(MLA_ROPE_DIM), angles(WIN_HEAD_DIM)
    cos_m, sin_m, cos_w, sin_w = np.cos(am), np.sin(am), np.cos(aw), np.sin(aw)
    one32, one16 = np.ones((seq, 32)), np.ones((seq, 16))
    cm = np.concatenate([one32, cos_m, one16] * 2, axis=-1)
    sm = np.concatenate([0 * one32, -sin_m, 0 * one16, 0 * one32, sin_m, 0 * one16], axis=-1)
    cw = np.concatenate([cos_w] * 4, axis=-1)
    sw = np.concatenate([-sin_w, -sin_w, sin_w, sin_w], axis=-1)
    return tuple(jnp.asarray(a, F32) for a in (cm, sm, cw, sw))


def _identity_tables(rows):
    one, zero = jnp.ones((rows, LANES), F32), jnp.zeros((rows, LANES), F32)
    return one, zero, one, zero


def kernel(x, c, ctx, c_ctx, l0_norm_g, l0_w_mod, l0_b_mod, l0_ffn1_w_gu, l0_ffn1_w_down, l0_ffn2_w_gu, l0_ffn2_w_down, l0_w_in, l0_mla_g_qa, l0_mla_g_kva, l0_mla_w_uq, l0_mla_w_ukv, l0_mla_g_qn, l0_mla_g_qr, l0_mla_g_kn, l0_mla_g_kr, l0_win_g_q, l0_win_g_k, l0_win_sink, l0_w_out, l1_norm_g, l1_w_mod, l1_b_mod, l1_ffn1_w_gu, l1_ffn1_w_down, l1_ffn2_w_gu, l1_ffn2_w_down, l1_w_in, l1_w_gk_f, l1_b_gk_f, l1_w_gk_b, l1_b_gk_b, l1_g_norm, l1_w_out):
    bsz, seq, _ = x.shape
    n_ctx = ctx.shape[1]
    out_dtype = x.dtype
    x = x.astype(F32)
    xc = ctx.astype(F32)

    cc = jnp.zeros((16, D_MODEL), F32).at[:bsz].set(c).at[bsz].set(c_ctx)

    def mods(w_mod, b_mod):
        tab = _mod_table(cc, w_mod, b_mod).reshape(16, N_MOD, D_MODEL)
        return tab[:bsz], tab[bsz:bsz + 1]

    p0 = dict(w_in=l0_w_in, mla_g_qa=l0_mla_g_qa, mla_g_kva=l0_mla_g_kva, mla_w_uq=l0_mla_w_uq,
              mla_w_ukv=l0_mla_w_ukv, mla_g_qn=l0_mla_g_qn, mla_g_qr=l0_mla_g_qr, mla_g_kn=l0_mla_g_kn,
              mla_g_kr=l0_mla_g_kr, win_g_q=l0_win_g_q, win_g_k=l0_win_g_k, w_out=l0_w_out)
    proj_w, w_out0 = _layer0_layout(p0)
    mod, mod_c = mods(l0_w_mod, l0_b_mod)
    wgu1, wd1 = l0_ffn1_w_gu.astype(BF16), l0_ffn1_w_down.astype(BF16)
    wgu2, wd2 = l0_ffn2_w_gu.astype(BF16), l0_ffn2_w_down.astype(BF16)
    sinkrow = jnp.repeat(l0_win_sink.astype(F32) * LOG2E, WIN_BLOCK, axis=1).reshape(WIN_KV_HEADS, 1, -1)

    x = _ffn(x, mod, l0_norm_g, wgu1, wd1, 0)
    xc = _ffn(xc, mod_c, l0_norm_g, wgu1, wd1, 0)
    qa, ka, vat, qw, kw, vwt = _proj0(x, mod, l0_norm_g, *proj_w, _rope_tables(seq))
    qa_c, ka_c, vat_c, qw_c, kw_c, vwt_c = _proj0(xc, mod_c, l0_norm_g, *proj_w, _identity_tables(n_ctx))
    vat_c = vat_c.reshape(bsz, MLA_HEADS * VT_ROWS, n_ctx)
    oa = _mla_attn(qa, ka_c, vat_c, ka, vat)
    ow = _win_attn(sinkrow, qw, kw_c, vwt_c, kw, vwt)
    oa_c = _mla_attn(qa_c, ka_c, vat_c)
    ow_c = _win_attn(sinkrow, qw_c, kw_c, vwt_c)
    x = _attn_out_ffn(x, oa, ow, mod, l0_norm_g, w_out0, wgu2, wd2)
    xc = _attn_out_ffn(xc, oa_c, ow_c, mod_c, l0_norm_g, w_out0, wgu2, wd2)

    w_in1, wgk, bgk = _layer1_layout(dict(w_in=l1_w_in, w_gk_f=l1_w_gk_f, b_gk_f=l1_b_gk_f,
                                          w_gk_b=l1_w_gk_b, b_gk_b=l1_b_gk_b))
    mod, mod_c = mods(l1_w_mod, l1_b_mod)
    wgu1, wd1 = l1_ffn1_w_gu.astype(BF16), l1_ffn1_w_down.astype(BF16)
    wgu2, wd2 = l1_ffn2_w_gu.astype(BF16), l1_ffn2_w_down.astype(BF16)

    x = _ffn(x, mod, l1_norm_g, wgu1, wd1, 0)
    xc = _ffn(xc, mod_c, l1_norm_g, wgu1, wd1, 0)
    k1, v1, q1, gt1, gf1, gb1 = _proj1(x, mod, l1_norm_g, w_in1, wgk, bgk)
    k1c, v1c, q1c, _, gf1c, gb1c = _proj1(xc, mod_c, l1_norm_g, w_in1, wgk, bgk)
    zero_state = jnp.zeros((bsz, GLA_HEADS, GLA_DV_HEAD, GLA_DK_HEAD), F32)
    _, _, s_f, s_b = _gla_scan(q1c, k1c, v1c, gf1c, gb1c, zero_state, zero_state)
    o_f, o_b, _, _ = _gla_scan(q1, k1, v1, gf1, gb1, s_f, s_b)
    x = _gla_out_ffn(x, o_f, o_b, gt1, l1_g_norm.reshape(1, GLA_DV_HEAD), mod, l1_norm_g,
                     l1_w_out.astype(BF16), wgu2, wd2)
    return x.astype(out_dtype)
```

```python
import functools

import numpy as np
import jax
import jax.numpy as jnp
from jax import lax
from jax.experimental import pallas as pl
from jax.experimental.pallas import tpu as pltpu

F32 = jnp.float32
BF16 = jnp.bfloat16

D_MODEL = 1024
GRID_W = 64
N_MOD = 9
EPS = 1e-6
ROPE_BASE = 10000.0
NEG_INF = -1e30
D_FF = 2816
LOG2E = 1.4426950408889634

MLA_HEADS = 8
MLA_Q_RANK = 256
MLA_KV_RANK = 128
MLA_NOPE_DIM = 64
MLA_ROPE_DIM = 32
MLA_V_DIM = 64
MLA_QK_DIM = MLA_NOPE_DIM + MLA_ROPE_DIM
MLA_SCALE = MLA_QK_DIM ** -0.5

WIN_HEADS = 8
WIN_KV_HEADS = 2
WIN_GROUP = WIN_HEADS // WIN_KV_HEADS
WIN_HEAD_DIM = 64
WINDOW = 128
WIN_BLOCK = 128
WIN_SCALE = WIN_HEAD_DIM ** -0.5

GLA_HEADS = 4
GLA_DK = D_MODEL // 2
GLA_DV = D_MODEL
GLA_DK_HEAD = GLA_DK // GLA_HEADS
GLA_DV_HEAD = GLA_DV // GLA_HEADS
GLA_LOWRANK = 16
GLA_GATE_NORM = 16.0
GLA_CHUNK = 64
GLA_BLOCK = 256

LANES = 128
VT_ROWS = 128
FF_CHUNK = 256
FFN_TM = 1024
FFN_SUB = 512
GLA_OUT_SUB = 512
MLA_TQ = 1024
MLA_TK = 512
WIN_TQ = 1024
VMEM_LIMIT = 58 * 1024 * 1024

P0_CKV, P0_KR, P0_CQ, P0_WQ, P0_WK, P0_COLS = 0, 128, 256, 512, 1024, 1152
P1_K, P1_V, P1_Q, P1_LOW, P1_G, P1_COLS = 0, 512, 1536, 2048, 2176, 3200


def _dot(a, b):
    return jnp.dot(a, b, preferred_element_type=F32)


def _dot_nt(a, b):
    return lax.dot_general(a, b, (((1,), (1,)), ((), ())), preferred_element_type=F32)


def _dot_tn(a, b):
    return lax.dot_general(a, b, (((0,), (0,)), ((), ())), preferred_element_type=F32)


def _params(*sem):
    return pltpu.CompilerParams(dimension_semantics=sem, vmem_limit_bytes=VMEM_LIMIT)


def _resident(shape):
    nd = len(shape)
    return pl.BlockSpec(shape, lambda *_: (0,) * nd, pipeline_mode=pl.Buffered(1))


def _silu(x):
    return x / (1.0 + jnp.exp(-x))


def _pre_mod(x, mod_ref, ng_ref, sub):
    shift = mod_ref[0, 3 * sub:3 * sub + 1, :]
    scale = mod_ref[0, 3 * sub + 1:3 * sub + 2, :]
    gate = mod_ref[0, 3 * sub + 2:3 * sub + 3, :]
    rinv = lax.rsqrt(jnp.mean(x * x, axis=-1, keepdims=True) + EPS)
    h = x * rinv * ng_ref[sub:sub + 1, :] * (1.0 + scale) + shift
    return h.astype(BF16), gate


def _swiglu(hb, wgu_ref, wd_ref, act_ref):
    for c in range(D_FF // FF_CHUNK):
        lo = c * FF_CHUNK
        g = _dot(hb, wgu_ref[:, lo:lo + FF_CHUNK])
        u = _dot(hb, wgu_ref[:, D_FF + lo:D_FF + lo + FF_CHUNK])
        act_ref[:, lo:lo + FF_CHUNK] = (_silu(g) * u).astype(BF16)
    return _dot(act_ref[...], wd_ref[...])


def _mod_kernel(c_ref, w_ref, b_ref, o_ref):
    a = _silu(c_ref[...]).astype(BF16)
    o_ref[...] = _dot(a, w_ref[...].astype(BF16)) + b_ref[...]


def _mod_table(cc, w_mod, b_mod):
    rows = cc.shape[0]
    n = w_mod.shape[1]
    bn = 1152
    return pl.pallas_call(
        _mod_kernel,
        grid=(n // bn,),
        in_specs=[pl.BlockSpec((rows, D_MODEL), lambda j: (0, 0)),
                  pl.BlockSpec((D_MODEL, bn), lambda j: (0, j)),
                  pl.BlockSpec((1, bn), lambda j: (0, j))],
        out_specs=pl.BlockSpec((rows, bn), lambda j: (0, j)),
        out_shape=jax.ShapeDtypeStruct((rows, n), F32),
        compiler_params=_params("arbitrary"),
        name="mod_table",
    )(cc, w_mod, b_mod.reshape(1, n))


def _ffn_kernel(x_ref, mod_ref, ng_ref, wgu_ref, wd_ref, o_ref, act_ref, *, sub):
    for r in range(x_ref.shape[1] // act_ref.shape[1]):
        rows = slice(r * act_ref.shape[1], (r + 1) * act_ref.shape[1])
        x = x_ref[0, rows]
        hb, gate = _pre_mod(x, mod_ref, ng_ref, sub)
        o_ref[0, rows] = x + 0.5 * gate * _swiglu(hb, wgu_ref, wd_ref, act_ref.at[r])


def _attn_out_ffn_kernel(x_ref, oa_ref, ob_ref, mod_ref, ng_ref, wo_ref, wgu_ref, wd_ref, o_ref, act_ref):
    half = oa_ref.shape[-1]
    for r in range(x_ref.shape[1] // act_ref.shape[1]):
        rows = slice(r * act_ref.shape[1], (r + 1) * act_ref.shape[1])
        y = _dot(oa_ref[0, rows], wo_ref[:half, :]) + _dot(ob_ref[0, rows], wo_ref[half:, :])
        x = x_ref[0, rows] + mod_ref[0, 5:6, :] * y
        hb, gate = _pre_mod(x, mod_ref, ng_ref, 2)
        o_ref[0, rows] = x + 0.5 * gate * _swiglu(hb, wgu_ref, wd_ref, act_ref.at[r])


def _gla_out_ffn_kernel(x_ref, of_ref, ob_ref, gt_ref, gn_ref, mod_ref, ng_ref, wo_ref, wgu_ref, wd_ref,
                        o_ref, act_ref, on_ref):
    for r in range(x_ref.shape[1] // act_ref.shape[1]):
        rows = slice(r * act_ref.shape[1], (r + 1) * act_ref.shape[1])
        for h in range(GLA_HEADS):
            hs = slice(h * GLA_DV_HEAD, (h + 1) * GLA_DV_HEAD)
            o = of_ref[0, rows, hs] + ob_ref[0, rows, hs]
            rinv = lax.rsqrt(jnp.mean(o * o, axis=-1, keepdims=True) + EPS)
            on_ref[r, :, hs] = (o * rinv * gn_ref[...] * gt_ref[0, rows, hs]).astype(BF16)
        x = x_ref[0, rows] + mod_ref[0, 5:6, :] * _dot(on_ref[r], wo_ref[...])
        hb, gate = _pre_mod(x, mod_ref, ng_ref, 2)
        o_ref[0, rows] = x + 0.5 * gate * _swiglu(hb, wgu_ref, wd_ref, act_ref.at[r])


def _tok_spec(tm, width):
    return pl.BlockSpec((1, tm, width), lambda b, i: (b, i, 0))


def _mod_spec(mod):
    if mod.shape[0] == 1:
        return pl.BlockSpec((1, N_MOD, D_MODEL), lambda b, i: (0, 0, 0))
    return pl.BlockSpec((1, N_MOD, D_MODEL), lambda b, i: (b, 0, 0))


def _row_tile(t, pref=512):
    return pref if t % pref == 0 else t


def _ffn(x, mod, ng, wgu, wd, sub):
    bx, t, _ = x.shape
    tm = _row_tile(t, FFN_TM)
    ts = _row_tile(tm, FFN_SUB)
    return pl.pallas_call(
        functools.partial(_ffn_kernel, sub=sub),
        grid=(bx, t // tm),
        in_specs=[_tok_spec(tm, D_MODEL), _mod_spec(mod), _resident((3, D_MODEL)),
                  _resident(wgu.shape), _resident(wd.shape)],
        out_specs=_tok_spec(tm, D_MODEL),
        out_shape=jax.ShapeDtypeStruct(x.shape, F32),
        scratch_shapes=[pltpu.VMEM((tm // ts, ts, D_FF), BF16)],
        compiler_params=_params("parallel", "parallel"),
        name="ffn",
    )(x, mod, ng, wgu, wd)


def _attn_out_ffn(x, oa, ob, mod, ng, wo, wgu, wd):
    bx, t, _ = x.shape
    tm = _row_tile(t, FFN_TM)
    ts = _row_tile(tm, FFN_SUB)
    return pl.pallas_call(
        _attn_out_ffn_kernel,
        grid=(bx, t // tm),
        in_specs=[_tok_spec(tm, D_MODEL), _tok_spec(tm, oa.shape[-1]), _tok_spec(tm, ob.shape[-1]),
                  _mod_spec(mod), _resident((3, D_MODEL)), _resident(wo.shape),
                  _resident(wgu.shape), _resident(wd.shape)],
        out_specs=_tok_spec(tm, D_MODEL),
        out_shape=jax.ShapeDtypeStruct(x.shape, F32),
        scratch_shapes=[pltpu.VMEM((tm // ts, ts, D_FF), BF16)],
        compiler_params=_params("parallel", "parallel"),
        name="attn_out_ffn",
    )(x, oa, ob, mod, ng, wo, wgu, wd)


def _gla_out_ffn(x, of, ob, gt, gn, mod, ng, wo, wgu, wd):
    bx, t, _ = x.shape
    tm = _row_tile(t)
    ts = _row_tile(tm, GLA_OUT_SUB)
    return pl.pallas_call(
        _gla_out_ffn_kernel,
        grid=(bx, t // tm),
        in_specs=[_tok_spec(tm, D_MODEL), _tok_spec(tm, GLA_DV), _tok_spec(tm, GLA_DV), _tok_spec(tm, GLA_DV),
                  _resident(gn.shape), _mod_spec(mod), _resident((3, D_MODEL)), _resident(wo.shape),
                  _resident(wgu.shape), _resident(wd.shape)],
        out_specs=_tok_spec(tm, D_MODEL),
        out_shape=jax.ShapeDtypeStruct(x.shape, F32),
        scratch_shapes=[pltpu.VMEM((tm // ts, ts, D_FF), BF16), pltpu.VMEM((tm // ts, ts, GLA_DV), BF16)],
        compiler_params=_params("parallel", "parallel"),
        name="gla_out_ffn",
    )(x, of, ob, gt, gn, mod, ng, wo, wgu, wd)


def _lane(shape):
    return lax.broadcasted_iota(jnp.int32, shape, 1)


def _rope(x, cos, sin):
    return x * cos + pltpu.roll(x, LANES // 2, 1) * sin


def _ones_row(vt, pat_ref):
    pat = pat_ref[:vt.shape[0], :]
    return (vt + jnp.concatenate([pat] * (vt.shape[1] // LANES), axis=1)).astype(BF16)


def _proj0_kernel(x_ref, mod_ref, ng_ref, win_ref, wuk_ref, wvt_ref, wuq_ref, wwvt_ref, seg_ref, gv_ref, pat_ref,
                  cm_ref, sm_ref, cw_ref, sw_ref, qa_ref, ka_ref, va_ref, qw_ref, kw_ref, vw_ref):
    hb, _ = _pre_mod(x_ref[0], mod_ref, ng_ref, 1)
    tm = hb.shape[0]
    z = _dot(hb, win_ref[...])
    zk = z[:, P0_CKV:P0_CQ]
    cm, sm, cw, sw = cm_ref[...], sm_ref[...], cw_ref[...], sw_ref[...]
    g_kva, g_qa, g_kn2 = gv_ref[0:1, :LANES], gv_ref[1:2, :], gv_ref[2:3, :]
    g_kr, g_q2, g_wq2 = gv_ref[3:4, :LANES], gv_ref[4:5, :], gv_ref[5:6, :]
    g_wk, n_q2 = gv_ref[6:7, :LANES], gv_ref[7:8, :]

    def seg_rinv(v, seg, inv_n):
        return lax.rsqrt(_dot((v * v).astype(BF16), seg) * inv_n + EPS)

    ckv = zk[:, :MLA_KV_RANK]
    ckv = (ckv * lax.rsqrt(jnp.mean(ckv * ckv, axis=-1, keepdims=True) + EPS) * g_kva).astype(BF16)
    kv = _dot(ckv, wuk_ref[...])
    kr = zk[:, MLA_KV_RANK:]
    kr = kr * lax.rsqrt(jnp.sum(kr * kr, axis=-1, keepdims=True) * (1.0 / MLA_ROPE_DIM) + EPS) * g_kr
    kr = _rope(kr, cm, sm)
    g_kn = g_kn2[:, :LANES]
    for h in range(MLA_HEADS):
        cols = slice(h * LANES, (h + 1) * LANES)
        kn = kv[:, cols]
        rinv = lax.rsqrt(jnp.sum(kn * kn, axis=-1, keepdims=True) * (1.0 / MLA_NOPE_DIM) + EPS)
        ka_ref[0, :, cols] = (kn * rinv * g_kn + kr).astype(BF16)

    head0 = (_lane((tm, LANES)) & 32) == 0

    def pair_rinv(v):
        sq = v * v
        s0 = jnp.sum(jnp.where(head0, sq, 0.0), axis=-1, keepdims=True) * (1.0 / WIN_HEAD_DIM)
        s1 = jnp.sum(jnp.where(head0, 0.0, sq), axis=-1, keepdims=True) * (1.0 / WIN_HEAD_DIM)
        return jnp.where(head0, lax.rsqrt(s0 + EPS), lax.rsqrt(s1 + EPS))

    zw = z[:, P0_WQ:P0_WK]
    for p in range(WIN_GROUP // 2):
        w2 = zw[:, p * 2 * LANES:(p + 1) * 2 * LANES]
        w2 = w2 * seg_rinv(w2, seg_ref[1], 1.0 / WIN_HEAD_DIM) * g_wq2
        for h in range(2):
            lo = (2 * p + h) * LANES
            qw_ref[0, :, lo:lo + LANES] = _rope(w2[:, h * LANES:(h + 1) * LANES], cw, sw).astype(BF16)
    wk = z[:, P0_WK:P0_COLS]
    kw_ref[0] = _rope(wk * pair_rinv(wk) * g_wk, cw, sw).astype(BF16)

    cq = z[:, P0_CQ:P0_WQ]
    cq = (cq * lax.rsqrt(jnp.mean(cq * cq, axis=-1, keepdims=True) + EPS) * g_qa).astype(BF16)
    q = _dot(cq, wuq_ref[...])
    for p in range(MLA_HEADS // 2):
        q2 = q[:, p * 2 * LANES:(p + 1) * 2 * LANES]
        q2 = q2 * seg_rinv(q2, seg_ref[0], n_q2) * g_q2
        for h in range(2):
            lo = (2 * p + h) * LANES
            qa_ref[0, :, lo:lo + LANES] = _rope(q2[:, h * LANES:(h + 1) * LANES], cm, sm).astype(BF16)

    va_ref[0, 0] = _ones_row(_dot_nt(wvt_ref[...], ckv), pat_ref)
    vwt = _ones_row(_dot_nt(wwvt_ref[...], hb), pat_ref)
    for c in range(tm // WIN_BLOCK):
        vw_ref[0, c] = vwt[:, c * WIN_BLOCK:(c + 1) * WIN_BLOCK]


def _proj0(x, mod, ng, win, wuk, wvt, wuq, wwvt, seg, gv, pat, tabs):
    bx, t, _ = x.shape
    tm = _row_tile(t, MLA_TK)
    ntab = tabs[0].shape[0] // tm
    tab_spec = pl.BlockSpec((tm, LANES), lambda b, i: (i % ntab, 0))
    nwb = tm // WIN_BLOCK
    out_specs = [_tok_spec(tm, MLA_HEADS * LANES), _tok_spec(tm, MLA_HEADS * LANES),
                 pl.BlockSpec((1, 1, MLA_HEADS * VT_ROWS, tm), lambda b, i: (b, i, 0, 0)),
                 _tok_spec(tm, WIN_HEADS * WIN_HEAD_DIM), _tok_spec(tm, LANES),
                 pl.BlockSpec((1, nwb, WIN_KV_HEADS * VT_ROWS, WIN_BLOCK), lambda b, i: (b, i, 0, 0))]
    out_shape = [jax.ShapeDtypeStruct((bx, t, MLA_HEADS * LANES), BF16),
                 jax.ShapeDtypeStruct((bx, t, MLA_HEADS * LANES), BF16),
                 jax.ShapeDtypeStruct((bx, t // tm, MLA_HEADS * VT_ROWS, tm), BF16),
                 jax.ShapeDtypeStruct((bx, t, WIN_HEADS * WIN_HEAD_DIM), BF16),
                 jax.ShapeDtypeStruct((bx, t, LANES), BF16),
                 jax.ShapeDtypeStruct((bx, t // WIN_BLOCK, WIN_KV_HEADS * VT_ROWS, WIN_BLOCK), BF16)]
    return pl.pallas_call(
        _proj0_kernel,
        grid=(bx, t // tm),
        in_specs=[_tok_spec(tm, D_MODEL), _mod_spec(mod), _resident((3, D_MODEL)), _resident(win.shape),
                  _resident(wuk.shape), _resident(wvt.shape), _resident(wuq.shape), _resident(wwvt.shape),
                  _resident(seg.shape), _resident(gv.shape), _resident(pat.shape)] + [tab_spec] * 4,
        out_specs=out_specs,
        out_shape=out_shape,
        compiler_params=_params("parallel", "parallel"),
        name="proj0",
    )(x, mod, ng, win, wuk, wvt, wuq, wwvt, seg, gv, pat, *tabs)


def _mla_attn_kernel(*refs, has_latent):
    if has_latent:
        q_ref, kc_ref, vc_ref, k_ref, v_ref, o_ref, m_ref, acc_ref, qt_ref, s_ref, mc_ref = refs
    else:
        q_ref, kc_ref, vc_ref, o_ref, m_ref, acc_ref, qt_ref = refs

    def head(hh):
        return slice(hh * LANES, (hh + 1) * LANES)

    def vrows(hh):
        return slice(hh * VT_ROWS, (hh + 1) * VT_ROWS)

    for hh in range(2):
        qt_ref[hh] = q_ref[0, :, head(hh)].astype(F32).T.astype(BF16)

    def scores(kblk, hh):
        return _dot(kblk, qt_ref[hh])

    def slot_cols(slot):
        return slice(slot * LANES, slot * LANES + q_ref.shape[1])

    def qk(j, slot, hh):
        st = scores(k_ref[0, pl.ds(pl.multiple_of(j * MLA_TK, MLA_TK), MLA_TK), head(hh)], hh)
        s_ref[slot, hh, :, slot_cols(slot)] = st
        mc_ref[slot, hh] = jnp.max(st, axis=0, keepdims=True)

    def process(j, slot, hh):
        m_prev = m_ref[hh]
        m_new = jnp.maximum(m_prev, mc_ref[slot, hh])
        p = jnp.exp2(s_ref[slot, hh, :, slot_cols(slot)] - m_new).astype(BF16)
        pv = _dot(v_ref[0, j, vrows(hh), :], p)
        acc_ref[hh] = acc_ref[hh] * jnp.exp2(m_prev - m_new) + pv
        m_ref[hh] = m_new

    def ctx_softmax(st, hh):
        m_new = jnp.max(st, axis=0, keepdims=True)
        p = jnp.exp2(st - m_new).astype(BF16)
        acc_ref[hh] = _dot(vc_ref[0, vrows(hh)], p)
        m_ref[hh] = m_new

    def pair(nxt, cur, slot):
        for hh in range(2):
            qk(nxt, 1 - slot, hh)
            process(cur, slot, hh)

    sts = [scores(kc_ref[0, :, head(hh)], hh) for hh in range(2)]
    if has_latent:
        n = v_ref.shape[1]
        for hh in range(2):
            qk(0, 0, hh)
            ctx_softmax(sts[hh], hh)

        def body(i, carry):
            pair(2 * i + 1, 2 * i, 0)
            pair(2 * i + 2, 2 * i + 1, 1)
            return carry
        lax.fori_loop(0, n // 2 - 1, body, 0)
        pair(n - 1, n - 2, 0)
        for hh in range(2):
            process(n - 1, 1, hh)
    else:
        for hh in range(2):
            ctx_softmax(sts[hh], hh)
    a0, a1 = acc_ref[0], acc_ref[1]
    ot = jnp.concatenate([a0[:MLA_V_DIM] / a0[MLA_V_DIM:MLA_V_DIM + 1],
                          a1[:MLA_V_DIM] / a1[MLA_V_DIM:MLA_V_DIM + 1]], axis=0)
    o_ref[0] = ot.T.astype(BF16)


def _mla_attn(q, kc, vtc, k=None, vt=None):
    bsz, t, _ = q.shape
    nc = kc.shape[1]
    has_latent = k is not None
    tq = _row_tile(t, MLA_TQ)
    in_specs = [pl.BlockSpec((1, tq, 2 * LANES), lambda b, h, i: (b, i, h)),
                pl.BlockSpec((1, nc, 2 * LANES), lambda b, h, i: (b, 0, h)),
                pl.BlockSpec((1, 2 * VT_ROWS, nc), lambda b, h, i: (b, h, 0))]
    args = [q, kc, vtc]
    scratch = [pltpu.VMEM((2, 1, tq), F32), pltpu.VMEM((2, VT_ROWS, tq), F32), pltpu.VMEM((2, LANES, tq), BF16)]
    if has_latent:
        n = k.shape[1]
        nt, _, tk = vt.shape[1:]
        assert tk == MLA_TK and nt * tk == n and nt % 2 == 0, (n, nt, tk)
        in_specs += [pl.BlockSpec((1, n, 2 * LANES), lambda b, h, i: (b, 0, h)),
                     pl.BlockSpec((1, nt, 2 * VT_ROWS, tk), lambda b, h, i: (b, 0, h, 0))]
        args += [k, vt]
        scratch += [pltpu.VMEM((2, 2, tk, tq + LANES), F32), pltpu.VMEM((2, 2, 1, tq), F32)]
    return pl.pallas_call(
        functools.partial(_mla_attn_kernel, has_latent=has_latent),
        grid=(bsz, MLA_HEADS // 2, t // tq),
        in_specs=in_specs,
        out_specs=pl.BlockSpec((1, tq, LANES), lambda b, h, i: (b, i, h)),
        out_shape=jax.ShapeDtypeStruct((bsz, t, MLA_HEADS * MLA_V_DIM), BF16),
        scratch_shapes=scratch,
        compiler_params=_params("parallel", "parallel", "arbitrary"),
        name="mla_attn" if has_latent else "mla_attn_ctx",
    )(*args)


def _win_attn_kernel(*refs, has_window, nb):
    if has_window:
        sink_ref, q_ref, kc_ref, vc_ref, k_ref, vt_ref, bias_ref, o_ref, s_ref, mc_ref = refs
    else:
        sink_ref, q_ref, kc_ref, vc_ref, o_ref, s_ref, mc_ref = refs
    qb = q_ref.shape[1] // WIN_BLOCK
    nc = kc_ref.shape[1]
    nct = vc_ref.shape[1]
    i = pl.program_id(1)
    units = [(blk, n) for blk in range(qb) for n in range(WIN_KV_HEADS)]
    cache = {}

    def block_operands(blk):
        if blk in cache:
            return cache[blk]
        vts = [vc_ref[0, t] for t in range(nct)]
        bias = None
        if has_window:
            g = i * qb + blk
            start = jnp.clip(g - 1, 0, nb - 3)
            kwin = k_ref[0, pl.ds(pl.multiple_of(start * WIN_BLOCK, WIN_BLOCK), 3 * WIN_BLOCK), :]
            kall = jnp.concatenate([kc_ref[0], kwin], axis=0)
            vts += [vt_ref[0, start + t] for t in range(3)]
            bias = jnp.concatenate([bias_ref[g - start]] * WIN_GROUP, axis=1)
        else:
            kall = kc_ref[0]
        rows = slice(blk * WIN_BLOCK, (blk + 1) * WIN_BLOCK)
        qs = jnp.concatenate([q_ref[0, rows, j * LANES:(j + 1) * LANES] for j in range(WIN_GROUP)], axis=0)
        cache[blk] = (kall, jnp.concatenate(vts, axis=1), qs, bias)
        return cache[blk]

    def qk(u, slot):
        blk, n = units[u]
        kall, _, qs, bias = block_operands(blk)
        head0 = (_lane(qs.shape) & 32) == 0
        zero = jnp.zeros_like(qs)
        st = _dot_nt(kall, jnp.where(head0, qs, zero) if n == 0 else jnp.where(head0, zero, qs))
        if bias is not None:
            st = jnp.concatenate([st[:nc], st[nc:] + bias], axis=0)
        s_ref[slot] = st
        mc_ref[slot] = jnp.max(st, axis=0, keepdims=True)

    outs = {}

    def process(u, slot):
        blk, n = units[u]
        vall = block_operands(blk)[1]
        sink = sink_ref[n]
        m = jnp.maximum(mc_ref[slot], sink)
        p = jnp.exp2(s_ref[slot] - m).astype(BF16)
        pv = _dot(vall[n * VT_ROWS:(n + 1) * VT_ROWS], p)
        denom = pv[WIN_HEAD_DIM:WIN_HEAD_DIM + 1] + jnp.exp2(sink - m)
        outs[(blk, n)] = pv[:WIN_HEAD_DIM] / denom
        if n == WIN_KV_HEADS - 1:
            rows = slice(blk * WIN_BLOCK, (blk + 1) * WIN_BLOCK)
            for j in range(WIN_GROUP):
                cols = slice(j * WIN_BLOCK, (j + 1) * WIN_BLOCK)
                both = jnp.concatenate([outs[(blk, 0)][:, cols], outs[(blk, 1)][:, cols]], axis=0)
                o_ref[0, rows, j * LANES:(j + 1) * LANES] = both.T.astype(BF16)

    qk(0, 0)
    for u in range(len(units)):
        if u + 1 < len(units):
            qk(u + 1, (u + 1) % 2)
        process(u, u % 2)


def _band_bias():
    krow = np.arange(3 * WIN_BLOCK)[:, None]
    qcol = np.arange(WIN_BLOCK)[None, :]
    keep = [np.abs(krow - (qcol + off * WIN_BLOCK)) <= WINDOW for off in range(3)]
    return jnp.asarray(np.where(np.stack(keep), 0.0, NEG_INF), F32)


def _win_attn(sinkrow, q, kc, vtc, k=None, vt=None):
    bsz, t, _ = q.shape
    nc = kc.shape[1]
    has_window = k is not None
    nb = t // WIN_BLOCK
    tq = _row_tile(t, WIN_TQ)
    kvw = WIN_KV_HEADS * WIN_HEAD_DIM
    nk = nc + (3 * WIN_BLOCK if has_window else 0)
    in_specs = [pl.BlockSpec((WIN_KV_HEADS, 1, WIN_GROUP * WIN_BLOCK), lambda b, i: (0, 0, 0)),
                pl.BlockSpec((1, tq, WIN_HEADS * WIN_HEAD_DIM), lambda b, i: (b, i, 0)),
                pl.BlockSpec((1, nc, kvw), lambda b, i: (b, 0, 0)),
                pl.BlockSpec((1, nc // WIN_BLOCK, WIN_KV_HEADS * VT_ROWS, WIN_BLOCK), lambda b, i: (b, 0, 0, 0))]
    args = [sinkrow, q, kc, vtc]
    if has_window:
        assert nb >= 3, nb
        in_specs += [pl.BlockSpec((1, t, kvw), lambda b, i: (b, 0, 0)),
                     pl.BlockSpec((1, nb, WIN_KV_HEADS * VT_ROWS, WIN_BLOCK), lambda b, i: (b, 0, 0, 0)),
                     pl.BlockSpec((3, 3 * WIN_BLOCK, WIN_BLOCK), lambda b, i: (0, 0, 0))]
        args += [k, vt, _band_bias()]
    return pl.pallas_call(
        functools.partial(_win_attn_kernel, has_window=has_window, nb=nb),
        grid=(bsz, t // tq),
        in_specs=in_specs,
        out_specs=pl.BlockSpec((1, tq, WIN_HEADS * WIN_HEAD_DIM), lambda b, i: (b, i, 0)),
        out_shape=jax.ShapeDtypeStruct((bsz, t, WIN_HEADS * WIN_HEAD_DIM), BF16),
        scratch_shapes=[pltpu.VMEM((2, nk, WIN_GROUP * WIN_BLOCK), F32),
                        pltpu.VMEM((2, 1, WIN_GROUP * WIN_BLOCK), F32)],
        compiler_params=_params("parallel", "parallel"),
        name="win_attn" if has_window else "win_attn_ctx",
    )(*args)


def _log_sigmoid(x):
    return jnp.minimum(x, 0.0) - jnp.log(1.0 + jnp.exp(-jnp.abs(x)))


def _proj1_kernel(x_ref, mod_ref, ng_ref, win_ref, wgk_ref, bgk_ref, k_ref, v_ref, q_ref, gt_ref, gf_ref, gb_ref):
    ts = min(x_ref.shape[1], FFN_SUB)
    for r in range(x_ref.shape[1] // ts):
        rows = slice(r * ts, (r + 1) * ts)
        hb, _ = _pre_mod(x_ref[0, rows], mod_ref, ng_ref, 1)
        ql = _dot(hb, win_ref[:, P1_Q:P1_LOW + LANES])
        q_ref[0, rows] = ql[:, :GLA_DK] * (GLA_DK_HEAD ** -0.5)
        low = ql[:, GLA_DK:].astype(BF16)
        k_ref[0, rows] = _dot(hb, win_ref[:, P1_K:P1_K + GLA_DK])
        pre_f = _dot(low, wgk_ref[:, :GLA_DK]) + bgk_ref[:, :GLA_DK]
        gf_ref[0, rows] = _log_sigmoid(pre_f) * (1.0 / GLA_GATE_NORM)
        v_ref[0, rows] = _dot(hb, win_ref[:, P1_V:P1_V + GLA_DV]).astype(BF16)
        pre_b = _dot(low, wgk_ref[:, GLA_DK:]) + bgk_ref[:, GLA_DK:]
        gb_ref[0, rows] = _log_sigmoid(pre_b) * (1.0 / GLA_GATE_NORM)
        gt_ref[0, rows] = _silu(_dot(hb, win_ref[:, P1_G:P1_G + GLA_DV]))


def _proj1(x, mod, ng, win, wgk, bgk):
    bx, t, _ = x.shape
    tm = _row_tile(t, FFN_TM)
    outs = ((GLA_DK, F32), (GLA_DV, BF16), (GLA_DK, F32), (GLA_DV, F32), (GLA_DK, F32), (GLA_DK, F32))
    return pl.pallas_call(
        _proj1_kernel,
        grid=(bx, t // tm),
        in_specs=[_tok_spec(tm, D_MODEL), _mod_spec(mod), _resident((3, D_MODEL)), _resident(win.shape),
                  _resident(wgk.shape), _resident(bgk.shape)],
        out_specs=[_tok_spec(tm, w) for w, _ in outs],
        out_shape=[jax.ShapeDtypeStruct((bx, t, w), d) for w, d in outs],
        compiler_params=_params("parallel", "parallel"),
        name="proj1",
    )(x, mod, ng, win, wgk, bgk)


def _gla_block(q_ref, k_ref, v_ref, g_ref, o_ref, st_ref, bi, cum, keep, backward):
    t = GLA_BLOCK
    g = g_ref[bi]
    g_hi = g.astype(BF16)
    g_lo = (g - g_hi.astype(F32)).astype(BF16)
    both = _dot(cum, g_hi) + _dot(cum, g_lo)
    bb, tot = both[:t], both[t:]
    k = k_ref[bi]
    q_dec = (q_ref[bi] * jnp.exp(bb)).astype(BF16)
    k_inv = (k * jnp.exp(-bb)).astype(BF16)
    k_end = (k * jnp.exp(tot - bb)).astype(BF16)
    decay = jnp.exp(tot)
    n_sub = t // GLA_CHUNK
    order = range(n_sub - 1, -1, -1) if backward else range(n_sub)
    for h in range(GLA_HEADS):
        ks = slice(h * GLA_DK_HEAD, (h + 1) * GLA_DK_HEAD)
        vs = slice(h * GLA_DV_HEAD, (h + 1) * GLA_DV_HEAD)
        vh = v_ref[bi, :, vs]
        a = jnp.where(keep, _dot_nt(q_dec[:, ks], k_inv[:, ks]), 0.0).astype(BF16)
        o_intra = _dot(a, vh)
        st = st_ref[bi, h]
        for c in order:
            rows = slice(c * GLA_CHUNK, (c + 1) * GLA_CHUNK)
            o_ref[bi, rows, vs] = o_intra[rows] + _dot_nt(q_dec[rows, ks], st.astype(BF16))
            st = st * decay[c * GLA_CHUNK:c * GLA_CHUNK + 1, ks] + _dot_tn(vh[rows], k_end[rows, ks])
        st_ref[bi, h] = st


def _gla_scan_kernel(qf_ref, kf_ref, vf_ref, gf_ref, qb_ref, kb_ref, vb_ref, gb_ref, s0f_ref, s0b_ref, cum_ref,
                     of_ref, ob_ref, sf_ref, sb_ref, stf_ref, stb_ref):
    i = pl.program_id(1)

    @pl.when(i == 0)
    def _():
        stf_ref[...] = s0f_ref[...]
        stb_ref[...] = s0b_ref[...]

    row = lax.broadcasted_iota(jnp.int32, (GLA_BLOCK, GLA_BLOCK), 0)
    col = lax.broadcasted_iota(jnp.int32, (GLA_BLOCK, GLA_BLOCK), 1)
    same = (row // GLA_CHUNK) == (col // GLA_CHUNK)
    for bi in range(qf_ref.shape[0]):
        _gla_block(qf_ref, kf_ref, vf_ref, gf_ref, of_ref, stf_ref, bi, cum_ref[0], same & (col <= row), False)
        _gla_block(qb_ref, kb_ref, vb_ref, gb_ref, ob_ref, stb_ref, bi, cum_ref[1], same & (col >= row), True)

    @pl.when(i == pl.num_programs(1) - 1)
    def _():
        sf_ref[...] = stf_ref[...]
        sb_ref[...] = stb_ref[...]


def _cum_matrices():
    r = np.arange(GLA_BLOCK)[:, None]
    c = np.arange(GLA_BLOCK)[None, :]
    same = (r // GLA_CHUNK) == (c // GLA_CHUNK)
    fwd = np.concatenate([same & (c <= r), same], axis=0)
    bwd = np.concatenate([same & (c >= r), same], axis=0)
    return jnp.asarray(np.stack([fwd, bwd]), BF16)


def _gla_scan(q, k, v, gf, gb, s0f, s0b):
    bsz, t, _ = q.shape
    tb = GLA_BLOCK
    assert t % tb == 0, t
    nblk = t // tb
    nbat = 2 if bsz % 2 == 0 else 1
    fwd = lambda w: pl.BlockSpec((nbat, tb, w), lambda b, i: (b, i, 0))
    bwd = lambda w: pl.BlockSpec((nbat, tb, w), lambda b, i: (b, nblk - 1 - i, 0))
    st_spec = pl.BlockSpec((nbat, GLA_HEADS, GLA_DV_HEAD, GLA_DK_HEAD), lambda b, i: (b, 0, 0, 0))
    st_shape = jax.ShapeDtypeStruct((bsz, GLA_HEADS, GLA_DV_HEAD, GLA_DK_HEAD), F32)
    cum = _cum_matrices()
    return pl.pallas_call(
        _gla_scan_kernel,
        grid=(bsz // nbat, nblk),
        in_specs=[fwd(GLA_DK), fwd(GLA_DK), fwd(GLA_DV), fwd(GLA_DK),
                  bwd(GLA_DK), bwd(GLA_DK), bwd(GLA_DV), bwd(GLA_DK), st_spec, st_spec, _resident(cum.shape)],
        out_specs=[fwd(GLA_DV), bwd(GLA_DV), st_spec, st_spec],
        out_shape=[jax.ShapeDtypeStruct((bsz, t, GLA_DV), F32)] * 2 + [st_shape] * 2,
        scratch_shapes=[pltpu.VMEM((nbat, GLA_HEADS, GLA_DV_HEAD, GLA_DK_HEAD), F32)] * 2,
        compiler_params=_params("parallel", "arbitrary"),
        name="gla_scan",
    )(q, k, v, gf, q, k, v, gb, s0f, s0b, cum)


def _deinterleave(n):
    return np.concatenate([np.arange(0, n, 2), np.arange(1, n, 2)])


def _gather_cols(w, idx):
    idx = np.asarray(idx)
    cols = jnp.take(w, jnp.asarray(np.maximum(idx, 0)), axis=-1)
    return jnp.where(jnp.asarray(idx >= 0), cols, 0.0)


def _mla_slot(nope, rope):
    pad = np.full((16,), -1, np.int64)
    return np.concatenate([nope[:32], rope[:16], pad, nope[32:], rope[16:], pad])


def _win_slot(a, b):
    return np.concatenate([a[:32], b[:32], a[32:], b[32:]])


def _take(vec, idx):
    return jnp.where(jnp.asarray(idx >= 0), vec[jnp.asarray(np.maximum(idx, 0))], 0.0)


def _layer0_layout(p):
    ckv0, kr0 = 0, MLA_KV_RANK
    wk0 = kr0 + MLA_ROPE_DIM
    wv0 = wk0 + WIN_KV_HEADS * WIN_HEAD_DIM
    cq0 = wv0 + WIN_KV_HEADS * WIN_HEAD_DIM
    wq0 = cq0 + MLA_Q_RANK
    de64, de32 = _deinterleave(WIN_HEAD_DIM), _deinterleave(MLA_ROPE_DIM)
    none64, none32 = np.full((64,), -1, np.int64), np.full((32,), -1, np.int64)

    idx = np.full((P0_COLS,), -1, np.int64)
    idx[P0_CKV:P0_CKV + MLA_KV_RANK] = ckv0 + np.arange(MLA_KV_RANK)
    idx[P0_WK:P0_WK + LANES] = _win_slot(wk0 + de64, wk0 + 64 + de64)
    idx[P0_CQ:P0_CQ + MLA_Q_RANK] = cq0 + np.arange(MLA_Q_RANK)
    for j in range(WIN_GROUP):
        idx[P0_WQ + j * LANES:P0_WQ + (j + 1) * LANES] = _win_slot(wq0 + j * 64 + de64,
                                                                   wq0 + (WIN_GROUP + j) * 64 + de64)
    idx[P0_KR:P0_KR + LANES] = _mla_slot(none64, kr0 + de32)
    w_in = _gather_cols(p["w_in"], idx).astype(BF16)
    vpad = np.full((VT_ROWS - MLA_V_DIM,), -1, np.int64)
    vidx = np.concatenate([np.concatenate([wv0 + n * 64 + np.arange(64), vpad]) for n in range(WIN_KV_HEADS)])
    w_wvt = _gather_cols(p["w_in"], vidx).T.astype(BF16)

    per = MLA_NOPE_DIM + MLA_V_DIM
    kidx, qidx, vidx = [], [], []
    for h in range(MLA_HEADS):
        kidx.append(_mla_slot(h * per + np.arange(64), none32))
        qidx.append(_mla_slot(h * MLA_QK_DIM + np.arange(64), h * MLA_QK_DIM + MLA_NOPE_DIM + de32))
        vidx.append(np.concatenate([h * per + MLA_NOPE_DIM + np.arange(64), vpad]))
    w_uk = _gather_cols(p["mla_w_ukv"], np.concatenate(kidx)).astype(BF16)
    w_vt = _gather_cols(p["mla_w_ukv"], np.concatenate(vidx)).T.astype(BF16)
    pat = np.zeros((MLA_HEADS * VT_ROWS, LANES), np.float32)
    pat[MLA_V_DIM::VT_ROWS] = 1.0
    w_uq = _gather_cols(p["mla_w_uq"], np.concatenate(qidx)).astype(BF16)

    lane = np.arange(LANES)
    nope, rot = (lane & 32) == 0, (lane & 48) == 32
    seg_a = (nope[:, None] & nope[None, :]) | (rot[:, None] & rot[None, :])
    seg_b = nope[:, None] == nope[None, :]
    two = lambda m: np.kron(np.eye(2), m.astype(np.float32))
    seg = jnp.asarray(np.stack([two(seg_a), two(seg_b)]), BF16)

    a64, a32 = np.arange(64), np.arange(32)
    twice = lambda v: jnp.concatenate([v, v])
    wide = lambda v: jnp.concatenate([v, jnp.zeros((LANES,), F32)])
    g_wq = _take(p["win_g_q"], _win_slot(de64, de64)) * (WIN_SCALE * LOG2E)
    g_wk = _take(p["win_g_k"], _win_slot(de64, de64))
    g_qslot = jnp.concatenate([p["mla_g_qn"], p["mla_g_qr"]])
    inv_n = jnp.concatenate([jnp.full((64,), 1.0 / MLA_NOPE_DIM, F32), jnp.full((32,), 1.0 / MLA_ROPE_DIM, F32)])
    gv = jnp.stack([
        wide(p["mla_g_kva"]),
        p["mla_g_qa"],
        twice(_take(p["mla_g_kn"], _mla_slot(a64, none32))),
        wide(_take(p["mla_g_kr"], _mla_slot(none64, de32))),
        twice(_take(g_qslot, _mla_slot(a64, 64 + de32))) * (MLA_SCALE * LOG2E),
        twice(g_wq),
        wide(g_wk),
        twice(_take(inv_n, _mla_slot(a64, 64 + a32))),
    ])

    out_rows = np.arange(D_MODEL)
    base = MLA_HEADS * MLA_V_DIM
    for j in range(WIN_GROUP):
        for n in range(WIN_KV_HEADS):
            lo = base + j * LANES + n * 64
            out_rows[lo:lo + 64] = base + (n * WIN_GROUP + j) * 64 + np.arange(64)
    w_out = p["w_out"][jnp.asarray(out_rows)].astype(BF16)
    return (w_in, w_uk, w_vt, w_uq, w_wvt, seg, gv, jnp.asarray(pat)), w_out


def _layer1_layout(p):
    k0, v0 = 0, GLA_DK
    lf0 = v0 + GLA_DV
    lb0 = lf0 + GLA_LOWRANK
    q0 = lb0 + GLA_LOWRANK
    g0 = q0 + GLA_DK
    idx = np.full((P1_COLS,), -1, np.int64)
    idx[P1_K:P1_K + GLA_DK] = k0 + np.arange(GLA_DK)
    idx[P1_V:P1_V + GLA_DV] = v0 + np.arange(GLA_DV)
    idx[P1_Q:P1_Q + GLA_DK] = q0 + np.arange(GLA_DK)
    idx[P1_G:P1_G + GLA_DV] = g0 + np.arange(GLA_DV)
    idx[P1_LOW:P1_LOW + GLA_LOWRANK] = lf0 + np.arange(GLA_LOWRANK)
    idx[P1_LOW + GLA_LOWRANK:P1_LOW + 2 * GLA_LOWRANK] = lb0 + np.arange(GLA_LOWRANK)
    w_in = _gather_cols(p["w_in"], idx).astype(BF16)
    wgk = jnp.zeros((LANES, 2 * GLA_DK), F32)
    wgk = wgk.at[:GLA_LOWRANK, :GLA_DK].set(p["w_gk_f"])
    wgk = wgk.at[GLA_LOWRANK:2 * GLA_LOWRANK, GLA_DK:].set(p["w_gk_b"])
    bgk = jnp.concatenate([p["b_gk_f"], p["b_gk_b"]]).reshape(1, 2 * GLA_DK)
    return w_in, wgk.astype(BF16), bgk


def _rope_tables(seq):
    t = np.arange(seq)
    row = (t // GRID_W).astype(np.float32)
    col = (t % GRID_W).astype(np.float32)

    def angles(rot_dim):
        n_freq = rot_dim // 4
        inv = (ROPE_BASE ** (-np.arange(n_freq, dtype=np.float32) / n_freq)).astype(np.float32)
        return np.concatenate([row[:, None] * inv, col[:, None] * inv], axis=-1)

    am, aw = angles(MLA_ROPE_DIM), angles(WIN_HEAD_DIM)
    cos_m, sin_m, cos_w, sin_w = np.cos(am), np.sin(am), np.cos(aw), np.sin(aw)
    one32, one16 = np.ones((seq, 32)), np.ones((seq, 16))
    cm = np.concatenate([one32, cos_m, one16] * 2, axis=-1)
    sm = np.concatenate([0 * one32, -sin_m, 0 * one16, 0 * one32, sin_m, 0 * one16], axis=-1)
    cw = np.concatenate([cos_w] * 4, axis=-1)
    sw = np.concatenate([-sin_w, -sin_w, sin_w, sin_w], axis=-1)
    return tuple(jnp.asarray(a, F32) for a in (cm, sm, cw, sw))


def _identity_tables(rows):
    one, zero = jnp.ones((rows, LANES), F32), jnp.zeros((rows, LANES), F32)
    return one, zero, one, zero


def kernel(x, c, ctx, c_ctx, l0_norm_g, l0_w_mod, l0_b_mod, l0_ffn1_w_gu, l0_ffn1_w_down, l0_ffn2_w_gu, l0_ffn2_w_down, l0_w_in, l0_mla_g_qa, l0_mla_g_kva, l0_mla_w_uq, l0_mla_w_ukv, l0_mla_g_qn, l0_mla_g_qr, l0_mla_g_kn, l0_mla_g_kr, l0_win_g_q, l0_win_g_k, l0_win_sink, l0_w_out, l1_norm_g, l1_w_mod, l1_b_mod, l1_ffn1_w_gu, l1_ffn1_w_down, l1_ffn2_w_gu, l1_ffn2_w_down, l1_w_in, l1_w_gk_f, l1_b_gk_f, l1_w_gk_b, l1_b_gk_b, l1_g_norm, l1_w_out):
    bsz, seq, _ = x.shape
    n_ctx = ctx.shape[1]
    out_dtype = x.dtype
    x = x.astype(F32)
    xc = ctx.astype(F32)

    cc = jnp.zeros((16, D_MODEL), F32).at[:bsz].set(c).at[bsz].set(c_ctx)

    def mods(w_mod, b_mod):
        tab = _mod_table(cc, w_mod, b_mod).reshape(16, N_MOD, D_MODEL)
        return tab[:bsz], tab[bsz:bsz + 1]

    p0 = dict(w_in=l0_w_in, mla_g_qa=l0_mla_g_qa, mla_g_kva=l0_mla_g_kva, mla_w_uq=l0_mla_w_uq,
              mla_w_ukv=l0_mla_w_ukv, mla_g_qn=l0_mla_g_qn, mla_g_qr=l0_mla_g_qr, mla_g_kn=l0_mla_g_kn,
              mla_g_kr=l0_mla_g_kr, win_g_q=l0_win_g_q, win_g_k=l0_win_g_k, w_out=l0_w_out)
    proj_w, w_out0 = _layer0_layout(p0)
    mod, mod_c = mods(l0_w_mod, l0_b_mod)
    wgu1, wd1 = l0_ffn1_w_gu.astype(BF16), l0_ffn1_w_down.astype(BF16)
    wgu2, wd2 = l0_ffn2_w_gu.astype(BF16), l0_ffn2_w_down.astype(BF16)
    sinkrow = jnp.repeat(l0_win_sink.astype(F32) * LOG2E, WIN_BLOCK, axis=1).reshape(WIN_KV_HEADS, 1, -1)

    x = _ffn(x, mod, l0_norm_g, wgu1, wd1, 0)
    xc = _ffn(xc, mod_c, l0_norm_g, wgu1, wd1, 0)
    qa, ka, vat, qw, kw, vwt = _proj0(x, mod, l0_norm_g, *proj_w, _rope_tables(seq))
    qa_c, ka_c, vat_c, qw_c, kw_c, vwt_c = _proj0(xc, mod_c, l0_norm_g, *proj_w, _identity_tables(n_ctx))
    vat_c = vat_c.reshape(bsz, MLA_HEADS * VT_ROWS, n_ctx)
    oa = _mla_attn(qa, ka_c, vat_c, ka, vat)
    ow = _win_attn(sinkrow, qw, kw_c, vwt_c, kw, vwt)
    oa_c = _mla_attn(qa_c, ka_c, vat_c)
    ow_c = _win_attn(sinkrow, qw_c, kw_c, vwt_c)
    x = _attn_out_ffn(x, oa, ow, mod, l0_norm_g, w_out0, wgu2, wd2)
    xc = _attn_out_ffn(xc, oa_c, ow_c, mod_c, l0_norm_g, w_out0, wgu2, wd2)

    w_in1, wgk, bgk = _layer1_layout(dict(w_in=l1_w_in, w_gk_f=l1_w_gk_f, b_gk_f=l1_b_gk_f,
                                          w_gk_b=l1_w_gk_b, b_gk_b=l1_b_gk_b))
    mod, mod_c = mods(l1_w_mod, l1_b_mod)
    wgu1, wd1 = l1_ffn1_w_gu.astype(BF16), l1_ffn1_w_down.astype(BF16)
    wgu2, wd2 = l1_ffn2_w_gu.astype(BF16), l1_ffn2_w_down.astype(BF16)

    x = _ffn(x, mod, l1_norm_g, wgu1, wd1, 0)
    xc = _ffn(xc, mod_c, l1_norm_g, wgu1, wd1, 0)
    k1, v1, q1, gt1, gf1, gb1 = _proj1(x, mod, l1_norm_g, w_in1, wgk, bgk)
    k1c, v1c, q1c, _, gf1c, gb1c = _proj1(xc, mod_c, l1_norm_g, w_in1, wgk, bgk)
    zero_state = jnp.zeros((bsz, GLA_HEADS, GLA_DV_HEAD, GLA_DK_HEAD), F32)
    _, _, s_f, s_b = _gla_scan(q1c, k1c, v1c, gf1c, gb1c, zero_state, zero_state)
    o_f, o_b, _, _ = _gla_scan(q1, k1, v1, gf1, gb1, s_f, s_b)
    x = _gla_out_ffn(x, o_f, o_b, gt1, l1_g_norm.reshape(1, GLA_DV_HEAD), mod, l1_norm_g,
                     l1_w_out.astype(BF16), wgu2, wd2)
    return x.astype(out_dtype)
```

```python
import functools

import numpy as np
import jax
import jax.numpy as jnp
from jax import lax
from jax.experimental import pallas as pl
from jax.experimental.pallas import tpu as pltpu

F32 = jnp.float32
BF16 = jnp.bfloat16

D_MODEL = 1024
GRID_W = 64
N_MOD = 9
EPS = 1e-6
ROPE_BASE = 10000.0
NEG_INF = -1e30
D_FF = 2816
LOG2E = 1.4426950408889634

MLA_HEADS = 8
MLA_Q_RANK = 256
MLA_KV_RANK = 128
MLA_NOPE_DIM = 64
MLA_ROPE_DIM = 32
MLA_V_DIM = 64
MLA_QK_DIM = MLA_NOPE_DIM + MLA_ROPE_DIM
MLA_SCALE = MLA_QK_DIM ** -0.5

WIN_HEADS = 8
WIN_KV_HEADS = 2
WIN_GROUP = WIN_HEADS // WIN_KV_HEADS
WIN_HEAD_DIM = 64
WINDOW = 128
WIN_BLOCK = 128
WIN_SCALE = WIN_HEAD_DIM ** -0.5

GLA_HEADS = 4
GLA_DK = D_MODEL // 2
GLA_DV = D_MODEL
GLA_DK_HEAD = GLA_DK // GLA_HEADS
GLA_DV_HEAD = GLA_DV // GLA_HEADS
GLA_LOWRANK = 16
GLA_GATE_NORM = 16.0
GLA_CHUNK = 64
GLA_BLOCK = 256

LANES = 128
VT_ROWS = 128
FF_CHUNK = 256
FFN_TM = 1024
FFN_SUB = 512
GLA_OUT_SUB = 512
MLA_TQ = 1024
MLA_TK = 512
WIN_TQ = 1024
VMEM_LIMIT = 58 * 1024 * 1024

P0_CKV, P0_KR, P0_CQ, P0_WQ, P0_WK, P0_COLS = 0, 128, 256, 512, 1024, 1152
P1_K, P1_V, P1_Q, P1_LOW, P1_G, P1_COLS = 0, 512, 1536, 2048, 2176, 3200


def _dot(a, b):
    return jnp.dot(a, b, preferred_element_type=F32)


def _dot_nt(a, b):
    return lax.dot_general(a, b, (((1,), (1,)), ((), ())), preferred_element_type=F32)


def _dot_tn(a, b):
    return lax.dot_general(a, b, (((0,), (0,)), ((), ())), preferred_element_type=F32)


def _params(*sem):
    return pltpu.CompilerParams(dimension_semantics=sem, vmem_limit_bytes=VMEM_LIMIT)


def _resident(shape):
    nd = len(shape)
    return pl.BlockSpec(shape, lambda *_: (0,) * nd, pipeline_mode=pl.Buffered(1))


def _silu(x):
    return x / (1.0 + jnp.exp(-x))


def _pre_mod(x, mod_ref, ng_ref, sub):
    shift = mod_ref[0, 3 * sub:3 * sub + 1, :]
    scale = mod_ref[0, 3 * sub + 1:3 * sub + 2, :]
    gate = mod_ref[0, 3 * sub + 2:3 * sub + 3, :]
    rinv = lax.rsqrt(jnp.mean(x * x, axis=-1, keepdims=True) + EPS)
    h = x * rinv * ng_ref[sub:sub + 1, :] * (1.0 + scale) + shift
    return h.astype(BF16), gate


def _swiglu(hb, wgu_ref, wd_ref, act_ref):
    for c in range(D_FF // FF_CHUNK):
        lo = c * FF_CHUNK
        g = _dot(hb, wgu_ref[:, lo:lo + FF_CHUNK])
        u = _dot(hb, wgu_ref[:, D_FF + lo:D_FF + lo + FF_CHUNK])
        act_ref[:, lo:lo + FF_CHUNK] = (_silu(g) * u).astype(BF16)
    return _dot(act_ref[...], wd_ref[...])


def _mod_kernel(c_ref, w_ref, b_ref, o_ref):
    a = _silu(c_ref[...]).astype(BF16)
    o_ref[...] = _dot(a, w_ref[...].astype(BF16)) + b_ref[...]


def _mod_table(cc, w_mod, b_mod):
    rows = cc.shape[0]
    n = w_mod.shape[1]
    bn = 1152
    return pl.pallas_call(
        _mod_kernel,
        grid=(n // bn,),
        in_specs=[pl.BlockSpec((rows, D_MODEL), lambda j: (0, 0)),
                  pl.BlockSpec((D_MODEL, bn), lambda j: (0, j)),
                  pl.BlockSpec((1, bn), lambda j: (0, j))],
        out_specs=pl.BlockSpec((rows, bn), lambda j: (0, j)),
        out_shape=jax.ShapeDtypeStruct((rows, n), F32),
        compiler_params=_params("arbitrary"),
        name="mod_table",
    )(cc, w_mod, b_mod.reshape(1, n))


def _ffn_kernel(x_ref, mod_ref, ng_ref, wgu_ref, wd_ref, o_ref, act_ref, *, sub):
    for r in range(x_ref.shape[1] // act_ref.shape[1]):
        rows = slice(r * act_ref.shape[1], (r + 1) * act_ref.shape[1])
        x = x_ref[0, rows]
        hb, gate = _pre_mod(x, mod_ref, ng_ref, sub)
        o_ref[0, rows] = x + 0.5 * gate * _swiglu(hb, wgu_ref, wd_ref, act_ref.at[r])


def _attn_out_ffn_kernel(x_ref, oa_ref, ob_ref, mod_ref, ng_ref, wo_ref, wgu_ref, wd_ref, o_ref, act_ref):
    half = oa_ref.shape[-1]
    for r in range(x_ref.shape[1] // act_ref.shape[1]):
        rows = slice(r * act_ref.shape[1], (r + 1) * act_ref.shape[1])
        y = _dot(oa_ref[0, rows], wo_ref[:half, :]) + _dot(ob_ref[0, rows], wo_ref[half:, :])
        x = x_ref[0, rows] + mod_ref[0, 5:6, :] * y
        hb, gate = _pre_mod(x, mod_ref, ng_ref, 2)
        o_ref[0, rows] = x + 0.5 * gate * _swiglu(hb, wgu_ref, wd_ref, act_ref.at[r])


def _gla_out_ffn_kernel(x_ref, of_ref, ob_ref, gt_ref, gn_ref, mod_ref, ng_ref, wo_ref, wgu_ref, wd_ref,
                        o_ref, act_ref, on_ref):
    for r in range(x_ref.shape[1] // act_ref.shape[1]):
        rows = slice(r * act_ref.shape[1], (r + 1) * act_ref.shape[1])
        for h in range(GLA_HEADS):
            hs = slice(h * GLA_DV_HEAD, (h + 1) * GLA_DV_HEAD)
            o = of_ref[0, rows, hs] + ob_ref[0, rows, hs]
            rinv = lax.rsqrt(jnp.mean(o * o, axis=-1, keepdims=True) + EPS)
            on_ref[r, :, hs] = (o * rinv * gn_ref[...] * gt_ref[0, rows, hs]).astype(BF16)
        x = x_ref[0, rows] + mod_ref[0, 5:6, :] * _dot(on_ref[r], wo_ref[...])
        hb, gate = _pre_mod(x, mod_ref, ng_ref, 2)
        o_ref[0, rows] = x + 0.5 * gate * _swiglu(hb, wgu_ref, wd_ref, act_ref.at[r])


def _tok_spec(tm, width):
    return pl.BlockSpec((1, tm, width), lambda b, i: (b, i, 0))


def _mod_spec(mod):
    if mod.shape[0] == 1:
        return pl.BlockSpec((1, N_MOD, D_MODEL), lambda b, i: (0, 0, 0))
    return pl.BlockSpec((1, N_MOD, D_MODEL), lambda b, i: (b, 0, 0))


def _row_tile(t, pref=512):
    return pref if t % pref == 0 else t


def _ffn(x, mod, ng, wgu, wd, sub):
    bx, t, _ = x.shape
    tm = _row_tile(t, FFN_TM)
    ts = _row_tile(tm, FFN_SUB)
    return pl.pallas_call(
        functools.partial(_ffn_kernel, sub=sub),
        grid=(bx, t // tm),
        in_specs=[_tok_spec(tm, D_MODEL), _mod_spec(mod), _resident((3, D_MODEL)),
                  _resident(wgu.shape), _resident(wd.shape)],
        out_specs=_tok_spec(tm, D_MODEL),
        out_shape=jax.ShapeDtypeStruct(x.shape, F32),
        scratch_shapes=[pltpu.VMEM((tm // ts, ts, D_FF), BF16)],
        compiler_params=_params("parallel", "parallel"),
        name="ffn",
    )(x, mod, ng, wgu, wd)


def _attn_out_ffn(x, oa, ob, mod, ng, wo, wgu, wd):
    bx, t, _ = x.shape
    tm = _row_tile(t, FFN_TM)
    ts = _row_tile(tm, FFN_SUB)
    return pl.pallas_call(
        _attn_out_ffn_kernel,
        grid=(bx, t // tm),
        in_specs=[_tok_spec(tm, D_MODEL), _tok_spec(tm, oa.shape[-1]), _tok_spec(tm, ob.shape[-1]),
                  _mod_spec(mod), _resident((3, D_MODEL)), _resident(wo.shape),
                  _resident(wgu.shape), _resident(wd.shape)],
        out_specs=_tok_spec(tm, D_MODEL),
        out_shape=jax.ShapeDtypeStruct(x.shape, F32),
        scratch_shapes=[pltpu.VMEM((tm // ts, ts, D_FF), BF16)],
        compiler_params=_params("parallel", "parallel"),
        name="attn_out_ffn",
    )(x, oa, ob, mod, ng, wo, wgu, wd)


def _gla_out_ffn(x, of, ob, gt, gn, mod, ng, wo, wgu, wd):
    bx, t, _ = x.shape
    tm = _row_tile(t)
    ts = _row_tile(tm, GLA_OUT_SUB)
    return pl.pallas_call(
        _gla_out_ffn_kernel,
        grid=(bx, t // tm),
        in_specs=[_tok_spec(tm, D_MODEL), _tok_spec(tm, GLA_DV), _tok_spec(tm, GLA_DV), _tok_spec(tm, GLA_DV),
                  _resident(gn.shape), _mod_spec(mod), _resident((3, D_MODEL)), _resident(wo.shape),
                  _resident(wgu.shape), _resident(wd.shape)],
        out_specs=_tok_spec(tm, D_MODEL),
        out_shape=jax.ShapeDtypeStruct(x.shape, F32),
        scratch_shapes=[pltpu.VMEM((tm // ts, ts, D_FF), BF16), pltpu.VMEM((tm // ts, ts, GLA_DV), BF16)],
        compiler_params=_params("parallel", "parallel"),
        name="gla_out_ffn",
    )(x, of, ob, gt, gn, mod, ng, wo, wgu, wd)


def _lane(shape):
    return lax.broadcasted_iota(jnp.int32, shape, 1)


def _rope(x, cos, sin):
    return x * cos + pltpu.roll(x, LANES // 2, 1) * sin


def _ones_row(vt, pat_ref):
    pat = pat_ref[:vt.shape[0], :]
    return (vt + jnp.concatenate([pat] * (vt.shape[1] // LANES), axis=1)).astype(BF16)


def _proj0_kernel(x_ref, mod_ref, ng_ref, win_ref, wuk_ref, wvt_ref, wuq_ref, wwvt_ref, seg_ref, gv_ref, pat_ref,
                  cm_ref, sm_ref, cw_ref, sw_ref, qa_ref, ka_ref, va_ref, qw_ref, kw_ref, vw_ref):
    hb, _ = _pre_mod(x_ref[0], mod_ref, ng_ref, 1)
    tm = hb.shape[0]
    z = _dot(hb, win_ref[...])
    zk = z[:, P0_CKV:P0_CQ]
    cm, sm, cw, sw = cm_ref[...], sm_ref[...], cw_ref[...], sw_ref[...]
    g_kva, g_qa, g_kn2 = gv_ref[0:1, :LANES], gv_ref[1:2, :], gv_ref[2:3, :]
    g_kr, g_q2, g_wq2 = gv_ref[3:4, :LANES], gv_ref[4:5, :], gv_ref[5:6, :]
    g_wk, n_q2 = gv_ref[6:7, :LANES], gv_ref[7:8, :]

    def seg_rinv(v, seg, inv_n):
        return lax.rsqrt(_dot((v * v).astype(BF16), seg) * inv_n + EPS)

    ckv = zk[:, :MLA_KV_RANK]
    ckv = (ckv * lax.rsqrt(jnp.mean(ckv * ckv, axis=-1, keepdims=True) + EPS) * g_kva).astype(BF16)
    kv = _dot(ckv, wuk_ref[...])
    kr = zk[:, MLA_KV_RANK:]
    kr = kr * lax.rsqrt(jnp.sum(kr * kr, axis=-1, keepdims=True) * (1.0 / MLA_ROPE_DIM) + EPS) * g_kr
    kr = _rope(kr, cm, sm)
    g_kn = g_kn2[:, :LANES]
    for h in range(MLA_HEADS):
        cols = slice(h * LANES, (h + 1) * LANES)
        kn = kv[:, cols]
        rinv = lax.rsqrt(jnp.sum(kn * kn, axis=-1, keepdims=True) * (1.0 / MLA_NOPE_DIM) + EPS)
        ka_ref[0, :, cols] = (kn * rinv * g_kn + kr).astype(BF16)

    head0 = (_lane((tm, LANES)) & 32) == 0

    def pair_rinv(v):
        sq = v * v
        s0 = jnp.sum(jnp.where(head0, sq, 0.0), axis=-1, keepdims=True) * (1.0 / WIN_HEAD_DIM)
        s1 = jnp.sum(jnp.where(head0, 0.0, sq), axis=-1, keepdims=True) * (1.0 / WIN_HEAD_DIM)
        return jnp.where(head0, lax.rsqrt(s0 + EPS), lax.rsqrt(s1 + EPS))

    zw = z[:, P0_WQ:P0_WK]
    for p in range(WIN_GROUP // 2):
        w2 = zw[:, p * 2 * LANES:(p + 1) * 2 * LANES]
        w2 = w2 * seg_rinv(w2, seg_ref[1], 1.0 / WIN_HEAD_DIM) * g_wq2
        for h in range(2):
            lo = (2 * p + h) * LANES
            qw_ref[0, :, lo:lo + LANES] = _rope(w2[:, h * LANES:(h + 1) * LANES], cw, sw).astype(BF16)
    wk = z[:, P0_WK:P0_COLS]
    kw_ref[0] = _rope(wk * pair_rinv(wk) * g_wk, cw, sw).astype(BF16)

    cq = z[:, P0_CQ:P0_WQ]
    cq = (cq * lax.rsqrt(jnp.mean(cq * cq, axis=-1, keepdims=True) + EPS) * g_qa).astype(BF16)
    q = _dot(cq, wuq_ref[...])
    for p in range(MLA_HEADS // 2):
        q2 = q[:, p * 2 * LANES:(p + 1) * 2 * LANES]
        q2 = q2 * seg_rinv(q2, seg_ref[0], n_q2) * g_q2
        for h in range(2):
            lo = (2 * p + h) * LANES
            qa_ref[0, :, lo:lo + LANES] = _rope(q2[:, h * LANES:(h + 1) * LANES], cm, sm).astype(BF16)

    va_ref[0, 0] = _ones_row(_dot_nt(wvt_ref[...], ckv), pat_ref)
    vwt = _ones_row(_dot_nt(wwvt_ref[...], hb), pat_ref)
    for c in range(tm // WIN_BLOCK):
        vw_ref[0, c] = vwt[:, c * WIN_BLOCK:(c + 1) * WIN_BLOCK]


def _proj0(x, mod, ng, win, wuk, wvt, wuq, wwvt, seg, gv, pat, tabs):
    bx, t, _ = x.shape
    tm = _row_tile(t, MLA_TK)
    ntab = tabs[0].shape[0] // tm
    tab_spec = pl.BlockSpec((tm, LANES), lambda b, i: (i % ntab, 0))
    nwb = tm // WIN_BLOCK
    out_specs = [_tok_spec(tm, MLA_HEADS * LANES), _tok_spec(tm, MLA_HEADS * LANES),
                 pl.BlockSpec((1, 1, MLA_HEADS * VT_ROWS, tm), lambda b, i: (b, i, 0, 0)),
                 _tok_spec(tm, WIN_HEADS * WIN_HEAD_DIM), _tok_spec(tm, LANES),
                 pl.BlockSpec((1, nwb, WIN_KV_HEADS * VT_ROWS, WIN_BLOCK), lambda b, i: (b, i, 0, 0))]
    out_shape = [jax.ShapeDtypeStruct((bx, t, MLA_HEADS * LANES), BF16),
                 jax.ShapeDtypeStruct((bx, t, MLA_HEADS * LANES), BF16),
                 jax.ShapeDtypeStruct((bx, t // tm, MLA_HEADS * VT_ROWS, tm), BF16),
                 jax.ShapeDtypeStruct((bx, t, WIN_HEADS * WIN_HEAD_DIM), BF16),
                 jax.ShapeDtypeStruct((bx, t, LANES), BF16),
                 jax.ShapeDtypeStruct((bx, t // WIN_BLOCK, WIN_KV_HEADS * VT_ROWS, WIN_BLOCK), BF16)]
    return pl.pallas_call(
        _proj0_kernel,
        grid=(bx, t // tm),
        in_specs=[_tok_spec(tm, D_MODEL), _mod_spec(mod), _resident((3, D_MODEL)), _resident(win.shape),
                  _resident(wuk.shape), _resident(wvt.shape), _resident(wuq.shape), _resident(wwvt.shape),
                  _resident(seg.shape), _resident(gv.shape), _resident(pat.shape)] + [tab_spec] * 4,
        out_specs=out_specs,
        out_shape=out_shape,
        compiler_params=_params("parallel", "parallel"),
        name="proj0",
    )(x, mod, ng, win, wuk, wvt, wuq, wwvt, seg, gv, pat, *tabs)


def _mla_attn_kernel(*refs, has_latent):
    if has_latent:
        q_ref, kc_ref, vc_ref, k_ref, v_ref, o_ref, m_ref, acc_ref, qt_ref, s_ref, mc_ref = refs
    else:
        q_ref, kc_ref, vc_ref, o_ref, m_ref, acc_ref, qt_ref = refs

    def head(hh):
        return slice(hh * LANES, (hh + 1) * LANES)

    def vrows(hh):
        return slice(hh * VT_ROWS, (hh + 1) * VT_ROWS)

    for hh in range(2):
        qt_ref[hh] = q_ref[0, :, head(hh)].astype(F32).T.astype(BF16)

    def scores(kblk, hh):
        return _dot(kblk, qt_ref[hh])

    def slot_cols(slot):
        return slice(slot * LANES, slot * LANES + q_ref.shape[1])

    def qk(j, slot, hh):
        st = scores(k_ref[0, pl.ds(pl.multiple_of(j * MLA_TK, MLA_TK), MLA_TK), head(hh)], hh)
        s_ref[slot, hh, :, slot_cols(slot)] = st
        mc_ref[slot, hh] = jnp.max(st, axis=0, keepdims=True)

    def process(j, slot, hh):
        m_prev = m_ref[hh]
        m_new = jnp.maximum(m_prev, mc_ref[slot, hh])
        p = jnp.exp2(s_ref[slot, hh, :, slot_cols(slot)] - m_new).astype(BF16)
        pv = _dot(v_ref[0, j, vrows(hh), :], p)
        acc_ref[hh] = acc_ref[hh] * jnp.exp2(m_prev - m_new) + pv
        m_ref[hh] = m_new

    def ctx_softmax(st, hh):
        m_new = jnp.max(st, axis=0, keepdims=True)
        p = jnp.exp2(st - m_new).astype(BF16)
        acc_ref[hh] = _dot(vc_ref[0, vrows(hh)], p)
        m_ref[hh] = m_new

    def pair(nxt, cur, slot):
        for hh in range(2):
            qk(nxt, 1 - slot, hh)
            process(cur, slot, hh)

    sts = [scores(kc_ref[0, :, head(hh)], hh) for hh in range(2)]
    if has_latent:
        n = v_ref.shape[1]
        for hh in range(2):
            qk(0, 0, hh)
            ctx_softmax(sts[hh], hh)

        def body(i, carry):
            pair(2 * i + 1, 2 * i, 0)
            pair(2 * i + 2, 2 * i + 1, 1)
            return carry
        lax.fori_loop(0, n // 2 - 1, body, 0)
        pair(n - 1, n - 2, 0)
        for hh in range(2):
            process(n - 1, 1, hh)
    else:
        for hh in range(2):
            ctx_softmax(sts[hh], hh)
    a0, a1 = acc_ref[0], acc_ref[1]
    ot = jnp.concatenate([a0[:MLA_V_DIM] / a0[MLA_V_DIM:MLA_V_DIM + 1],
                          a1[:MLA_V_DIM] / a1[MLA_V_DIM:MLA_V_DIM + 1]], axis=0)
    o_ref[0] = ot.T.astype(BF16)


def _mla_attn(q, kc, vtc, k=None, vt=None):
    bsz, t, _ = q.shape
    nc = kc.shape[1]
    has_latent = k is not None
    tq = _row_tile(t, MLA_TQ)
    in_specs = [pl.BlockSpec((1, tq, 2 * LANES), lambda b, h, i: (b, i, h)),
                pl.BlockSpec((1, nc, 2 * LANES), lambda b, h, i: (b, 0, h)),
                pl.BlockSpec((1, 2 * VT_ROWS, nc), lambda b, h, i: (b, h, 0))]
    args = [q, kc, vtc]
    scratch = [pltpu.VMEM((2, 1, tq), F32), pltpu.VMEM((2, VT_ROWS, tq), F32), pltpu.VMEM((2, LANES, tq), BF16)]
    if has_latent:
        n = k.shape[1]
        nt, _, tk = vt.shape[1:]
        assert tk == MLA_TK and nt * tk == n and nt % 2 == 0, (n, nt, tk)
        in_specs += [pl.BlockSpec((1, n, 2 * LANES), lambda b, h, i: (b, 0, h)),
                     pl.BlockSpec((1, nt, 2 * VT_ROWS, tk), lambda b, h, i: (b, 0, h, 0))]
        args += [k, vt]
        scratch += [pltpu.VMEM((2, 2, tk, tq + LANES), F32), pltpu.VMEM((2, 2, 1, tq), F32)]
    return pl.pallas_call(
        functools.partial(_mla_attn_kernel, has_latent=has_latent),
        grid=(bsz, MLA_HEADS // 2, t // tq),
        in_specs=in_specs,
        out_specs=pl.BlockSpec((1, tq, LANES), lambda b, h, i: (b, i, h)),
        out_shape=jax.ShapeDtypeStruct((bsz, t, MLA_HEADS * MLA_V_DIM), BF16),
        scratch_shapes=scratch,
        compiler_params=_params("parallel", "parallel", "arbitrary"),
        name="mla_attn" if has_latent else "mla_attn_ctx",
    )(*args)


def _win_attn_kernel(*refs, has_window, nb):
    if has_window:
        sink_ref, q_ref, kc_ref, vc_ref, k_ref, vt_ref, bias_ref, o_ref, s_ref, mc_ref = refs
    else:
        sink_ref, q_ref, kc_ref, vc_ref, o_ref, s_ref, mc_ref = refs
    qb = q_ref.shape[1] // WIN_BLOCK
    nc = kc_ref.shape[1]
    nct = vc_ref.shape[1]
    i = pl.program_id(1)
    units = [(blk, n) for blk in range(qb) for n in range(WIN_KV_HEADS)]
    cache = {}

    def block_operands(blk):
        if blk in cache:
            return cache[blk]
        vts = [vc_ref[0, t] for t in range(nct)]
        bias = None
        if has_window:
            g = i * qb + blk
            start = jnp.clip(g - 1, 0, nb - 3)
            kwin = k_ref[0, pl.ds(pl.multiple_of(start * WIN_BLOCK, WIN_BLOCK), 3 * WIN_BLOCK), :]
            kall = jnp.concatenate([kc_ref[0], kwin], axis=0)
            vts += [vt_ref[0, start + t] for t in range(3)]
            bias = jnp.concatenate([bias_ref[g - start]] * WIN_GROUP, axis=1)
        else:
            kall = kc_ref[0]
        rows = slice(blk * WIN_BLOCK, (blk + 1) * WIN_BLOCK)
        qs = jnp.concatenate([q_ref[0, rows, j * LANES:(j + 1) * LANES] for j in range(WIN_GROUP)], axis=0)
        cache[blk] = (kall, jnp.concatenate(vts, axis=1), qs, bias)
        return cache[blk]

    def qk(u, slot):
        blk, n = units[u]
        kall, _, qs, bias = block_operands(blk)
        head0 = (_lane(qs.shape) & 32) == 0
        zero = jnp.zeros_like(qs)
        st = _dot_nt(kall, jnp.where(head0, qs, zero) if n == 0 else jnp.where(head0, zero, qs))
        if bias is not None:
            st = jnp.concatenate([st[:nc], st[nc:] + bias], axis=0)
        s_ref[slot] = st
        mc_ref[slot] = jnp.max(st, axis=0, keepdims=True)

    outs = {}

    def process(u, slot):
        blk, n = units[u]
        vall = block_operands(blk)[1]
        sink = sink_ref[n]
        m = jnp.maximum(mc_ref[slot], sink)
        p = jnp.exp2(s_ref[slot] - m).astype(BF16)
        pv = _dot(vall[n * VT_ROWS:(n + 1) * VT_ROWS], p)
        denom = pv[WIN_HEAD_DIM:WIN_HEAD_DIM + 1] + jnp.exp2(sink - m)
        outs[(blk, n)] = pv[:WIN_HEAD_DIM] / denom
        if n == WIN_KV_HEADS - 1:
            rows = slice(blk * WIN_BLOCK, (blk + 1) * WIN_BLOCK)
            for j in range(WIN_GROUP):
                cols = slice(j * WIN_BLOCK, (j + 1) * WIN_BLOCK)
                both = jnp.concatenate([outs[(blk, 0)][:, cols], outs[(blk, 1)][:, cols]], axis=0)
                o_ref[0, rows, j * LANES:(j + 1) * LANES] = both.T.astype(BF16)

    qk(0, 0)
    for u in range(len(units)):
        if u + 1 < len(units):
            qk(u + 1, (u + 1) % 2)
        process(u, u % 2)


def _band_bias():
    krow = np.arange(3 * WIN_BLOCK)[:, None]
    qcol = np.arange(WIN_BLOCK)[None, :]
    keep = [np.abs(krow - (qcol + off * WIN_BLOCK)) <= WINDOW for off in range(3)]
    return jnp.asarray(np.where(np.stack(keep), 0.0, NEG_INF), F32)


def _win_attn(sinkrow, q, kc, vtc, k=None, vt=None):
    bsz, t, _ = q.shape
    nc = kc.shape[1]
    has_window = k is not None
    nb = t // WIN_BLOCK
    tq = _row_tile(t, WIN_TQ)
    kvw = WIN_KV_HEADS * WIN_HEAD_DIM
    nk = nc + (3 * WIN_BLOCK if has_window else 0)
    in_specs = [pl.BlockSpec((WIN_KV_HEADS, 1, WIN_GROUP * WIN_BLOCK), lambda b, i: (0, 0, 0)),
                pl.BlockSpec((1, tq, WIN_HEADS * WIN_HEAD_DIM), lambda b, i: (b, i, 0)),
                pl.BlockSpec((1, nc, kvw), lambda b, i: (b, 0, 0)),
                pl.BlockSpec((1, nc // WIN_BLOCK, WIN_KV_HEADS * VT_ROWS, WIN_BLOCK), lambda b, i: (b, 0, 0, 0))]
    args = [sinkrow, q, kc, vtc]
    if has_window:
        assert nb >= 3, nb
        in_specs += [pl.BlockSpec((1, t, kvw), lambda b, i: (b, 0, 0)),
                     pl.BlockSpec((1, nb, WIN_KV_HEADS * VT_ROWS, WIN_BLOCK), lambda b, i: (b, 0, 0, 0)),
                     pl.BlockSpec((3, 3 * WIN_BLOCK, WIN_BLOCK), lambda b, i: (0, 0, 0))]
        args += [k, vt, _band_bias()]
    return pl.pallas_call(
        functools.partial(_win_attn_kernel, has_window=has_window, nb=nb),
        grid=(bsz, t // tq),
        in_specs=in_specs,
        out_specs=pl.BlockSpec((1, tq, WIN_HEADS * WIN_HEAD_DIM), lambda b, i: (b, i, 0)),
        out_shape=jax.ShapeDtypeStruct((bsz, t, WIN_HEADS * WIN_HEAD_DIM), BF16),
        scratch_shapes=[pltpu.VMEM((2, nk, WIN_GROUP * WIN_BLOCK), F32),
                        pltpu.VMEM((2, 1, WIN_GROUP * WIN_BLOCK), F32)],
        compiler_params=_params("parallel", "parallel"),
        name="win_attn" if has_window else "win_attn_ctx",
    )(*args)


def _log_sigmoid(x):
    return jnp.minimum(x, 0.0) - jnp.log(1.0 + jnp.exp(-jnp.abs(x)))


def _proj1_kernel(x_ref, mod_ref, ng_ref, win_ref, wgk_ref, bgk_ref, k_ref, v_ref, q_ref, gt_ref, gf_ref, gb_ref):
    ts = min(x_ref.shape[1], FFN_SUB)
    for r in range(x_ref.shape[1] // ts):
        rows = slice(r * ts, (r + 1) * ts)
        hb, _ = _pre_mod(x_ref[0, rows], mod_ref, ng_ref, 1)
        ql = _dot(hb, win_ref[:, P1_Q:P1_LOW + LANES])
        q_ref[0, rows] = ql[:, :GLA_DK] * (GLA_DK_HEAD ** -0.5)
        low = ql[:, GLA_DK:].astype(BF16)
        k_ref[0, rows] = _dot(hb, win_ref[:, P1_K:P1_K + GLA_DK])
        pre_f = _dot(low, wgk_ref[:, :GLA_DK]) + bgk_ref[:, :GLA_DK]
        gf_ref[0, rows] = _log_sigmoid(pre_f) * (1.0 / GLA_GATE_NORM)
        v_ref[0, rows] = _dot(hb, win_ref[:, P1_V:P1_V + GLA_DV]).astype(BF16)
        pre_b = _dot(low, wgk_ref[:, GLA_DK:]) + bgk_ref[:, GLA_DK:]
        gb_ref[0, rows] = _log_sigmoid(pre_b) * (1.0 / GLA_GATE_NORM)
        gt_ref[0, rows] = _silu(_dot(hb, win_ref[:, P1_G:P1_G + GLA_DV]))


def _proj1(x, mod, ng, win, wgk, bgk):
    bx, t, _ = x.shape
    tm = _row_tile(t, FFN_TM)
    outs = ((GLA_DK, F32), (GLA_DV, BF16), (GLA_DK, F32), (GLA_DV, F32), (GLA_DK, F32), (GLA_DK, F32))
    return pl.pallas_call(
        _proj1_kernel,
        grid=(bx, t // tm),
        in_specs=[_tok_spec(tm, D_MODEL), _mod_spec(mod), _resident((3, D_MODEL)), _resident(win.shape),
                  _resident(wgk.shape), _resident(bgk.shape)],
        out_specs=[_tok_spec(tm, w) for w, _ in outs],
        out_shape=[jax.ShapeDtypeStruct((bx, t, w), d) for w, d in outs],
        compiler_params=_params("parallel", "parallel"),
        name="proj1",
    )(x, mod, ng, win, wgk, bgk)


def _gla_block(q_ref, k_ref, v_ref, g_ref, o_ref, st_ref, bi, cum, keep, backward):
    t = GLA_BLOCK
    g = g_ref[bi]
    g_hi = g.astype(BF16)
    g_lo = (g - g_hi.astype(F32)).astype(BF16)
    both = _dot(cum, g_hi) + _dot(cum, g_lo)
    bb, tot = both[:t], both[t:]
    k = k_ref[bi]
    q_dec = (q_ref[bi] * jnp.exp(bb)).astype(BF16)
    k_inv = (k * jnp.exp(-bb)).astype(BF16)
    k_end = (k * jnp.exp(tot - bb)).astype(BF16)
    decay = jnp.exp(tot)
    n_sub = t // GLA_CHUNK
    order = range(n_sub - 1, -1, -1) if backward else range(n_sub)
    for h in range(GLA_HEADS):
        ks = slice(h * GLA_DK_HEAD, (h + 1) * GLA_DK_HEAD)
        vs = slice(h * GLA_DV_HEAD, (h + 1) * GLA_DV_HEAD)
        vh = v_ref[bi, :, vs]
        a = jnp.where(keep, _dot_nt(q_dec[:, ks], k_inv[:, ks]), 0.0).astype(BF16)
        o_intra = _dot(a, vh)
        st = st_ref[bi, h]
        for c in order:
            rows = slice(c * GLA_CHUNK, (c + 1) * GLA_CHUNK)
            o_ref[bi, rows, vs] = o_intra[rows] + _dot_nt(q_dec[rows, ks], st.astype(BF16))
            st = st * decay[c * GLA_CHUNK:c * GLA_CHUNK + 1, ks] + _dot_tn(vh[rows], k_end[rows, ks])
        st_ref[bi, h] = st


def _gla_scan_kernel(qf_ref, kf_ref, vf_ref, gf_ref, qb_ref, kb_ref, vb_ref, gb_ref, s0f_ref, s0b_ref, cum_ref,
                     of_ref, ob_ref, sf_ref, sb_ref, stf_ref, stb_ref):
    i = pl.program_id(1)

    @pl.when(i == 0)
    def _():
        stf_ref[...] = s0f_ref[...]
        stb_ref[...] = s0b_ref[...]

    row = lax.broadcasted_iota(jnp.int32, (GLA_BLOCK, GLA_BLOCK), 0)
    col = lax.broadcasted_iota(jnp.int32, (GLA_BLOCK, GLA_BLOCK), 1)
    same = (row // GLA_CHUNK) == (col // GLA_CHUNK)
    for bi in range(qf_ref.shape[0]):
        _gla_block(qf_ref, kf_ref, vf_ref, gf_ref, of_ref, stf_ref, bi, cum_ref[0], same & (col <= row), False)
        _gla_block(qb_ref, kb_ref, vb_ref, gb_ref, ob_ref, stb_ref, bi, cum_ref[1], same & (col >= row), True)

    @pl.when(i == pl.num_programs(1) - 1)
    def _():
        sf_ref[...] = stf_ref[...]
        sb_ref[...] = stb_ref[...]


def _cum_matrices():
    r = np.arange(GLA_BLOCK)[:, None]
    c = np.arange(GLA_BLOCK)[None, :]
    same = (r // GLA_CHUNK) == (c // GLA_CHUNK)
    fwd = np.concatenate([same & (c <= r), same], axis=0)
    bwd = np.concatenate([same & (c >= r), same], axis=0)
    return jnp.asarray(np.stack([fwd, bwd]), BF16)


def _gla_scan(q, k, v, gf, gb, s0f, s0b):
    bsz, t, _ = q.shape
    tb = GLA_BLOCK
    assert t % tb == 0, t
    nblk = t // tb
    nbat = 2 if bsz % 2 == 0 else 1
    fwd = lambda w: pl.BlockSpec((nbat, tb, w), lambda b, i: (b, i, 0))
    bwd = lambda w: pl.BlockSpec((nbat, tb, w), lambda b, i: (b, nblk - 1 - i, 0))
    st_spec = pl.BlockSpec((nbat, GLA_HEADS, GLA_DV_HEAD, GLA_DK_HEAD), lambda b, i: (b, 0, 0, 0))
    st_shape = jax.ShapeDtypeStruct((bsz, GLA_HEADS, GLA_DV_HEAD, GLA_DK_HEAD), F32)
    cum = _cum_matrices()
    return pl.pallas_call(
        _gla_scan_kernel,
        grid=(bsz // nbat, nblk),
        in_specs=[fwd(GLA_DK), fwd(GLA_DK), fwd(GLA_DV), fwd(GLA_DK),
                  bwd(GLA_DK), bwd(GLA_DK), bwd(GLA_DV), bwd(GLA_DK), st_spec, st_spec, _resident(cum.shape)],
        out_specs=[fwd(GLA_DV), bwd(GLA_DV), st_spec, st_spec],
        out_shape=[jax.ShapeDtypeStruct((bsz, t, GLA_DV), F32)] * 2 + [st_shape] * 2,
        scratch_shapes=[pltpu.VMEM((nbat, GLA_HEADS, GLA_DV_HEAD, GLA_DK_HEAD), F32)] * 2,
        compiler_params=_params("parallel", "arbitrary"),
        name="gla_scan",
    )(q, k, v, gf, q, k, v, gb, s0f, s0b, cum)


def _deinterleave(n):
    return np.concatenate([np.arange(0, n, 2), np.arange(1, n, 2)])


def _gather_cols(w, idx):
    idx = np.asarray(idx)
    cols = jnp.take(w, jnp.asarray(np.maximum(idx, 0)), axis=-1)
    return jnp.where(jnp.asarray(idx >= 0), cols, 0.0)


def _mla_slot(nope, rope):
    pad = np.full((16,), -1, np.int64)
    return np.concatenate([nope[:32], rope[:16], pad, nope[32:], rope[16:], pad])


def _win_slot(a, b):
    return np.concatenate([a[:32], b[:32], a[32:], b[32:]])


def _take(vec, idx):
    return jnp.where(jnp.asarray(idx >= 0), vec[jnp.asarray(np.maximum(idx, 0))], 0.0)


def _layer0_layout(p):
    ckv0, kr0 = 0, MLA_KV_RANK
    wk0 = kr0 + MLA_ROPE_DIM
    wv0 = wk0 + WIN_KV_HEADS * WIN_HEAD_DIM
    cq0 = wv0 + WIN_KV_HEADS * WIN_HEAD_DIM
    wq0 = cq0 + MLA_Q_RANK
    de64, de32 = _deinterleave(WIN_HEAD_DIM), _deinterleave(MLA_ROPE_DIM)
    none64, none32 = np.full((64,), -1, np.int64), np.full((32,), -1, np.int64)

    idx = np.full((P0_COLS,), -1, np.int64)
    idx[P0_CKV:P0_CKV + MLA_KV_RANK] = ckv0 + np.arange(MLA_KV_RANK)
    idx[P0_WK:P0_WK + LANES] = _win_slot(wk0 + de64, wk0 + 64 + de64)
    idx[P0_CQ:P0_CQ + MLA_Q_RANK] = cq0 + np.arange(MLA_Q_RANK)
    for j in range(WIN_GROUP):
        idx[P0_WQ + j * LANES:P0_WQ + (j + 1) * LANES] = _win_slot(wq0 + j * 64 + de64,
                                                                   wq0 + (WIN_GROUP + j) * 64 + de64)
    idx[P0_KR:P0_KR + LANES] = _mla_slot(none64, kr0 + de32)
    w_in = _gather_cols(p["w_in"], idx).astype(BF16)
    vpad = np.full((VT_ROWS - MLA_V_DIM,), -1, np.int64)
    vidx = np.concatenate([np.concatenate([wv0 + n * 64 + np.arange(64), vpad]) for n in range(WIN_KV_HEADS)])
    w_wvt = _gather_cols(p["w_in"], vidx).T.astype(BF16)

    per = MLA_NOPE_DIM + MLA_V_DIM
    kidx, qidx, vidx = [], [], []
    for h in range(MLA_HEADS):
        kidx.append(_mla_slot(h * per + np.arange(64), none32))
        qidx.append(_mla_slot(h * MLA_QK_DIM + np.arange(64), h * MLA_QK_DIM + MLA_NOPE_DIM + de32))
        vidx.append(np.concatenate([h * per + MLA_NOPE_DIM + np.arange(64), vpad]))
    w_uk = _gather_cols(p["mla_w_ukv"], np.concatenate(kidx)).astype(BF16)
    w_vt = _gather_cols(p["mla_w_ukv"], np.concatenate(vidx)).T.astype(BF16)
    pat = np.zeros((MLA_HEADS * VT_ROWS, LANES), np.float32)
    pat[MLA_V_DIM::VT_ROWS] = 1.0
    w_uq = _gather_cols(p["mla_w_uq"], np.concatenate(qidx)).astype(BF16)

    lane = np.arange(LANES)
    nope, rot = (lane & 32) == 0, (lane & 48) == 32
    seg_a = (nope[:, None] & nope[None, :]) | (rot[:, None] & rot[None, :])
    seg_b = nope[:, None] == nope[None, :]
    two = lambda m: np.kron(np.eye(2), m.astype(np.float32))
    seg = jnp.asarray(np.stack([two(seg_a), two(seg_b)]), BF16)

    a64, a32 = np.arange(64), np.arange(32)
    twice = lambda v: jnp.concatenate([v, v])
    wide = lambda v: jnp.concatenate([v, jnp.zeros((LANES,), F32)])
    g_wq = _take(p["win_g_q"], _win_slot(de64, de64)) * (WIN_SCALE * LOG2E)
    g_wk = _take(p["win_g_k"], _win_slot(de64, de64))
    g_qslot = jnp.concatenate([p["mla_g_qn"], p["mla_g_qr"]])
    inv_n = jnp.concatenate([jnp.full((64,), 1.0 / MLA_NOPE_DIM, F32), jnp.full((32,), 1.0 / MLA_ROPE_DIM, F32)])
    gv = jnp.stack([
        wide(p["mla_g_kva"]),
        p["mla_g_qa"],
        twice(_take(p["mla_g_kn"], _mla_slot(a64, none32))),
        wide(_take(p["mla_g_kr"], _mla_slot(none64, de32))),
        twice(_take(g_qslot, _mla_slot(a64, 64 + de32))) * (MLA_SCALE * LOG2E),
        twice(g_wq),
        wide(g_wk),
        twice(_take(inv_n, _mla_slot(a64, 64 + a32))),
    ])

    out_rows = np.arange(D_MODEL)
    base = MLA_HEADS * MLA_V_DIM
    for j in range(WIN_GROUP):
        for n in range(WIN_KV_HEADS):
            lo = base + j * LANES + n * 64
            out_rows[lo:lo + 64] = base + (n * WIN_GROUP + j) * 64 + np.arange(64)
    w_out = p["w_out"][jnp.asarray(out_rows)].astype(BF16)
    return (w_in, w_uk, w_vt, w_uq, w_wvt, seg, gv, jnp.asarray(pat)), w_out


def _layer1_layout(p):
    k0, v0 = 0, GLA_DK
    lf0 = v0 + GLA_DV
    lb0 = lf0 + GLA_LOWRANK
    q0 = lb0 + GLA_LOWRANK
    g0 = q0 + GLA_DK
    w = p["w_in"]
    assert (P1_K, P1_V, P1_Q, P1_LOW) == (k0, v0, v0 + GLA_DV, v0 + GLA_DV + GLA_DK)
    pad = jnp.zeros((w.shape[0], P1_G - P1_LOW - 2 * GLA_LOWRANK), w.dtype)
    w_in = jnp.concatenate([w[:, k0:lf0], w[:, q0:g0], w[:, lf0:q0], pad, w[:, g0:]], axis=1).astype(BF16)
    assert w_in.shape[1] == P1_COLS
    wgk = jnp.zeros((LANES, 2 * GLA_DK), F32)
    wgk = wgk.at[:GLA_LOWRANK, :GLA_DK].set(p["w_gk_f"])
    wgk = wgk.at[GLA_LOWRANK:2 * GLA_LOWRANK, GLA_DK:].set(p["w_gk_b"])
    bgk = jnp.concatenate([p["b_gk_f"], p["b_gk_b"]]).reshape(1, 2 * GLA_DK)
    return w_in, wgk.astype(BF16), bgk


def _rope_tables(seq):
    t = np.arange(seq)
    row = (t // GRID_W).astype(np.float32)
    col = (t % GRID_W).astype(np.float32)

    def angles(rot_dim):
        n_freq = rot_dim // 4
        inv = (ROPE_BASE ** (-np.arange(n_freq, dtype=np.float32) / n_freq)).astype(np.float32)
        return np.concatenate([row[:, None] * inv, col[:, None] * inv], axis=-1)

    am, aw = angles(MLA_ROPE_DIM), angles(WIN_HEAD_DIM)
    cos_m, sin_m, cos_w, sin_w = np.cos(am), np.sin(am), np.cos(aw), np.sin(aw)
    one32, one16 = np.ones((seq, 32)), np.ones((seq, 16))
    cm = np.concatenate([one32, cos_m, one16] * 2, axis=-1)
    sm = np.concatenate([0 * one32, -sin_m, 0 * one16, 0 * one32, sin_m, 0 * one16], axis=-1)
    cw = np.concatenate([cos_w] * 4, axis=-1)
    sw = np.concatenate([-sin_w, -sin_w, sin_w, sin_w], axis=-1)
    return tuple(jnp.asarray(a, F32) for a in (cm, sm, cw, sw))


def _identity_tables(rows):
    one, zero = jnp.ones((rows, LANES), F32), jnp.zeros((rows, LANES), F32)
    return one, zero, one, zero


def kernel(x, c, ctx, c_ctx, l0_norm_g, l0_w_mod, l0_b_mod, l0_ffn1_w_gu, l0_ffn1_w_down, l0_ffn2_w_gu, l0_ffn2_w_down, l0_w_in, l0_mla_g_qa, l0_mla_g_kva, l0_mla_w_uq, l0_mla_w_ukv, l0_mla_g_qn, l0_mla_g_qr, l0_mla_g_kn, l0_mla_g_kr, l0_win_g_q, l0_win_g_k, l0_win_sink, l0_w_out, l1_norm_g, l1_w_mod, l1_b_mod, l1_ffn1_w_gu, l1_ffn1_w_down, l1_ffn2_w_gu, l1_ffn2_w_down, l1_w_in, l1_w_gk_f, l1_b_gk_f, l1_w_gk_b, l1_b_gk_b, l1_g_norm, l1_w_out):
    bsz, seq, _ = x.shape
    n_ctx = ctx.shape[1]
    out_dtype = x.dtype
    x = x.astype(F32)
    xc = ctx.astype(F32)
    flat = lambda a: a.reshape(1, bsz * n_ctx, a.shape[-1])
    per_sample = lambda a: a.reshape(bsz, n_ctx, a.shape[-1])

    cc = jnp.zeros((16, D_MODEL), F32).at[:bsz].set(c).at[bsz].set(c_ctx)

    def mods(w_mod, b_mod):
        tab = _mod_table(cc, w_mod, b_mod).reshape(16, N_MOD, D_MODEL)
        return tab[:bsz], tab[bsz:bsz + 1]

    p0 = dict(w_in=l0_w_in, mla_g_qa=l0_mla_g_qa, mla_g_kva=l0_mla_g_kva, mla_w_uq=l0_mla_w_uq,
              mla_w_ukv=l0_mla_w_ukv, mla_g_qn=l0_mla_g_qn, mla_g_qr=l0_mla_g_qr, mla_g_kn=l0_mla_g_kn,
              mla_g_kr=l0_mla_g_kr, win_g_q=l0_win_g_q, win_g_k=l0_win_g_k, w_out=l0_w_out)
    proj_w, w_out0 = _layer0_layout(p0)
    mod, mod_c = mods(l0_w_mod, l0_b_mod)
    wgu1, wd1 = l0_ffn1_w_gu.astype(BF16), l0_ffn1_w_down.astype(BF16)
    wgu2, wd2 = l0_ffn2_w_gu.astype(BF16), l0_ffn2_w_down.astype(BF16)
    sinkrow = jnp.repeat(l0_win_sink.astype(F32) * LOG2E, WIN_BLOCK, axis=1).reshape(WIN_KV_HEADS, 1, -1)

    x = _ffn(x, mod, l0_norm_g, wgu1, wd1, 0)
    xc = per_sample(_ffn(flat(xc), mod_c, l0_norm_g, wgu1, wd1, 0))
    qa, ka, vat, qw, kw, vwt = _proj0(x, mod, l0_norm_g, *proj_w, _rope_tables(seq))
    qa_c, ka_c, vat_c, qw_c, kw_c, vwt_c = _proj0(xc, mod_c, l0_norm_g, *proj_w, _identity_tables(n_ctx))
    vat_c = vat_c.reshape(bsz, MLA_HEADS * VT_ROWS, n_ctx)
    oa = _mla_attn(qa, ka_c, vat_c, ka, vat)
    ow = _win_attn(sinkrow, qw, kw_c, vwt_c, kw, vwt)
    oa_c = _mla_attn(qa_c, ka_c, vat_c)
    ow_c = _win_attn(sinkrow, qw_c, kw_c, vwt_c)
    x = _attn_out_ffn(x, oa, ow, mod, l0_norm_g, w_out0, wgu2, wd2)
    xc = _attn_out_ffn(flat(xc), flat(oa_c), flat(ow_c), mod_c, l0_norm_g, w_out0, wgu2, wd2)

    w_in1, wgk, bgk = _layer1_layout(dict(w_in=l1_w_in, w_gk_f=l1_w_gk_f, b_gk_f=l1_b_gk_f,
                                          w_gk_b=l1_w_gk_b, b_gk_b=l1_b_gk_b))
    mod, mod_c = mods(l1_w_mod, l1_b_mod)
    wgu1, wd1 = l1_ffn1_w_gu.astype(BF16), l1_ffn1_w_down.astype(BF16)
    wgu2, wd2 = l1_ffn2_w_gu.astype(BF16), l1_ffn2_w_down.astype(BF16)

    x = _ffn(x, mod, l1_norm_g, wgu1, wd1, 0)
    xc = _ffn(xc, mod_c, l1_norm_g, wgu1, wd1, 0)
    k1, v1, q1, gt1, gf1, gb1 = _proj1(x, mod, l1_norm_g, w_in1, wgk, bgk)
    k1c, v1c, q1c, _, gf1c, gb1c = [per_sample(a) for a in _proj1(xc, mod_c, l1_norm_g, w_in1, wgk, bgk)]
    zero_state = jnp.zeros((bsz, GLA_HEADS, GLA_DV_HEAD, GLA_DK_HEAD), F32)
    _, _, s_f, s_b = _gla_scan(q1c, k1c, v1c, gf1c, gb1c, zero_state, zero_state)
    o_f, o_b, _, _ = _gla_scan(q1, k1, v1, gf1, gb1, s_f, s_b)
    x = _gla_out_ffn(x, o_f, o_b, gt1, l1_g_norm.reshape(1, GLA_DV_HEAD), mod, l1_norm_g,
                     l1_w_out.astype(BF16), wgu2, wd2)
    return x.astype(out_dtype)
```
